```python
import jax, jax.numpy as jnp
from jax import lax
import numpy as np

D_MODEL = 2048
BATCH = 1
SEQ = 8192
DEPTH = 4

N_MIXERS = 4
D_MIX = D_MODEL
POOL_WINDOWS = (2, 4, 8, 16)
POOL_GROUPS = len(POOL_WINDOWS)
POOL_GROUP_DIM = D_MIX // POOL_GROUPS
CONV_WIDTH = 31
SGU_CHUNK = 128
SGU_HEADS = 8
SGU_HEAD_DIM = D_MIX // SGU_HEADS
SHORT_CONV_WIDTH = 3
D_FF_DENSE = 5632
N_EXPERTS = 8
TOP_K = 2
D_FF_EXPERT = 4096
N_DENSE = (DEPTH + 1) // 2
N_MOE = DEPTH // 2
N_MOD = 6
EPS = 1e-6

kernel_name = "hybrid_pool_conv_sgu_shortconv_moe_block"


def _uses(m):
    return len(range(m, DEPTH, N_MIXERS))


def rmsnorm(x, g):
    xf = x.astype(jnp.float32)
    y = xf * lax.rsqrt(jnp.mean(xf * xf, axis=-1, keepdims=True) + EPS)
    return (y * g.astype(jnp.float32)).astype(x.dtype)


def layernorm(x, g, b):
    xf = x.astype(jnp.float32)
    mu = jnp.mean(xf, axis=-1, keepdims=True)
    var = jnp.mean(jnp.square(xf - mu), axis=-1, keepdims=True)
    y = (xf - mu) * lax.rsqrt(var + EPS) * g.astype(jnp.float32) + b.astype(jnp.float32)
    return y.astype(x.dtype)


def causal_depthwise_conv(h, w):
    k = w.shape[0]
    return lax.conv_general_dilated(
        h, w[:, None, :].astype(h.dtype), window_strides=(1,), padding=[(k - 1, 0)],
        dimension_numbers=("NWC", "WIO", "NWC"), feature_group_count=h.shape[-1])


def pool_mixer(h, w_in, w_grp, layer_scale, w_out):
    b, s, _ = h.shape
    z = h @ w_in
    zf = z.astype(jnp.float32).reshape(b, s, POOL_GROUPS, POOL_GROUP_DIM)
    cs = jnp.cumsum(zf, axis=1)
    t = jnp.arange(s)
    pooled = []
    for gi, win in enumerate(POOL_WINDOWS):
        csg = cs[:, :, gi]
        shifted = jnp.pad(csg, ((0, 0), (win, 0), (0, 0)))[:, :s]
        count = jnp.minimum(t + 1, win).astype(jnp.float32)[None, :, None]
        pooled.append((csg - shifted) / count)
    pooled = (jnp.stack(pooled, axis=2) - zf).astype(z.dtype)
    mixed = jnp.einsum("bsgc,gcd->bsgd", pooled, w_grp).reshape(b, s, D_MIX)
    return (mixed * layer_scale) @ w_out


def conformer_conv_mixer(h, w_in, b_in, dw_w, dw_b, ln_g, ln_b, w_out):
    a, gate = jnp.split(h @ w_in + b_in, 2, axis=-1)
    z = a * jax.nn.sigmoid(gate)
    z = causal_depthwise_conv(z, dw_w) + dw_b
    z = layernorm(z, ln_g, ln_b)
    return jax.nn.silu(z) @ w_out


def sgu_mixer(h, w_in, b_in, ln_g, ln_b, w_s, b_s, w_out):
    b, s, _ = h.shape
    a = jax.nn.gelu(h @ w_in + b_in, approximate=False)
    u, v = jnp.split(a, 2, axis=-1)
    v = layernorm(v, ln_g, ln_b).reshape(b, s // SGU_CHUNK, SGU_CHUNK, SGU_HEADS, SGU_HEAD_DIM)
    causal = jnp.tril(jnp.ones((SGU_CHUNK, SGU_CHUNK), dtype=bool))
    w_causal = jnp.where(causal[None], w_s, 0.0).astype(v.dtype)
    sv = jnp.einsum("hts,bnshc->bnthc", w_causal, v) + b_s.T[None, None, :, :, None]
    return (u * sv.reshape(b, s, D_MIX)) @ w_out


def short_conv_mixer(h, w_in, conv_w, w_out):
    bg, cg, z = jnp.split(h @ w_in, 3, axis=-1)
    return (bg * causal_depthwise_conv(cg * z, conv_w)) @ w_out


def swiglu(h, w1, w3, w2):
    return (jax.nn.silu(h @ w1) * (h @ w3)) @ w2


def moe_swiglu(h, w_router, w1, w3, w2):
    logits = (h @ w_router).astype(jnp.float32)
    top_v, top_i = lax.top_k(logits, TOP_K)
    top_w = jax.nn.softmax(top_v, axis=-1)
    combine = jnp.sum(jax.nn.one_hot(top_i, N_EXPERTS, dtype=jnp.float32) * top_w[..., None], axis=-2)
    combine = combine.astype(h.dtype)
    out = jnp.zeros_like(h)
    for e in range(N_EXPERTS):
        out = out + combine[..., e:e + 1] * swiglu(h, w1[e], w3[e], w2[e])
    return out


def setup_inputs(seed: int = 0) -> dict:
    key = jax.random.key(seed)
    keys = iter(jax.random.split(key, 64))

    def nrm(shape, scale):
        return scale * jax.random.normal(next(keys), shape, jnp.float32)

    def gain(shape):
        return 1.0 + nrm(shape, 0.02)

    d, dm = D_MODEL, D_MIX
    n0, n1, n2, n3 = _uses(0), _uses(1), _uses(2), _uses(3)
    return {
        "x": nrm((BATCH, SEQ, d), 1.0),
        "c": nrm((BATCH, d), 1.0),
        "ada_w": nrm((DEPTH, d, N_MOD * d), 0.5 * d ** -0.5),
        "ada_b": nrm((DEPTH, N_MOD * d), 0.02),
        "norm1_g": gain((DEPTH, d)),
        "norm2_g": gain((DEPTH, d)),
        "pool_w_in": nrm((n0, d, dm), d ** -0.5),
        "pool_w_grp": nrm((n0, POOL_GROUPS, POOL_GROUP_DIM, POOL_GROUP_DIM), POOL_GROUP_DIM ** -0.5),
        "pool_scale": gain((n0, dm)),
        "pool_w_out": nrm((n0, dm, d), dm ** -0.5),
        "conv_w_in": nrm((n1, d, 2 * dm), d ** -0.5),
        "conv_b_in": nrm((n1, 2 * dm), 0.02),
        "conv_dw_w": nrm((n1, CONV_WIDTH, dm), CONV_WIDTH ** -0.5),
        "conv_dw_b": nrm((n1, dm), 0.02),
        "conv_ln_g": gain((n1, dm)),
        "conv_ln_b": nrm((n1, dm), 0.02),
        "conv_w_out": nrm((n1, dm, d), dm ** -0.5),
        "sgu_w_in": nrm((n2, d, 2 * dm), d ** -0.5),
        "sgu_b_in": nrm((n2, 2 * dm), 0.02),
        "sgu_ln_g": gain((n2, dm)),
        "sgu_ln_b": nrm((n2, dm), 0.02),
        "sgu_w_s": nrm((n2, SGU_HEADS, SGU_CHUNK, SGU_CHUNK), SGU_CHUNK ** -0.5),
        "sgu_b_s": gain((n2, SGU_HEADS, SGU_CHUNK)),
        "sgu_w_out": nrm((n2, dm, d), dm ** -0.5),
        "sconv_w_in": nrm((n3, d, 3 * dm), d ** -0.5),
        "sconv_w": nrm((n3, SHORT_CONV_WIDTH, dm), SHORT_CONV_WIDTH ** -0.5),
        "sconv_w_out": nrm((n3, dm, d), dm ** -0.5),
        "ffn_w1": nrm((N_DENSE, d, D_FF_DENSE), d ** -0.5),
        "ffn_w3": nrm((N_DENSE, d, D_FF_DENSE), d ** -0.5),
        "ffn_w2": nrm((N_DENSE, D_FF_DENSE, d), D_FF_DENSE ** -0.5),
        "moe_router": nrm((N_MOE, d, N_EXPERTS), d ** -0.5),
        "moe_w1": nrm((N_MOE, N_EXPERTS, d, D_FF_EXPERT), d ** -0.5),
        "moe_w3": nrm((N_MOE, N_EXPERTS, d, D_FF_EXPERT), d ** -0.5),
        "moe_w2": nrm((N_MOE, N_EXPERTS, D_FF_EXPERT, d), D_FF_EXPERT ** -0.5),
        "final_g": gain((d,)),
    }


def reference(x, c, ada_w, ada_b, norm1_g, norm2_g,
              pool_w_in, pool_w_grp, pool_scale, pool_w_out,
              conv_w_in, conv_b_in, conv_dw_w, conv_dw_b, conv_ln_g, conv_ln_b, conv_w_out,
              sgu_w_in, sgu_b_in, sgu_ln_g, sgu_ln_b, sgu_w_s, sgu_b_s, sgu_w_out,
              sconv_w_in, sconv_w, sconv_w_out,
              ffn_w1, ffn_w3, ffn_w2,
              moe_router, moe_w1, moe_w3, moe_w2,
              final_g):
    c_act = jax.nn.silu(c)
    for i in range(DEPTH):
        mod = c_act @ ada_w[i] + ada_b[i]
        sh1, sc1, g1, sh2, sc2, g2 = [m[:, None, :] for m in jnp.split(mod, N_MOD, axis=-1)]

        h = rmsnorm(x, norm1_g[i]) * (1.0 + sc1) + sh1
        m, j = i % N_MIXERS, i // N_MIXERS
        if m == 0:
            y = pool_mixer(h, pool_w_in[j], pool_w_grp[j], pool_scale[j], pool_w_out[j])
        elif m == 1:
            y = conformer_conv_mixer(h, conv_w_in[j], conv_b_in[j], conv_dw_w[j], conv_dw_b[j],
                                     conv_ln_g[j], conv_ln_b[j], conv_w_out[j])
        elif m == 2:
            y = sgu_mixer(h, sgu_w_in[j], sgu_b_in[j], sgu_ln_g[j], sgu_ln_b[j],
                          sgu_w_s[j], sgu_b_s[j], sgu_w_out[j])
        else:
            y = short_conv_mixer(h, sconv_w_in[j], sconv_w[j], sconv_w_out[j])
        x = x + g1 * y

        h = rmsnorm(x, norm2_g[i]) * (1.0 + sc2) + sh2
        k = i // 2
        if i % 2 == 0:
            y = swiglu(h, ffn_w1[k], ffn_w3[k], ffn_w2[k])
        else:
            y = moe_swiglu(h, moe_router[k], moe_w1[k], moe_w3[k], moe_w2[k])
        x = x + g2 * y
    return rmsnorm(x, final_g)
```

```python
import functools
import math

import jax
import jax.numpy as jnp
from jax import lax
from jax.experimental import pallas as pl
from jax.experimental.pallas import tpu as pltpu

EPS = 1e-6
POOL_WINDOWS = (2, 4, 8, 16)
CONV_WIDTH = 31
SGU_CHUNK = 128
SGU_HEADS = 8
SHORT_CONV_WIDTH = 3
N_EXPERTS = 8
LANES = 128
SUBLANES = 8
MIB = 1024 * 1024

BF16 = jnp.bfloat16
F32 = jnp.float32


def _params(vmem_mib, ndims):
    return pltpu.CompilerParams(
        dimension_semantics=("arbitrary",) * ndims,
        vmem_limit_bytes=vmem_mib * MIB)


def _ada_body(c_ref, w_ref, b_ref, o_ref):
    c_act = jax.nn.silu(c_ref[...])
    o_ref[0] = jnp.sum(w_ref[0] * c_act, axis=0, keepdims=True) + b_ref[0]


def _ada_mod(c, ada_w, ada_b):
    depth, d, n = ada_w.shape
    tn = 1024
    return pl.pallas_call(
        _ada_body,
        grid=(depth, n // tn),
        in_specs=[
            pl.BlockSpec((d, 1), lambda i, j: (0, 0)),
            pl.BlockSpec((1, d, tn), lambda i, j: (i, 0, j)),
            pl.BlockSpec((1, 1, tn), lambda i, j: (i, 0, j)),
        ],
        out_specs=pl.BlockSpec((1, 1, tn), lambda i, j: (i, 0, j)),
        out_shape=jax.ShapeDtypeStruct((depth, 1, n), F32),
        compiler_params=_params(40, 2),
        name="ada_mod",
    )(c.reshape(d, 1), ada_w, ada_b.reshape(depth, 1, n))


def _norm_mod_body(x_ref, g_ref, sc_ref, sh_ref, o_ref):
    x = x_ref[...]
    y = x * lax.rsqrt(jnp.mean(x * x, axis=-1, keepdims=True) + EPS)
    o_ref[...] = ((y * g_ref[...]) * (1.0 + sc_ref[...]) + sh_ref[...]).astype(o_ref.dtype)


def _norm_mod(x, g, sc, sh):
    m, d = x.shape
    tm = 512
    row = pl.BlockSpec((1, d), lambda i: (0, 0))
    return pl.pallas_call(
        _norm_mod_body,
        grid=(m // tm,),
        in_specs=[pl.BlockSpec((tm, d), lambda i: (i, 0)), row, row, row],
        out_specs=pl.BlockSpec((tm, d), lambda i: (i, 0)),
        out_shape=jax.ShapeDtypeStruct((m, d), BF16),
        compiler_params=_params(40, 1),
        name="norm_mod",
    )(x, g, sc, sh)


def _final_norm_body(x_ref, g_ref, o_ref):
    x = x_ref[...]
    y = x * lax.rsqrt(jnp.mean(x * x, axis=-1, keepdims=True) + EPS)
    o_ref[...] = y * g_ref[...]


def _final_norm(x, g):
    m, d = x.shape
    tm = 512
    return pl.pallas_call(
        _final_norm_body,
        grid=(m // tm,),
        in_specs=[pl.BlockSpec((tm, d), lambda i: (i, 0)), pl.BlockSpec((1, d), lambda i: (0, 0))],
        out_specs=pl.BlockSpec((tm, d), lambda i: (i, 0)),
        out_shape=jax.ShapeDtypeStruct((m, d), F32),
        compiler_params=_params(40, 1),
        name="final_norm",
    )(x, g)


def _w_spec(w, lead, k, tn, off):
    assert w.shape[len(lead)] == k
    return pl.BlockSpec((None,) * len(lead) + (k, tn), lambda n, m: (*lead, 0, n + off))


def _row_spec(a, lead, tn, off=0):
    return pl.BlockSpec((None,) * len(lead) + (1, tn), lambda n, m: (*lead, 0, n + off))


def _colmm(x, weights, raw_inputs, outs, epilogue, *, tm, tn, nt, scratch=(), vmem_mib, name):
    m_rows, k = x.shape
    nw, nr, no = len(weights), len(raw_inputs), len(outs)

    def body(*refs):
        x_ref = refs[0]
        w_refs = refs[1:1 + nw]
        r_refs = refs[1 + nw:1 + nw + nr]
        o_refs = refs[1 + nw + nr:1 + nw + nr + no]
        s_refs = refs[1 + nw + nr + no:]
        wb_refs, extra = s_refs[:nw], s_refs[nw:]

        @pl.when(pl.program_id(1) == 0)
        def _():
            for w_ref, wb in zip(w_refs, wb_refs):
                wb[...] = w_ref[...].astype(BF16)

        xv = x_ref[...]
        accs = [jnp.dot(xv, wb[...], preferred_element_type=F32) for wb in wb_refs]
        epilogue(accs, r_refs, o_refs, extra)

    in_specs = [pl.BlockSpec((tm, k), lambda n, m: (m, 0))]
    in_specs += [_w_spec(w, lead, k, tn, off) for (w, lead, off) in weights]
    in_specs += [spec for (_, spec) in raw_inputs]
    return pl.pallas_call(
        body,
        grid=(nt, m_rows // tm),
        in_specs=in_specs,
        out_specs=[spec for (_, spec) in outs],
        out_shape=[sds for (sds, _) in outs],
        scratch_shapes=[pltpu.VMEM((k, tn), BF16) for _ in weights] + list(scratch),
        compiler_params=_params(vmem_mib, 2),
        name=name,
    )(x, *[w for (w, _, _) in weights], *[a for (a, _) in raw_inputs])


def _tile_spec(tm, tn):
    return pl.BlockSpec((tm, tn), lambda n, m: (m, n))


def _mm_residual(a, w, lead, x_res, gate, row_scale=None, *, tm, tn, vmem_mib, name):
    m_rows = a.shape[0]
    n_cols = x_res.shape[1]
    raw = [(gate, _row_spec(gate, (), tn)), (x_res, _tile_spec(tm, tn))]
    if row_scale is not None:
        raw.append((row_scale, pl.BlockSpec((tm, 1), lambda n, m: (m, 0))))

    def epilogue(accs, r_refs, o_refs, _):
        y = accs[0] * r_refs[0][...]
        if row_scale is not None:
            y = y * r_refs[2][...]
        o_refs[0][...] = r_refs[1][...] + y

    out = _colmm(a, [(w, lead, 0)], raw,
                 [(jax.ShapeDtypeStruct((m_rows, n_cols), F32), _tile_spec(tm, tn))],
                 epilogue, tm=tm, tn=tn, nt=n_cols // tn, vmem_mib=vmem_mib, name=name)
    return out[0]


def _swiglu_up(h, w1, w3, lead, *, tm, tn, vmem_mib, name):
    m_rows = h.shape[0]
    n_cols = w1.shape[-1]

    def epilogue(accs, r_refs, o_refs, _):
        o_refs[0][...] = (jax.nn.silu(accs[0]) * accs[1]).astype(BF16)

    out = _colmm(h, [(w1, lead, 0), (w3, lead, 0)], [],
                 [(jax.ShapeDtypeStruct((m_rows, n_cols), BF16), _tile_spec(tm, tn))],
                 epilogue, tm=tm, tn=tn, nt=n_cols // tn, vmem_mib=vmem_mib, name=name)
    return out[0]


def _pool_in(h, w_in, w_grp, scale, j):
    m_rows, k = h.shape
    n_groups = len(POOL_WINDOWS)
    tn = w_in.shape[-1] // n_groups
    tm = 1024
    halo = max(POOL_WINDOWS)

    def epilogue(accs, r_refs, o_refs, s_refs):
        z = accs[0]
        wg_ref, scale_ref = r_refs
        zext, wgb, pooled = s_refs
        n, m = pl.program_id(0), pl.program_id(1)

        @pl.when(m == 0)
        def _():
            zext[0:halo, :] = jnp.zeros((halo, tn), F32)
            wgb[...] = wg_ref[...].astype(BF16)

        zext[halo:halo + tm, :] = z
        t = m * tm + lax.broadcasted_iota(jnp.int32, (tm, tn), 0)
        for gi, win in enumerate(POOL_WINDOWS):
            @pl.when(n == gi)
            def _(win=win):
                wsum = z
                for back in range(1, win):
                    wsum = wsum + zext[halo - back:halo - back + tm, :]
                count = jnp.minimum(t + 1, win).astype(F32)
                pooled[...] = (wsum / count - z).astype(BF16)
        zext[0:halo, :] = zext[tm:tm + halo, :]
        mixed = jnp.dot(pooled[...], wgb[...], preferred_element_type=F32)
        o_refs[0][...] = (mixed * scale_ref[...]).astype(BF16)

    raw = [(w_grp, pl.BlockSpec((None, None, tn, tn), lambda n, m: (j, n, 0, 0))),
           (scale, _row_spec(scale, (), tn))]
    out = _colmm(h, [(w_in, (j,), 0)], raw,
                 [(jax.ShapeDtypeStruct((m_rows, n_groups * tn), BF16), _tile_spec(tm, tn))],
                 epilogue, tm=tm, tn=tn, nt=n_groups,
                 scratch=[pltpu.VMEM((tm + halo, tn), F32), pltpu.VMEM((tn, tn), BF16),
                          pltpu.VMEM((tm, tn), BF16)],
                 vmem_mib=48, name="pool_in")
    return out[0]


def _glu_in(h, w_in, b_in, j):
    m_rows, k = h.shape
    dm = w_in.shape[-1] // 2
    tm, tn = 1024, 512
    nt = dm // tn

    def epilogue(accs, r_refs, o_refs, _):
        a = accs[0] + r_refs[0][...]
        g = accs[1] + r_refs[1][...]
        o_refs[0][...] = (a * jax.nn.sigmoid(g)).astype(BF16)

    raw = [(b_in, _row_spec(b_in, (), tn)), (b_in, _row_spec(b_in, (), tn, off=nt))]
    out = _colmm(h, [(w_in, (j,), 0), (w_in, (j,), nt)], raw,
                 [(jax.ShapeDtypeStruct((m_rows, dm), BF16), _tile_spec(tm, tn))],
                 epilogue, tm=tm, tn=tn, nt=nt, vmem_mib=48, name="glu_in")
    return out[0]


def _conv_ln_body(z_ref, w_ref, cb_ref, g_ref, b_ref, o_ref, zext, conv, *, tm, halo, rows, cols):
    d = z_ref.shape[1]

    @pl.when(pl.program_id(0) == 0)
    def _():
        zext[0:halo, :] = jnp.zeros((halo, d), F32)

    zext[halo:halo + tm, :] = z_ref[...].astype(F32)
    first = halo - (CONV_WIDTH - 1)
    for c0 in range(0, d, cols):
        for r0 in range(0, tm, rows):
            acc = jnp.zeros((rows, cols), F32)
            for kk in range(CONV_WIDTH):
                acc = acc + w_ref[kk:kk + 1, c0:c0 + cols] * zext[first + kk + r0:first + kk + r0 + rows, c0:c0 + cols]
            conv[r0:r0 + rows, c0:c0 + cols] = acc + cb_ref[:, c0:c0 + cols]
    zext[0:halo, :] = zext[tm:tm + halo, :]

    y = conv[...]
    mu = jnp.mean(y, axis=-1, keepdims=True)
    var = jnp.mean(jnp.square(y - mu), axis=-1, keepdims=True)
    yn = (y - mu) * lax.rsqrt(var + EPS) * g_ref[...] + b_ref[...]
    o_ref[...] = jax.nn.silu(yn).astype(o_ref.dtype)


def _conv_ln(z, dw_w, dw_b, ln_g, ln_b, j):
    m_rows, d = z.shape
    tm, halo = 256, 32
    row = pl.BlockSpec((1, d), lambda i: (0, 0))
    return pl.pallas_call(
        functools.partial(_conv_ln_body, tm=tm, halo=halo, rows=64, cols=512),
        grid=(m_rows // tm,),
        in_specs=[pl.BlockSpec((tm, d), lambda i: (i, 0)),
                  pl.BlockSpec((None, CONV_WIDTH, d), lambda i: (j, 0, 0)), row, row, row],
        out_specs=pl.BlockSpec((tm, d), lambda i: (i, 0)),
        out_shape=jax.ShapeDtypeStruct((m_rows, d), BF16),
        scratch_shapes=[pltpu.VMEM((tm + halo, d), F32), pltpu.VMEM((tm, d), F32)],
        compiler_params=_params(40, 1),
        name="conv_ln",
    )(z, dw_w, dw_b, ln_g, ln_b)


def _gelu_in(h, w_in, b_in, j):
    m_rows, k = h.shape
    n_cols = w_in.shape[-1]
    tm, tn = 512, 1024

    def epilogue(accs, r_refs, o_refs, _):
        a = accs[0] + r_refs[0][...]
        o_refs[0][...] = (0.5 * a * (1.0 + lax.erf(a * math.sqrt(0.5)))).astype(BF16)

    out = _colmm(h, [(w_in, (j,), 0)], [(b_in, _row_spec(b_in, (), tn))],
                 [(jax.ShapeDtypeStruct((m_rows, n_cols), BF16), _tile_spec(tm, tn))],
                 epilogue, tm=tm, tn=tn, nt=n_cols // tn, vmem_mib=48, name="gelu_in")
    return out[0]


def _sgu_gate_body(u_ref, v_ref, g_ref, b_ref, ws_ref, bst_ref, o_ref, *, tm):
    v = v_ref[...].astype(F32)
    mu = jnp.mean(v, axis=-1, keepdims=True)
    var = jnp.mean(jnp.square(v - mu), axis=-1, keepdims=True)
    vn = ((v - mu) * lax.rsqrt(var + EPS) * g_ref[...] + b_ref[...]).astype(BF16)
    hd = v.shape[1] // SGU_HEADS
    tri = (lax.broadcasted_iota(jnp.int32, (SGU_CHUNK, SGU_CHUNK), 0)
           >= lax.broadcasted_iota(jnp.int32, (SGU_CHUNK, SGU_CHUNK), 1))
    for hh in range(SGU_HEADS):
        wc = jnp.where(tri, ws_ref[hh], 0.0).astype(BF16)
        bias = bst_ref[:, hh:hh + 1]
        for ck in range(tm // SGU_CHUNK):
            rs = slice(ck * SGU_CHUNK, (ck + 1) * SGU_CHUNK)
            cs = slice(hh * hd, (hh + 1) * hd)
            sv = jnp.dot(wc, vn[rs, cs], preferred_element_type=F32) + bias
            o_ref[rs, cs] = (u_ref[rs, cs].astype(F32) * sv).astype(o_ref.dtype)


def _sgu_gate(a, ln_g, ln_b, w_s, b_s_t, j):
    m_rows = a.shape[0]
    d = a.shape[1] // 2
    tm = 256
    row = pl.BlockSpec((1, d), lambda i: (0, 0))
    return pl.pallas_call(
        functools.partial(_sgu_gate_body, tm=tm),
        grid=(m_rows // tm,),
        in_specs=[pl.BlockSpec((tm, d), lambda i: (i, 0)), pl.BlockSpec((tm, d), lambda i: (i, 1)),
                  row, row,
                  pl.BlockSpec((None, SGU_HEADS, SGU_CHUNK, SGU_CHUNK), lambda i: (j, 0, 0, 0)),
                  pl.BlockSpec((SGU_CHUNK, SGU_HEADS), lambda i: (0, 0))],
        out_specs=pl.BlockSpec((tm, d), lambda i: (i, 0)),
        out_shape=jax.ShapeDtypeStruct((m_rows, d), BF16),
        compiler_params=_params(40, 1),
        name="sgu_gate",
    )(a, a, ln_g, ln_b, w_s, b_s_t)


def _sconv_in(h, w_in, conv_w, j):
    m_rows, k = h.shape
    dm = w_in.shape[-1] // 3
    tm, tn = 1024, 256
    nt = dm // tn
    halo = SUBLANES

    def epilogue(accs, r_refs, o_refs, s_refs):
        bg, cg, z = accs
        cw_ref = r_refs[0]
        ext = s_refs[0]

        @pl.when(pl.program_id(1) == 0)
        def _():
            ext[0:halo, :] = jnp.zeros((halo, tn), F32)

        cz = cg * z
        ext[halo:halo + tm, :] = cz
        conv = cw_ref[SHORT_CONV_WIDTH - 1:SHORT_CONV_WIDTH, :] * cz
        for back in range(1, SHORT_CONV_WIDTH):
            tap = SHORT_CONV_WIDTH - 1 - back
            conv = conv + cw_ref[tap:tap + 1, :] * ext[halo - back:halo - back + tm, :]
        ext[0:halo, :] = ext[tm:tm + halo, :]
        o_refs[0][...] = (bg * conv).astype(BF16)

    raw = [(conv_w, pl.BlockSpec((None, SHORT_CONV_WIDTH, tn), lambda n, m: (j, 0, n)))]
    out = _colmm(h, [(w_in, (j,), 0), (w_in, (j,), nt), (w_in, (j,), 2 * nt)], raw,
                 [(jax.ShapeDtypeStruct((m_rows, dm), BF16), _tile_spec(tm, tn))],
                 epilogue, tm=tm, tn=tn, nt=nt,
                 scratch=[pltpu.VMEM((tm + halo, tn), F32)], vmem_mib=48, name="sconv_in")
    return out[0]


def _router_body(h_ref, w_ref, o_ref):
    logits = jnp.dot(h_ref[...], w_ref[...].astype(BF16), preferred_element_type=F32)
    lane = lax.broadcasted_iota(jnp.int32, logits.shape, 1)
    neg = jnp.float32(-jnp.inf)
    lg = jnp.where(lane < N_EXPERTS, logits, neg)
    v1 = jnp.max(lg, axis=-1, keepdims=True)
    lane_f = lane.astype(F32)
    i1 = jnp.min(jnp.where(lg == v1, lane_f, float(LANES)), axis=-1, keepdims=True)
    lg2 = jnp.where(lane_f == i1, neg, lg)
    v2 = jnp.max(lg2, axis=-1, keepdims=True)
    i2 = jnp.min(jnp.where(lg2 == v2, lane_f, float(LANES)), axis=-1, keepdims=True)
    e2 = jnp.exp(v2 - v1)
    den = 1.0 + e2
    o_ref[...] = jnp.where(lane_f == i1, 1.0 / den, 0.0) + jnp.where(lane_f == i2, e2 / den, 0.0)


def _router(h, w_router_padded):
    m_rows, k = h.shape
    tm = 512
    return pl.pallas_call(
        _router_body,
        grid=(m_rows // tm,),
        in_specs=[pl.BlockSpec((tm, k), lambda i: (i, 0)), pl.BlockSpec((k, LANES), lambda i: (0, 0))],
        out_specs=pl.BlockSpec((tm, LANES), lambda i: (i, 0)),
        out_shape=jax.ShapeDtypeStruct((m_rows, LANES), F32),
        compiler_params=_params(40, 1),
        name="router",
    )(h, w_router_padded)


def kernel(x, c, ada_w, ada_b, norm1_g, norm2_g, pool_w_in, pool_w_grp, pool_scale, pool_w_out, conv_w_in, conv_b_in, conv_dw_w, conv_dw_b, conv_ln_g, conv_ln_b, conv_w_out, sgu_w_in, sgu_b_in, sgu_ln_g, sgu_ln_b, sgu_w_s, sgu_b_s, sgu_w_out, sconv_w_in, sconv_w, sconv_w_out, ffn_w1, ffn_w3, ffn_w2, moe_router, moe_w1, moe_w3, moe_w2, final_g):
    batch, seq, d = x.shape
    assert batch == 1
    depth = ada_w.shape[0]
    xs = x.reshape(seq, d)
    mod = _ada_mod(c, ada_w, ada_b)

    for i in range(depth):
        sh1, sc1, g1, sh2, sc2, g2 = [mod[i, :, q * d:(q + 1) * d] for q in range(6)]

        h = _norm_mod(xs, norm1_g[i:i + 1], sc1, sh1)
        mixer, j = i % 4, i // 4
        if mixer == 0:
            a = _pool_in(h, pool_w_in, pool_w_grp, pool_scale[j:j + 1], j)
            w_out = pool_w_out
        elif mixer == 1:
            z = _glu_in(h, conv_w_in, conv_b_in[j:j + 1], j)
            a = _conv_ln(z, conv_dw_w, conv_dw_b[j:j + 1], conv_ln_g[j:j + 1], conv_ln_b[j:j + 1], j)
            w_out = conv_w_out
        elif mixer == 2:
            a = _gelu_in(h, sgu_w_in, sgu_b_in[j:j + 1], j)
            a = _sgu_gate(a, sgu_ln_g[j:j + 1], sgu_ln_b[j:j + 1], sgu_w_s, sgu_b_s[j].T, j)
            w_out = sgu_w_out
        else:
            a = _sconv_in(h, sconv_w_in, sconv_w, j)
            w_out = sconv_w_out
        xs = _mm_residual(a, w_out, (j,), xs, g1, tm=512, tn=1024, vmem_mib=48, name="mixer_out")

        h = _norm_mod(xs, norm2_g[i:i + 1], sc2, sh2)
        kk = i // 2
        if i % 2 == 0:
            hid = _swiglu_up(h, ffn_w1, ffn_w3, (kk,), tm=1024, tn=512, vmem_mib=48, name="ffn_up")
            xs = _mm_residual(hid, ffn_w2, (kk,), xs, g2, tm=512, tn=512, vmem_mib=56, name="ffn_down")
        else:
            w_r = jnp.pad(moe_router[kk], ((0, 0), (0, LANES - N_EXPERTS)))
            combine = _router(h, w_r)
            for e in range(N_EXPERTS):
                hid = _swiglu_up(h, moe_w1, moe_w3, (kk, e), tm=1024, tn=512, vmem_mib=48, name="moe_up")
                xs = _mm_residual(hid, moe_w2, (kk, e), xs, g2, combine[:, e:e + 1],
                                  tm=512, tn=512, vmem_mib=56, name="moe_down")
    return _final_norm(xs, final_g.reshape(1, d)).reshape(batch, seq, d)
```

```python
import functools
import math

import jax
import jax.numpy as jnp
from jax import lax
from jax.experimental import pallas as pl
from jax.experimental.pallas import tpu as pltpu

EPS = 1e-6
POOL_WINDOWS = (2, 4, 8, 16)
CONV_WIDTH = 31
SGU_CHUNK = 128
SGU_HEADS = 8
SHORT_CONV_WIDTH = 3
N_EXPERTS = 8
MOE_BLOCK = 256
LANES = 128
SUBLANES = 8
MIB = 1024 * 1024

BF16 = jnp.bfloat16
F32 = jnp.float32


def _params(vmem_mib, ndims):
    return pltpu.CompilerParams(
        dimension_semantics=("arbitrary",) * ndims,
        vmem_limit_bytes=vmem_mib * MIB)


def _ada_body(c_ref, w_ref, b_ref, o_ref):
    c_act = jax.nn.silu(c_ref[...])
    o_ref[0] = jnp.sum(w_ref[0] * c_act, axis=0, keepdims=True) + b_ref[0]


def _ada_mod(c, ada_w, ada_b):
    depth, d, n = ada_w.shape
    tn = 1024
    return pl.pallas_call(
        _ada_body,
        grid=(depth, n // tn),
        in_specs=[
            pl.BlockSpec((d, 1), lambda i, j: (0, 0)),
            pl.BlockSpec((1, d, tn), lambda i, j: (i, 0, j)),
            pl.BlockSpec((1, 1, tn), lambda i, j: (i, 0, j)),
        ],
        out_specs=pl.BlockSpec((1, 1, tn), lambda i, j: (i, 0, j)),
        out_shape=jax.ShapeDtypeStruct((depth, 1, n), F32),
        compiler_params=_params(40, 2),
        name="ada_mod",
    )(c.reshape(d, 1), ada_w, ada_b.reshape(depth, 1, n))


def _norm_mod_body(x_ref, g_ref, sc_ref, sh_ref, o_ref):
    x = x_ref[...]
    y = x * lax.rsqrt(jnp.mean(x * x, axis=-1, keepdims=True) + EPS)
    o_ref[...] = ((y * g_ref[...]) * (1.0 + sc_ref[...]) + sh_ref[...]).astype(o_ref.dtype)


def _norm_mod(x, g, sc, sh):
    m, d = x.shape
    tm = 512
    row = pl.BlockSpec((1, d), lambda i: (0, 0))
    return pl.pallas_call(
        _norm_mod_body,
        grid=(m // tm,),
        in_specs=[pl.BlockSpec((tm, d), lambda i: (i, 0)), row, row, row],
        out_specs=pl.BlockSpec((tm, d), lambda i: (i, 0)),
        out_shape=jax.ShapeDtypeStruct((m, d), BF16),
        compiler_params=_params(40, 1),
        name="norm_mod",
    )(x, g, sc, sh)


def _final_norm_body(x_ref, g_ref, o_ref):
    x = x_ref[...]
    y = x * lax.rsqrt(jnp.mean(x * x, axis=-1, keepdims=True) + EPS)
    o_ref[...] = y * g_ref[...]


def _final_norm(x, g):
    m, d = x.shape
    tm = 512
    return pl.pallas_call(
        _final_norm_body,
        grid=(m // tm,),
        in_specs=[pl.BlockSpec((tm, d), lambda i: (i, 0)), pl.BlockSpec((1, d), lambda i: (0, 0))],
        out_specs=pl.BlockSpec((tm, d), lambda i: (i, 0)),
        out_shape=jax.ShapeDtypeStruct((m, d), F32),
        compiler_params=_params(40, 1),
        name="final_norm",
    )(x, g)


def _w_spec(w, lead, k, tn, off):
    assert w.shape[len(lead)] == k
    return pl.BlockSpec((None,) * len(lead) + (k, tn), lambda n, m: (*lead, 0, n + off))


def _row_spec(a, lead, tn, off=0):
    return pl.BlockSpec((None,) * len(lead) + (1, tn), lambda n, m: (*lead, 0, n + off))


def _colmm(x, weights, raw_inputs, outs, epilogue, *, tm, tn, nt, scratch=(), vmem_mib, name):
    m_rows, k = x.shape
    nw, nr, no = len(weights), len(raw_inputs), len(outs)

    def body(*refs):
        x_ref = refs[0]
        w_refs = refs[1:1 + nw]
        r_refs = refs[1 + nw:1 + nw + nr]
        o_refs = refs[1 + nw + nr:1 + nw + nr + no]
        s_refs = refs[1 + nw + nr + no:]
        wb_refs, extra = s_refs[:nw], s_refs[nw:]

        @pl.when(pl.program_id(1) == 0)
        def _():
            for w_ref, wb in zip(w_refs, wb_refs):
                wb[...] = w_ref[...].astype(BF16)

        xv = x_ref[...]
        accs = [jnp.dot(xv, wb[...], preferred_element_type=F32) for wb in wb_refs]
        epilogue(accs, r_refs, o_refs, extra)

    in_specs = [pl.BlockSpec((tm, k), lambda n, m: (m, 0))]
    in_specs += [_w_spec(w, lead, k, tn, off) for (w, lead, off) in weights]
    in_specs += [spec for (_, spec) in raw_inputs]
    return pl.pallas_call(
        body,
        grid=(nt, m_rows // tm),
        in_specs=in_specs,
        out_specs=[spec for (_, spec) in outs],
        out_shape=[sds for (sds, _) in outs],
        scratch_shapes=[pltpu.VMEM((k, tn), BF16) for _ in weights] + list(scratch),
        compiler_params=_params(vmem_mib, 2),
        name=name,
    )(x, *[w for (w, _, _) in weights], *[a for (a, _) in raw_inputs])


def _tile_spec(tm, tn):
    return pl.BlockSpec((tm, tn), lambda n, m: (m, n))


def _mm_residual(a, w, lead, x_res, gate, row_scale=None, *, tm, tn, vmem_mib, name):
    m_rows = a.shape[0]
    n_cols = x_res.shape[1]
    raw = [(gate, _row_spec(gate, (), tn)), (x_res, _tile_spec(tm, tn))]
    if row_scale is not None:
        raw.append((row_scale, pl.BlockSpec((tm, 1), lambda n, m: (m, 0))))

    def epilogue(accs, r_refs, o_refs, _):
        y = accs[0] * r_refs[0][...]
        if row_scale is not None:
            y = y * r_refs[2][...]
        o_refs[0][...] = r_refs[1][...] + y

    out = _colmm(a, [(w, lead, 0)], raw,
                 [(jax.ShapeDtypeStruct((m_rows, n_cols), F32), _tile_spec(tm, tn))],
                 epilogue, tm=tm, tn=tn, nt=n_cols // tn, vmem_mib=vmem_mib, name=name)
    return out[0]


def _swiglu_up(h, w1, w3, lead, *, tm, tn, vmem_mib, name):
    m_rows = h.shape[0]
    n_cols = w1.shape[-1]

    def epilogue(accs, r_refs, o_refs, _):
        o_refs[0][...] = (jax.nn.silu(accs[0]) * accs[1]).astype(BF16)

    out = _colmm(h, [(w1, lead, 0), (w3, lead, 0)], [],
                 [(jax.ShapeDtypeStruct((m_rows, n_cols), BF16), _tile_spec(tm, tn))],
                 epilogue, tm=tm, tn=tn, nt=n_cols // tn, vmem_mib=vmem_mib, name=name)
    return out[0]


def _pool_in(h, w_in, w_grp, scale, j):
    m_rows, k = h.shape
    n_groups = len(POOL_WINDOWS)
    tn = w_in.shape[-1] // n_groups
    tm = 1024
    halo = max(POOL_WINDOWS)

    def epilogue(accs, r_refs, o_refs, s_refs):
        z = accs[0]
        wg_ref, scale_ref = r_refs
        zext, wgb, pooled = s_refs
        n, m = pl.program_id(0), pl.program_id(1)

        @pl.when(m == 0)
        def _():
            zext[0:halo, :] = jnp.zeros((halo, tn), F32)
            wgb[...] = wg_ref[...].astype(BF16)

        zext[halo:halo + tm, :] = z
        t = m * tm + lax.broadcasted_iota(jnp.int32, (tm, tn), 0)
        for gi, win in enumerate(POOL_WINDOWS):
            @pl.when(n == gi)
            def _(win=win):
                wsum = z
                for back in range(1, win):
                    wsum = wsum + zext[halo - back:halo - back + tm, :]
                count = jnp.minimum(t + 1, win).astype(F32)
                pooled[...] = (wsum / count - z).astype(BF16)
        zext[0:halo, :] = zext[tm:tm + halo, :]
        mixed = jnp.dot(pooled[...], wgb[...], preferred_element_type=F32)
        o_refs[0][...] = (mixed * scale_ref[...]).astype(BF16)

    raw = [(w_grp, pl.BlockSpec((None, None, tn, tn), lambda n, m: (j, n, 0, 0))),
           (scale, _row_spec(scale, (), tn))]
    out = _colmm(h, [(w_in, (j,), 0)], raw,
                 [(jax.ShapeDtypeStruct((m_rows, n_groups * tn), BF16), _tile_spec(tm, tn))],
                 epilogue, tm=tm, tn=tn, nt=n_groups,
                 scratch=[pltpu.VMEM((tm + halo, tn), F32), pltpu.VMEM((tn, tn), BF16),
                          pltpu.VMEM((tm, tn), BF16)],
                 vmem_mib=48, name="pool_in")
    return out[0]


def _glu_in(h, w_in, b_in, j):
    m_rows, k = h.shape
    dm = w_in.shape[-1] // 2
    tm, tn = 1024, 512
    nt = dm // tn

    def epilogue(accs, r_refs, o_refs, _):
        a = accs[0] + r_refs[0][...]
        g = accs[1] + r_refs[1][...]
        o_refs[0][...] = (a * jax.nn.sigmoid(g)).astype(BF16)

    raw = [(b_in, _row_spec(b_in, (), tn)), (b_in, _row_spec(b_in, (), tn, off=nt))]
    out = _colmm(h, [(w_in, (j,), 0), (w_in, (j,), nt)], raw,
                 [(jax.ShapeDtypeStruct((m_rows, dm), BF16), _tile_spec(tm, tn))],
                 epilogue, tm=tm, tn=tn, nt=nt, vmem_mib=48, name="glu_in")
    return out[0]


def _conv_ln_body(z_ref, w_ref, cb_ref, g_ref, b_ref, o_ref, zext, conv, *, tm, halo, rows, cols):
    d = z_ref.shape[1]

    @pl.when(pl.program_id(0) == 0)
    def _():
        zext[0:halo, :] = jnp.zeros((halo, d), F32)

    zext[halo:halo + tm, :] = z_ref[...].astype(F32)
    first = halo - (CONV_WIDTH - 1)
    for c0 in range(0, d, cols):
        for r0 in range(0, tm, rows):
            acc = jnp.zeros((rows, cols), F32)
            for kk in range(CONV_WIDTH):
                acc = acc + w_ref[kk:kk + 1, c0:c0 + cols] * zext[first + kk + r0:first + kk + r0 + rows, c0:c0 + cols]
            conv[r0:r0 + rows, c0:c0 + cols] = acc + cb_ref[:, c0:c0 + cols]
    zext[0:halo, :] = zext[tm:tm + halo, :]

    y = conv[...]
    mu = jnp.mean(y, axis=-1, keepdims=True)
    var = jnp.mean(jnp.square(y - mu), axis=-1, keepdims=True)
    yn = (y - mu) * lax.rsqrt(var + EPS) * g_ref[...] + b_ref[...]
    o_ref[...] = jax.nn.silu(yn).astype(o_ref.dtype)


def _conv_ln(z, dw_w, dw_b, ln_g, ln_b, j):
    m_rows, d = z.shape
    tm, halo = 256, 32
    row = pl.BlockSpec((1, d), lambda i: (0, 0))
    return pl.pallas_call(
        functools.partial(_conv_ln_body, tm=tm, halo=halo, rows=64, cols=512),
        grid=(m_rows // tm,),
        in_specs=[pl.BlockSpec((tm, d), lambda i: (i, 0)),
                  pl.BlockSpec((None, CONV_WIDTH, d), lambda i: (j, 0, 0)), row, row, row],
        out_specs=pl.BlockSpec((tm, d), lambda i: (i, 0)),
        out_shape=jax.ShapeDtypeStruct((m_rows, d), BF16),
        scratch_shapes=[pltpu.VMEM((tm + halo, d), F32), pltpu.VMEM((tm, d), F32)],
        compiler_params=_params(40, 1),
        name="conv_ln",
    )(z, dw_w, dw_b, ln_g, ln_b)


def _gelu_in(h, w_in, b_in, j):
    m_rows, k = h.shape
    n_cols = w_in.shape[-1]
    tm, tn = 512, 1024

    def epilogue(accs, r_refs, o_refs, _):
        a = accs[0] + r_refs[0][...]
        o_refs[0][...] = (0.5 * a * (1.0 + lax.erf(a * math.sqrt(0.5)))).astype(BF16)

    out = _colmm(h, [(w_in, (j,), 0)], [(b_in, _row_spec(b_in, (), tn))],
                 [(jax.ShapeDtypeStruct((m_rows, n_cols), BF16), _tile_spec(tm, tn))],
                 epilogue, tm=tm, tn=tn, nt=n_cols // tn, vmem_mib=48, name="gelu_in")
    return out[0]


def _sgu_gate_body(u_ref, v_ref, g_ref, b_ref, ws_ref, bst_ref, o_ref, *, tm):
    v = v_ref[...].astype(F32)
    mu = jnp.mean(v, axis=-1, keepdims=True)
    var = jnp.mean(jnp.square(v - mu), axis=-1, keepdims=True)
    vn = ((v - mu) * lax.rsqrt(var + EPS) * g_ref[...] + b_ref[...]).astype(BF16)
    hd = v.shape[1] // SGU_HEADS
    tri = (lax.broadcasted_iota(jnp.int32, (SGU_CHUNK, SGU_CHUNK), 0)
           >= lax.broadcasted_iota(jnp.int32, (SGU_CHUNK, SGU_CHUNK), 1))
    for hh in range(SGU_HEADS):
        wc = jnp.where(tri, ws_ref[hh], 0.0).astype(BF16)
        bias = bst_ref[:, hh:hh + 1]
        for ck in range(tm // SGU_CHUNK):
            rs = slice(ck * SGU_CHUNK, (ck + 1) * SGU_CHUNK)
            cs = slice(hh * hd, (hh + 1) * hd)
            sv = jnp.dot(wc, vn[rs, cs], preferred_element_type=F32) + bias
            o_ref[rs, cs] = (u_ref[rs, cs].astype(F32) * sv).astype(o_ref.dtype)


def _sgu_gate(a, ln_g, ln_b, w_s, b_s_t, j):
    m_rows = a.shape[0]
    d = a.shape[1] // 2
    tm = 256
    row = pl.BlockSpec((1, d), lambda i: (0, 0))
    return pl.pallas_call(
        functools.partial(_sgu_gate_body, tm=tm),
        grid=(m_rows // tm,),
        in_specs=[pl.BlockSpec((tm, d), lambda i: (i, 0)), pl.BlockSpec((tm, d), lambda i: (i, 1)),
                  row, row,
                  pl.BlockSpec((None, SGU_HEADS, SGU_CHUNK, SGU_CHUNK), lambda i: (j, 0, 0, 0)),
                  pl.BlockSpec((SGU_CHUNK, SGU_HEADS), lambda i: (0, 0))],
        out_specs=pl.BlockSpec((tm, d), lambda i: (i, 0)),
        out_shape=jax.ShapeDtypeStruct((m_rows, d), BF16),
        compiler_params=_params(40, 1),
        name="sgu_gate",
    )(a, a, ln_g, ln_b, w_s, b_s_t)


def _sconv_in(h, w_in, conv_w, j):
    m_rows, k = h.shape
    dm = w_in.shape[-1] // 3
    tm, tn = 1024, 256
    nt = dm // tn
    halo = SUBLANES

    def epilogue(accs, r_refs, o_refs, s_refs):
        bg, cg, z = accs
        cw_ref = r_refs[0]
        ext = s_refs[0]

        @pl.when(pl.program_id(1) == 0)
        def _():
            ext[0:halo, :] = jnp.zeros((halo, tn), F32)

        cz = cg * z
        ext[halo:halo + tm, :] = cz
        conv = cw_ref[SHORT_CONV_WIDTH - 1:SHORT_CONV_WIDTH, :] * cz
        for back in range(1, SHORT_CONV_WIDTH):
            tap = SHORT_CONV_WIDTH - 1 - back
            conv = conv + cw_ref[tap:tap + 1, :] * ext[halo - back:halo - back + tm, :]
        ext[0:halo, :] = ext[tm:tm + halo, :]
        o_refs[0][...] = (bg * conv).astype(BF16)

    raw = [(conv_w, pl.BlockSpec((None, SHORT_CONV_WIDTH, tn), lambda n, m: (j, 0, n)))]
    out = _colmm(h, [(w_in, (j,), 0), (w_in, (j,), nt), (w_in, (j,), 2 * nt)], raw,
                 [(jax.ShapeDtypeStruct((m_rows, dm), BF16), _tile_spec(tm, tn))],
                 epilogue, tm=tm, tn=tn, nt=nt,
                 scratch=[pltpu.VMEM((tm + halo, tn), F32)], vmem_mib=48, name="sconv_in")
    return out[0]


def _router_body(h_ref, w_ref, info_ref, cum_ref, carry):
    logits = jnp.dot(h_ref[...], w_ref[...].astype(BF16), preferred_element_type=F32)
    lane = lax.broadcasted_iota(jnp.int32, logits.shape, 1)
    neg = jnp.float32(-jnp.inf)
    lg = jnp.where(lane < N_EXPERTS, logits, neg)
    v1 = jnp.max(lg, axis=-1, keepdims=True)
    lane_f = lane.astype(F32)
    i1 = jnp.min(jnp.where(lg == v1, lane_f, float(LANES)), axis=-1, keepdims=True)
    lg2 = jnp.where(lane_f == i1, neg, lg)
    v2 = jnp.max(lg2, axis=-1, keepdims=True)
    i2 = jnp.min(jnp.where(lg2 == v2, lane_f, float(LANES)), axis=-1, keepdims=True)
    e2 = jnp.exp(v2 - v1)
    den = 1.0 + e2
    w1, w2 = 1.0 / den, e2 / den

    @pl.when(pl.program_id(0) == 0)
    def _():
        carry[...] = jnp.zeros(carry.shape, F32)

    tm = logits.shape[0]
    cnt = jnp.where(lane_f == i1, 1.0, 0.0) + jnp.where(lane_f == i2, 1.0, 0.0)
    strict = jnp.where(lax.broadcasted_iota(jnp.int32, (tm, tm), 1)
                       < lax.broadcasted_iota(jnp.int32, (tm, tm), 0), 1.0, 0.0).astype(BF16)
    before = jnp.dot(strict, cnt.astype(BF16), preferred_element_type=F32) + carry[...]
    r1 = jnp.sum(jnp.where(lane_f == i1, before, 0.0), axis=-1, keepdims=True)
    r2 = jnp.sum(jnp.where(lane_f == i2, before, 0.0), axis=-1, keepdims=True)
    fields = (i1, i2, w1, w2, r1, r2)
    info = jnp.zeros(logits.shape, F32)
    for q, val in enumerate(fields):
        info = jnp.where(lane == q, val, info)
    info_ref[...] = info
    total = carry[...] + jnp.sum(cnt, axis=0, keepdims=True)
    carry[...] = total
    cum_ref[...] = total


def _router(h, w_router_padded):
    m_rows, k = h.shape
    tm = MOE_BLOCK
    return pl.pallas_call(
        _router_body,
        grid=(m_rows // tm,),
        in_specs=[pl.BlockSpec((tm, k), lambda i: (i, 0)), pl.BlockSpec((k, LANES), lambda i: (0, 0))],
        out_specs=[pl.BlockSpec((tm, LANES), lambda i: (i, 0)),
                   pl.BlockSpec((None, 1, LANES), lambda i: (i, 0, 0))],
        out_shape=[jax.ShapeDtypeStruct((m_rows, LANES), F32),
                   jax.ShapeDtypeStruct((m_rows // tm, 1, LANES), F32)],
        scratch_shapes=[pltpu.VMEM((1, LANES), F32)],
        compiler_params=_params(40, 1),
        name="router",
    )(h, w_router_padded)


def _dispatch_plan(info, cum):
    blk, n_exp = MOE_BLOCK, N_EXPERTS
    n_tok = info.shape[0]
    tbn = n_tok // blk
    nb_max = 2 * tbn + n_exp
    s_max = nb_max + n_exp * (tbn - 1)
    i32 = jnp.int32
    i1, i2 = info[:, 0].astype(i32), info[:, 1].astype(i32)
    r1, r2 = info[:, 4].astype(i32), info[:, 5].astype(i32)
    cb = jnp.concatenate([jnp.zeros((1, n_exp), i32), cum[:, 0, :n_exp].astype(i32)])
    counts = cb[-1]
    nblk = (counts + blk - 1) // blk
    blk_end = jnp.cumsum(nblk)
    blk_off = blk_end - nblk
    nb = blk_end[-1]
    slot1 = blk_off[i1] * blk + r1
    slot2 = blk_off[i2] * blk + r2

    b_idx = jnp.arange(nb_max, dtype=i32)
    b_clamped = jnp.minimum(b_idx, nb - 1)
    blk_e = jnp.minimum(jnp.searchsorted(blk_end, b_clamped, side="right").astype(i32), n_exp - 1)
    prev_e = jnp.concatenate([jnp.full((1,), -1, i32), blk_e[:-1]])
    b_valid = b_idx < nb
    blk_flag = b_valid.astype(i32) + 2 * (b_valid & (blk_e != prev_e)).astype(i32)

    k = jnp.arange(tbn, dtype=i32)[None, :, None]
    lo = jnp.maximum(k * blk, cb[:-1].T[:, None, :])
    hi = jnp.minimum((k + 1) * blk, cb[1:].T[:, None, :])
    inter = lo < hi
    n_steps = jnp.sum(inter).astype(i32)
    s_idx = jnp.arange(s_max, dtype=i32)
    s_clamped = jnp.minimum(s_idx, n_steps - 1)
    s_valid = s_idx < n_steps

    def step_lists(mask, decode):
        f = jnp.nonzero(mask.ravel(), size=s_max, fill_value=0)[0].astype(i32)[s_clamped]
        gb, tb, major = decode(f)
        prev_m = jnp.concatenate([jnp.full((1,), -1, i32), major[:-1]])
        next_m = jnp.concatenate([major[1:], jnp.full((1,), -1, i32)])
        first = s_valid & (prev_m != major)
        last = s_valid & ((next_m != major) | (s_idx == n_steps - 1))
        flag = s_valid.astype(i32) + 2 * first.astype(i32) + 4 * last.astype(i32)
        return gb, tb, flag

    def decode_slot_major(f):
        e, kk, tb = f // (tbn * tbn), (f // tbn) % tbn, f % tbn
        gb = blk_off[e] + kk
        return gb, tb, gb

    def decode_token_major(f):
        tb, e, kk = f // (n_exp * tbn), (f // tbn) % n_exp, f % tbn
        gb = blk_off[e] + kk
        return gb, tb, tb

    g_gb, g_tb, g_flag = step_lists(inter, decode_slot_major)
    c_gb, c_tb, c_flag = step_lists(inter.transpose(2, 0, 1), decode_token_major)
    return dict(slot1=slot1, slot2=slot2, w1=info[:, 2], w2=info[:, 3],
                blk_e=blk_e, blk_row=b_clamped, blk_flag=blk_flag,
                g_gb=g_gb, g_tb=g_tb, g_flag=g_flag, c_gb=c_gb, c_tb=c_tb, c_flag=c_flag,
                nb_max=nb_max, s_max=s_max)


def _moe_gather(h, plan):
    n_tok, d = h.shape
    blk = MOE_BLOCK
    tbn = n_tok // blk
    p_rows = plan["nb_max"] * blk

    def body(gb_ref, tb_ref, fl_ref, h_ref, s1_ref, s2_ref, w1_ref, w2_ref, hg_ref, ws_ref, acc, wacc):
        s = pl.program_id(0)
        flag = fl_ref[s]

        @pl.when((flag & 2) != 0)
        def _():
            acc[...] = jnp.zeros(acc.shape, F32)
            wacc[...] = jnp.zeros(wacc.shape, F32)

        @pl.when((flag & 1) != 0)
        def _():
            slot = gb_ref[s] * blk + lax.broadcasted_iota(jnp.int32, (blk, blk), 0)
            d1 = s1_ref[...] == slot
            d2 = s2_ref[...] == slot
            sel = jnp.where(d1, 1.0, jnp.where(d2, 1.0, 0.0)).astype(BF16)
            acc[...] += jnp.dot(sel, h_ref[...], preferred_element_type=F32)
            wacc[...] += jnp.sum(jnp.where(d1, w1_ref[...], 0.0) + jnp.where(d2, w2_ref[...], 0.0),
                                 axis=1, keepdims=True)

        @pl.when((flag & 4) != 0)
        def _():
            hg_ref[...] = acc[...].astype(BF16)
            ws_ref[...] = wacc[...]

    tok_row = pl.BlockSpec((None, 1, blk), lambda s, gb, tb, fl: (tb[s], 0, 0))
    grid_spec = pltpu.PrefetchScalarGridSpec(
        num_scalar_prefetch=3,
        grid=(plan["s_max"],),
        in_specs=[pl.BlockSpec((blk, d), lambda s, gb, tb, fl: (tb[s], 0)),
                  tok_row, tok_row, tok_row, tok_row],
        out_specs=[pl.BlockSpec((blk, d), lambda s, gb, tb, fl: (gb[s], 0)),
                   pl.BlockSpec((blk, 1), lambda s, gb, tb, fl: (gb[s], 0))],
        scratch_shapes=[pltpu.VMEM((blk, d), F32), pltpu.VMEM((blk, 1), F32)])
    rows = lambda a: a.reshape(tbn, 1, blk)
    return pl.pallas_call(
        body, grid_spec=grid_spec,
        out_shape=[jax.ShapeDtypeStruct((p_rows, d), BF16), jax.ShapeDtypeStruct((p_rows, 1), F32)],
        compiler_params=_params(40, 1), name="moe_gather",
    )(plan["g_gb"], plan["g_tb"], plan["g_flag"], h,
      rows(plan["slot1"]), rows(plan["slot2"]), rows(plan["w1"]), rows(plan["w2"]))


def _grouped_mm(x, weights, lead, plan, row_inputs, out_cols, epilogue, *, tn, vmem_mib, name):
    p_rows, k = x.shape
    blk = MOE_BLOCK
    nw, nr = len(weights), len(row_inputs)

    def body(be_ref, br_ref, fl_ref, x_ref, *refs):
        w_refs, r_refs = refs[:nw], refs[nw:nw + nr]
        o_ref = refs[nw + nr]
        wb_refs = refs[nw + nr + 1:]
        flag = fl_ref[pl.program_id(1)]

        @pl.when((flag & 2) != 0)
        def _():
            for w_ref, wb in zip(w_refs, wb_refs):
                wb[...] = w_ref[...].astype(BF16)

        @pl.when((flag & 1) != 0)
        def _():
            xv = x_ref[...]
            accs = [jnp.dot(xv, wb[...], preferred_element_type=F32) for wb in wb_refs]
            o_ref[...] = epilogue(accs, r_refs).astype(BF16)

    w_spec = pl.BlockSpec((None,) * (len(lead) + 1) + (k, tn),
                          lambda n, b, be, br, fl: (*lead, be[b], 0, n))
    grid_spec = pltpu.PrefetchScalarGridSpec(
        num_scalar_prefetch=3,
        grid=(out_cols // tn, plan["nb_max"]),
        in_specs=[pl.BlockSpec((blk, k), lambda n, b, be, br, fl: (br[b], 0))]
        + [w_spec] * nw
        + [pl.BlockSpec((blk, a.shape[1]), lambda n, b, be, br, fl: (br[b], 0)) for a in row_inputs],
        out_specs=pl.BlockSpec((blk, tn), lambda n, b, be, br, fl: (br[b], n)),
        scratch_shapes=[pltpu.VMEM((k, tn), BF16) for _ in weights])
    return pl.pallas_call(
        body, grid_spec=grid_spec,
        out_shape=jax.ShapeDtypeStruct((p_rows, out_cols), BF16),
        compiler_params=_params(vmem_mib, 2), name=name,
    )(plan["blk_e"], plan["blk_row"], plan["blk_flag"], x, *weights, *row_inputs)


def _moe_combine(yw, plan, x_res, gate):
    n_tok, d = x_res.shape
    blk = MOE_BLOCK

    def body(gb_ref, tb_ref, fl_ref, y_ref, s1_ref, s2_ref, x_ref, g_ref, o_ref, acc):
        s = pl.program_id(0)
        flag = fl_ref[s]

        @pl.when((flag & 2) != 0)
        def _():
            acc[...] = jnp.zeros(acc.shape, F32)

        @pl.when((flag & 1) != 0)
        def _():
            slot = gb_ref[s] * blk + lax.broadcasted_iota(jnp.int32, (blk, blk), 1)
            sel = jnp.where(s1_ref[...] == slot, 1.0,
                            jnp.where(s2_ref[...] == slot, 1.0, 0.0)).astype(BF16)
            acc[...] += jnp.dot(sel, y_ref[...], preferred_element_type=F32)

        @pl.when((flag & 4) != 0)
        def _():
            o_ref[...] = x_ref[...] + g_ref[...] * acc[...]

    tok_col = pl.BlockSpec((blk, 1), lambda s, gb, tb, fl: (tb[s], 0))
    tok_blk = pl.BlockSpec((blk, d), lambda s, gb, tb, fl: (tb[s], 0))
    grid_spec = pltpu.PrefetchScalarGridSpec(
        num_scalar_prefetch=3,
        grid=(plan["s_max"],),
        in_specs=[pl.BlockSpec((blk, d), lambda s, gb, tb, fl: (gb[s], 0)), tok_col, tok_col, tok_blk,
                  pl.BlockSpec((1, d), lambda s, gb, tb, fl: (0, 0))],
        out_specs=tok_blk,
        scratch_shapes=[pltpu.VMEM((blk, d), F32)])
    col = lambda a: a.reshape(n_tok, 1)
    return pl.pallas_call(
        body, grid_spec=grid_spec,
        out_shape=jax.ShapeDtypeStruct((n_tok, d), F32),
        compiler_params=_params(40, 1), name="moe_combine",
    )(plan["c_gb"], plan["c_tb"], plan["c_flag"], yw, col(plan["slot1"]), col(plan["slot2"]), x_res, gate)


def _moe(h, x_res, gate, w_router, w1, w3, w2, kk):
    w_r = jnp.pad(w_router[kk], ((0, 0), (0, LANES - N_EXPERTS)))
    info, cum = _router(h, w_r)
    plan = _dispatch_plan(info, cum)
    hg, w_slot = _moe_gather(h, plan)
    hid = _grouped_mm(hg, [w1, w3], (kk,), plan, [], w1.shape[-1],
                      lambda accs, _: jax.nn.silu(accs[0]) * accs[1],
                      tn=1024, vmem_mib=56, name="moe_up")
    yw = _grouped_mm(hid, [w2], (kk,), plan, [w_slot], w2.shape[-1],
                     lambda accs, r_refs: accs[0] * r_refs[0][...],
                     tn=512, vmem_mib=48, name="moe_down")
    return _moe_combine(yw, plan, x_res, gate)


def kernel(x, c, ada_w, ada_b, norm1_g, norm2_g, pool_w_in, pool_w_grp, pool_scale, pool_w_out, conv_w_in, conv_b_in, conv_dw_w, conv_dw_b, conv_ln_g, conv_ln_b, conv_w_out, sgu_w_in, sgu_b_in, sgu_ln_g, sgu_ln_b, sgu_w_s, sgu_b_s, sgu_w_out, sconv_w_in, sconv_w, sconv_w_out, ffn_w1, ffn_w3, ffn_w2, moe_router, moe_w1, moe_w3, moe_w2, final_g):
    batch, seq, d = x.shape
    assert batch == 1
    depth = ada_w.shape[0]
    xs = x.reshape(seq, d)
    mod = _ada_mod(c, ada_w, ada_b)

    for i in range(depth):
        sh1, sc1, g1, sh2, sc2, g2 = [mod[i, :, q * d:(q + 1) * d] for q in range(6)]

        h = _norm_mod(xs, norm1_g[i:i + 1], sc1, sh1)
        mixer, j = i % 4, i // 4
        if mixer == 0:
            a = _pool_in(h, pool_w_in, pool_w_grp, pool_scale[j:j + 1], j)
            w_out = pool_w_out
        elif mixer == 1:
            z = _glu_in(h, conv_w_in, conv_b_in[j:j + 1], j)
            a = _conv_ln(z, conv_dw_w, conv_dw_b[j:j + 1], conv_ln_g[j:j + 1], conv_ln_b[j:j + 1], j)
            w_out = conv_w_out
        elif mixer == 2:
            a = _gelu_in(h, sgu_w_in, sgu_b_in[j:j + 1], j)
            a = _sgu_gate(a, sgu_ln_g[j:j + 1], sgu_ln_b[j:j + 1], sgu_w_s, sgu_b_s[j].T, j)
            w_out = sgu_w_out
        else:
            a = _sconv_in(h, sconv_w_in, sconv_w, j)
            w_out = sconv_w_out
        xs = _mm_residual(a, w_out, (j,), xs, g1, tm=512, tn=1024, vmem_mib=48, name="mixer_out")

        h = _norm_mod(xs, norm2_g[i:i + 1], sc2, sh2)
        kk = i // 2
        if i % 2 == 0:
            hid = _swiglu_up(h, ffn_w1, ffn_w3, (kk,), tm=1024, tn=512, vmem_mib=48, name="ffn_up")
            xs = _mm_residual(hid, ffn_w2, (kk,), xs, g2, tm=512, tn=512, vmem_mib=56, name="ffn_down")
        else:
            xs = _moe(h, xs, g2, moe_router, moe_w1, moe_w3, moe_w2, kk)
    return _final_norm(xs, final_g.reshape(1, d)).reshape(batch, seq, d)
```

```python
import functools
import math

import jax
import jax.numpy as jnp
from jax import lax
from jax.experimental import pallas as pl
from jax.experimental.pallas import tpu as pltpu

EPS = 1e-6
POOL_WINDOWS = (2, 4, 8, 16)
CONV_WIDTH = 31
SGU_CHUNK = 128
SGU_HEADS = 8
SHORT_CONV_WIDTH = 3
N_EXPERTS = 8
MOE_BLOCK = 512
LANES = 128
SUBLANES = 8
MIB = 1024 * 1024

BF16 = jnp.bfloat16
F32 = jnp.float32


def _params(vmem_mib, ndims):
    return pltpu.CompilerParams(
        dimension_semantics=("arbitrary",) * ndims,
        vmem_limit_bytes=vmem_mib * MIB)


def _ada_body(c_ref, w_ref, b_ref, o_ref):
    c_act = jax.nn.silu(c_ref[...])
    o_ref[0] = jnp.sum(w_ref[0] * c_act, axis=0, keepdims=True) + b_ref[0]


def _ada_mod(c, ada_w, ada_b):
    depth, d, n = ada_w.shape
    tn = 1024
    return pl.pallas_call(
        _ada_body,
        grid=(depth, n // tn),
        in_specs=[
            pl.BlockSpec((d, 1), lambda i, j: (0, 0)),
            pl.BlockSpec((1, d, tn), lambda i, j: (i, 0, j)),
            pl.BlockSpec((1, 1, tn), lambda i, j: (i, 0, j)),
        ],
        out_specs=pl.BlockSpec((1, 1, tn), lambda i, j: (i, 0, j)),
        out_shape=jax.ShapeDtypeStruct((depth, 1, n), F32),
        compiler_params=_params(40, 2),
        name="ada_mod",
    )(c.reshape(d, 1), ada_w, ada_b.reshape(depth, 1, n))


def _rms(x, g):
    return (x * lax.rsqrt(jnp.mean(x * x, axis=-1, keepdims=True) + EPS)) * g


def _rms_mod(x, g, sc, sh):
    return _rms(x, g) * (1.0 + sc) + sh


def _norm_mod_body(x_ref, g_ref, sc_ref, sh_ref, o_ref):
    o_ref[...] = _rms_mod(x_ref[...], g_ref[...], sc_ref[...], sh_ref[...]).astype(o_ref.dtype)


def _norm_mod(x, g, sc, sh):
    m, d = x.shape
    tm = 512
    row = pl.BlockSpec((1, d), lambda i: (0, 0))
    return pl.pallas_call(
        _norm_mod_body,
        grid=(m // tm,),
        in_specs=[pl.BlockSpec((tm, d), lambda i: (i, 0)), row, row, row],
        out_specs=pl.BlockSpec((tm, d), lambda i: (i, 0)),
        out_shape=jax.ShapeDtypeStruct((m, d), BF16),
        compiler_params=_params(40, 1),
        name="norm_mod",
    )(x, g, sc, sh)


def _final_norm_body(x_ref, g_ref, o_ref):
    o_ref[...] = _rms(x_ref[...], g_ref[...])


def _final_norm(x, g):
    m, d = x.shape
    tm = 512
    return pl.pallas_call(
        _final_norm_body,
        grid=(m // tm,),
        in_specs=[pl.BlockSpec((tm, d), lambda i: (i, 0)), pl.BlockSpec((1, d), lambda i: (0, 0))],
        out_specs=pl.BlockSpec((tm, d), lambda i: (i, 0)),
        out_shape=jax.ShapeDtypeStruct((m, d), F32),
        compiler_params=_params(40, 1),
        name="final_norm",
    )(x, g)


def _w_spec(w, lead, k, tn, off):
    assert w.shape[len(lead)] == k
    return pl.BlockSpec((None,) * len(lead) + (k, tn), lambda n, m: (*lead, 0, n + off))


def _row_spec(a, lead, tn, off=0):
    return pl.BlockSpec((None,) * len(lead) + (1, tn), lambda n, m: (*lead, 0, n + off))


def _colmm(x, weights, raw_inputs, outs, epilogue, *, tm, tn, nt, scratch=(), vmem_mib, name):
    m_rows, k = x.shape
    nw, nr, no = len(weights), len(raw_inputs), len(outs)

    def body(*refs):
        x_ref = refs[0]
        w_refs = refs[1:1 + nw]
        r_refs = refs[1 + nw:1 + nw + nr]
        o_refs = refs[1 + nw + nr:1 + nw + nr + no]
        s_refs = refs[1 + nw + nr + no:]
        wb_refs, extra = s_refs[:nw], s_refs[nw:]

        @pl.when(pl.program_id(1) == 0)
        def _():
            for w_ref, wb in zip(w_refs, wb_refs):
                wb[...] = w_ref[...].astype(BF16)

        xv = x_ref[...]
        accs = [jnp.dot(xv, wb[...], preferred_element_type=F32) for wb in wb_refs]
        epilogue(accs, r_refs, o_refs, extra)

    in_specs = [pl.BlockSpec((tm, k), lambda n, m: (m, 0))]
    in_specs += [_w_spec(w, lead, k, tn, off) for (w, lead, off) in weights]
    in_specs += [spec for (_, spec) in raw_inputs]
    return pl.pallas_call(
        body,
        grid=(nt, m_rows // tm),
        in_specs=in_specs,
        out_specs=[spec for (_, spec) in outs],
        out_shape=[sds for (sds, _) in outs],
        scratch_shapes=[pltpu.VMEM((k, tn), BF16) for _ in weights] + list(scratch),
        compiler_params=_params(vmem_mib, 2),
        name=name,
    )(x, *[w for (w, _, _) in weights], *[a for (a, _) in raw_inputs])


def _tile_spec(tm, tn):
    return pl.BlockSpec((tm, tn), lambda n, m: (m, n))


def _mm_residual(a, w, lead, x_res, gate, *, tm, tn, vmem_mib, name):
    m_rows = a.shape[0]
    n_cols = x_res.shape[1]
    raw = [(gate, _row_spec(gate, (), tn)), (x_res, _tile_spec(tm, tn))]

    def epilogue(accs, r_refs, o_refs, _):
        o_refs[0][...] = r_refs[1][...] + accs[0] * r_refs[0][...]

    out = _colmm(a, [(w, lead, 0)], raw,
                 [(jax.ShapeDtypeStruct((m_rows, n_cols), F32), _tile_spec(tm, tn))],
                 epilogue, tm=tm, tn=tn, nt=n_cols // tn, vmem_mib=vmem_mib, name=name)
    return out[0]


def _mixer_out_norm(a, w, j, x_res, gate, g, sc, sh):
    m_rows, k = a.shape
    d = x_res.shape[1]
    tm = 256

    def body(a_ref, w_ref, gate_ref, x_ref, g_ref, sc_ref, sh_ref, xo_ref, ho_ref, wb):
        @pl.when(pl.program_id(0) == 0)
        def _():
            wb[...] = w_ref[...].astype(BF16)

        y = jnp.dot(a_ref[...], wb[...], preferred_element_type=F32)
        x_new = x_ref[...] + gate_ref[...] * y
        xo_ref[...] = x_new
        ho_ref[...] = _rms_mod(x_new, g_ref[...], sc_ref[...], sh_ref[...]).astype(BF16)

    row = pl.BlockSpec((1, d), lambda i: (0, 0))
    blk_in = pl.BlockSpec((tm, k), lambda i: (i, 0))
    blk_d = pl.BlockSpec((tm, d), lambda i: (i, 0))
    return pl.pallas_call(
        body,
        grid=(m_rows // tm,),
        in_specs=[blk_in,
                  pl.BlockSpec((None, k, d), lambda i: (j, 0, 0), pipeline_mode=pl.Buffered(1)),
                  row, blk_d, row, row, row],
        out_specs=[blk_d, blk_d],
        out_shape=[jax.ShapeDtypeStruct((m_rows, d), F32), jax.ShapeDtypeStruct((m_rows, d), BF16)],
        scratch_shapes=[pltpu.VMEM((k, d), BF16)],
        compiler_params=_params(48, 1),
        name="mixer_out",
    )(a, w, gate, x_res, g, sc, sh)


def _swiglu_up(h, w1, w3, lead, *, tm, tn, vmem_mib, name):
    m_rows = h.shape[0]
    n_cols = w1.shape[-1]

    def epilogue(accs, r_refs, o_refs, _):
        o_refs[0][...] = (jax.nn.silu(accs[0]) * accs[1]).astype(BF16)

    out = _colmm(h, [(w1, lead, 0), (w3, lead, 0)], [],
                 [(jax.ShapeDtypeStruct((m_rows, n_cols), BF16), _tile_spec(tm, tn))],
                 epilogue, tm=tm, tn=tn, nt=n_cols // tn, vmem_mib=vmem_mib, name=name)
    return out[0]


def _pool_in(h, w_in, w_grp, scale, j):
    m_rows, k = h.shape
    n_groups = len(POOL_WINDOWS)
    tn = w_in.shape[-1] // n_groups
    tm = 1024
    halo = max(POOL_WINDOWS)

    def epilogue(accs, r_refs, o_refs, s_refs):
        z = accs[0]
        wg_ref, scale_ref = r_refs
        zext, wgb, pooled = s_refs
        n, m = pl.program_id(0), pl.program_id(1)

        @pl.when(m == 0)
        def _():
            zext[0:halo, :] = jnp.zeros((halo, tn), F32)
            wgb[...] = wg_ref[...].astype(BF16)

        zext[halo:halo + tm, :] = z
        t = m * tm + lax.broadcasted_iota(jnp.int32, (tm, tn), 0)
        for gi, win in enumerate(POOL_WINDOWS):
            @pl.when(n == gi)
            def _(win=win):
                wsum = z
                for back in range(1, win):
                    wsum = wsum + zext[halo - back:halo - back + tm, :]
                count = jnp.minimum(t + 1, win).astype(F32)
                pooled[...] = (wsum / count - z).astype(BF16)
        zext[0:halo, :] = zext[tm:tm + halo, :]
        mixed = jnp.dot(pooled[...], wgb[...], preferred_element_type=F32)
        o_refs[0][...] = (mixed * scale_ref[...]).astype(BF16)

    raw = [(w_grp, pl.BlockSpec((None, None, tn, tn), lambda n, m: (j, n, 0, 0))),
           (scale, _row_spec(scale, (), tn))]
    out = _colmm(h, [(w_in, (j,), 0)], raw,
                 [(jax.ShapeDtypeStruct((m_rows, n_groups * tn), BF16), _tile_spec(tm, tn))],
                 epilogue, tm=tm, tn=tn, nt=n_groups,
                 scratch=[pltpu.VMEM((tm + halo, tn), F32), pltpu.VMEM((tn, tn), BF16),
                          pltpu.VMEM((tm, tn), BF16)],
                 vmem_mib=48, name="pool_in")
    return out[0]


def _glu_in(h, w_in, b_in, j):
    m_rows, k = h.shape
    dm = w_in.shape[-1] // 2
    tm, tn = 1024, 512
    nt = dm // tn

    def epilogue(accs, r_refs, o_refs, _):
        a = accs[0] + r_refs[0][...]
        g = accs[1] + r_refs[1][...]
        o_refs[0][...] = (a * jax.nn.sigmoid(g)).astype(BF16)

    raw = [(b_in, _row_spec(b_in, (), tn)), (b_in, _row_spec(b_in, (), tn, off=nt))]
    out = _colmm(h, [(w_in, (j,), 0), (w_in, (j,), nt)], raw,
                 [(jax.ShapeDtypeStruct((m_rows, dm), BF16), _tile_spec(tm, tn))],
                 epilogue, tm=tm, tn=tn, nt=nt, vmem_mib=48, name="glu_in")
    return out[0]


def _conv_ln_body(z_ref, w_ref, cb_ref, g_ref, b_ref, o_ref, zsh, conv, *, tm, halo, rows, cols):
    d = z_ref.shape[1]

    @pl.when(pl.program_id(0) == 0)
    def _():
        zsh[0, 0:halo, :] = jnp.zeros((halo, d), F32)

    zsh[0, halo:halo + tm, :] = z_ref[...].astype(F32)
    for b in range(1, SUBLANES):
        zsh[b, SUBLANES:halo + tm, :] = zsh[0, SUBLANES - b:halo + tm - b, :]
    for c0 in range(0, d, cols):
        for r0 in range(0, tm, rows):
            acc = jnp.zeros((rows, cols), F32)
            for kk in range(CONV_WIDTH):
                a, b = divmod(CONV_WIDTH - 1 - kk, SUBLANES)
                start = halo + r0 - SUBLANES * a
                acc = acc + w_ref[kk:kk + 1, c0:c0 + cols] * zsh[b, start:start + rows, c0:c0 + cols]
            conv[r0:r0 + rows, c0:c0 + cols] = acc + cb_ref[:, c0:c0 + cols]
    zsh[0, 0:halo, :] = zsh[0, tm:tm + halo, :]

    y = conv[...]
    mu = jnp.mean(y, axis=-1, keepdims=True)
    var = jnp.mean(jnp.square(y - mu), axis=-1, keepdims=True)
    yn = (y - mu) * lax.rsqrt(var + EPS) * g_ref[...] + b_ref[...]
    o_ref[...] = jax.nn.silu(yn).astype(o_ref.dtype)


def _conv_ln(z, dw_w, dw_b, ln_g, ln_b, j):
    m_rows, d = z.shape
    tm, halo = 256, 32
    row = pl.BlockSpec((1, d), lambda i: (0, 0))
    return pl.pallas_call(
        functools.partial(_conv_ln_body, tm=tm, halo=halo, rows=64, cols=512),
        grid=(m_rows // tm,),
        in_specs=[pl.BlockSpec((tm, d), lambda i: (i, 0)),
                  pl.BlockSpec((None, CONV_WIDTH, d), lambda i: (j, 0, 0)), row, row, row],
        out_specs=pl.BlockSpec((tm, d), lambda i: (i, 0)),
        out_shape=jax.ShapeDtypeStruct((m_rows, d), BF16),
        scratch_shapes=[pltpu.VMEM((SUBLANES, tm + halo, d), F32), pltpu.VMEM((tm, d), F32)],
        compiler_params=_params(40, 1),
        name="conv_ln",
    )(z, dw_w, dw_b, ln_g, ln_b)


def _gelu_in(h, w_in, b_in, j):
    m_rows, k = h.shape
    n_cols = w_in.shape[-1]
    tm, tn = 512, 1024

    def epilogue(accs, r_refs, o_refs, _):
        a = accs[0] + r_refs[0][...]
        o_refs[0][...] = (0.5 * a * (1.0 + lax.erf(a * math.sqrt(0.5)))).astype(BF16)

    out = _colmm(h, [(w_in, (j,), 0)], [(b_in, _row_spec(b_in, (), tn))],
                 [(jax.ShapeDtypeStruct((m_rows, n_cols), BF16), _tile_spec(tm, tn))],
                 epilogue, tm=tm, tn=tn, nt=n_cols // tn, vmem_mib=48, name="gelu_in")
    return out[0]


def _sgu_gate_body(u_ref, v_ref, g_ref, b_ref, ws_ref, bst_ref, o_ref, *, tm):
    v = v_ref[...].astype(F32)
    mu = jnp.mean(v, axis=-1, keepdims=True)
    var = jnp.mean(jnp.square(v - mu), axis=-1, keepdims=True)
    vn = ((v - mu) * lax.rsqrt(var + EPS) * g_ref[...] + b_ref[...]).astype(BF16)
    hd = v.shape[1] // SGU_HEADS
    tri = (lax.broadcasted_iota(jnp.int32, (SGU_CHUNK, SGU_CHUNK), 0)
           >= lax.broadcasted_iota(jnp.int32, (SGU_CHUNK, SGU_CHUNK), 1))
    for hh in range(SGU_HEADS):
        wc = jnp.where(tri, ws_ref[hh], 0.0).astype(BF16)
        bias = bst_ref[:, hh:hh + 1]
        for ck in range(tm // SGU_CHUNK):
            rs = slice(ck * SGU_CHUNK, (ck + 1) * SGU_CHUNK)
            cs = slice(hh * hd, (hh + 1) * hd)
            sv = jnp.dot(wc, vn[rs, cs], preferred_element_type=F32) + bias
            o_ref[rs, cs] = (u_ref[rs, cs].astype(F32) * sv).astype(o_ref.dtype)


def _sgu_gate(a, ln_g, ln_b, w_s, b_s_t, j):
    m_rows = a.shape[0]
    d = a.shape[1] // 2
    tm = 256
    row = pl.BlockSpec((1, d), lambda i: (0, 0))
    return pl.pallas_call(
        functools.partial(_sgu_gate_body, tm=tm),
        grid=(m_rows // tm,),
        in_specs=[pl.BlockSpec((tm, d), lambda i: (i, 0)), pl.BlockSpec((tm, d), lambda i: (i, 1)),
                  row, row,
                  pl.BlockSpec((None, SGU_HEADS, SGU_CHUNK, SGU_CHUNK), lambda i: (j, 0, 0, 0)),
                  pl.BlockSpec((SGU_CHUNK, SGU_HEADS), lambda i: (0, 0))],
        out_specs=pl.BlockSpec((tm, d), lambda i: (i, 0)),
        out_shape=jax.ShapeDtypeStruct((m_rows, d), BF16),
        compiler_params=_params(40, 1),
        name="sgu_gate",
    )(a, a, ln_g, ln_b, w_s, b_s_t)


def _sconv_in(h, w_in, conv_w, j):
    m_rows, k = h.shape
    dm = w_in.shape[-1] // 3
    tm, tn = 1024, 256
    nt = dm // tn
    halo = SUBLANES

    def epilogue(accs, r_refs, o_refs, s_refs):
        bg, cg, z = accs
        cw_ref = r_refs[0]
        ext = s_refs[0]

        @pl.when(pl.program_id(1) == 0)
        def _():
            ext[0:halo, :] = jnp.zeros((halo, tn), F32)

        cz = cg * z
        ext[halo:halo + tm, :] = cz
        conv = cw_ref[SHORT_CONV_WIDTH - 1:SHORT_CONV_WIDTH, :] * cz
        for back in range(1, SHORT_CONV_WIDTH):
            tap = SHORT_CONV_WIDTH - 1 - back
            conv = conv + cw_ref[tap:tap + 1, :] * ext[halo - back:halo - back + tm, :]
        ext[0:halo, :] = ext[tm:tm + halo, :]
        o_refs[0][...] = (bg * conv).astype(BF16)

    raw = [(conv_w, pl.BlockSpec((None, SHORT_CONV_WIDTH, tn), lambda n, m: (j, 0, n)))]
    out = _colmm(h, [(w_in, (j,), 0), (w_in, (j,), nt), (w_in, (j,), 2 * nt)], raw,
                 [(jax.ShapeDtypeStruct((m_rows, dm), BF16), _tile_spec(tm, tn))],
                 epilogue, tm=tm, tn=tn, nt=nt,
                 scratch=[pltpu.VMEM((tm + halo, tn), F32)], vmem_mib=48, name="sconv_in")
    return out[0]


def _router_body(h_ref, w_ref, info_ref, cum_ref, carry):
    logits = jnp.dot(h_ref[...], w_ref[...].astype(BF16), preferred_element_type=F32)
    lane = lax.broadcasted_iota(jnp.int32, logits.shape, 1)
    neg = jnp.float32(-jnp.inf)
    lg = jnp.where(lane < N_EXPERTS, logits, neg)
    v1 = jnp.max(lg, axis=-1, keepdims=True)
    lane_f = lane.astype(F32)
    i1 = jnp.min(jnp.where(lg == v1, lane_f, float(LANES)), axis=-1, keepdims=True)
    lg2 = jnp.where(lane_f == i1, neg, lg)
    v2 = jnp.max(lg2, axis=-1, keepdims=True)
    i2 = jnp.min(jnp.where(lg2 == v2, lane_f, float(LANES)), axis=-1, keepdims=True)
    e2 = jnp.exp(v2 - v1)
    den = 1.0 + e2
    w1, w2 = 1.0 / den, e2 / den

    @pl.when(pl.program_id(0) == 0)
    def _():
        carry[...] = jnp.zeros(carry.shape, F32)

    tm = logits.shape[0]
    cnt = jnp.where(lane_f == i1, 1.0, 0.0) + jnp.where(lane_f == i2, 1.0, 0.0)
    strict = jnp.where(lax.broadcasted_iota(jnp.int32, (tm, tm), 1)
                       < lax.broadcasted_iota(jnp.int32, (tm, tm), 0), 1.0, 0.0).astype(BF16)
    before = jnp.dot(strict, cnt.astype(BF16), preferred_element_type=F32) + carry[...]
    r1 = jnp.sum(jnp.where(lane_f == i1, before, 0.0), axis=-1, keepdims=True)
    r2 = jnp.sum(jnp.where(lane_f == i2, before, 0.0), axis=-1, keepdims=True)
    fields = (i1, i2, w1, w2, r1, r2)
    info = jnp.zeros(logits.shape, F32)
    for q, val in enumerate(fields):
        info = jnp.where(lane == q, val, info)
    info_ref[...] = info
    total = carry[...] + jnp.sum(cnt, axis=0, keepdims=True)
    carry[...] = total
    cum_ref[...] = total


def _router(h, w_router_padded):
    m_rows, k = h.shape
    tm = MOE_BLOCK
    return pl.pallas_call(
        _router_body,
        grid=(m_rows // tm,),
        in_specs=[pl.BlockSpec((tm, k), lambda i: (i, 0)), pl.BlockSpec((k, LANES), lambda i: (0, 0))],
        out_specs=[pl.BlockSpec((tm, LANES), lambda i: (i, 0)),
                   pl.BlockSpec((None, 1, LANES), lambda i: (i, 0, 0))],
        out_shape=[jax.ShapeDtypeStruct((m_rows, LANES), F32),
                   jax.ShapeDtypeStruct((m_rows // tm, 1, LANES), F32)],
        scratch_shapes=[pltpu.VMEM((1, LANES), F32)],
        compiler_params=_params(40, 1),
        name="router",
    )(h, w_router_padded)


def _dispatch_plan(info, cum):
    blk, n_exp = MOE_BLOCK, N_EXPERTS
    n_tok = info.shape[0]
    tbn = n_tok // blk
    nb_max = 2 * tbn + n_exp
    s_max = nb_max + n_exp * (tbn - 1)
    i32 = jnp.int32
    i1, i2 = info[:, 0].astype(i32), info[:, 1].astype(i32)
    r1, r2 = info[:, 4].astype(i32), info[:, 5].astype(i32)
    cb = jnp.concatenate([jnp.zeros((1, n_exp), i32), cum[:, 0, :n_exp].astype(i32)])
    counts = cb[-1]
    nblk = (counts + blk - 1) // blk
    blk_end = jnp.cumsum(nblk)
    blk_off = blk_end - nblk
    nb = blk_end[-1]
    slot1 = blk_off[i1] * blk + r1
    slot2 = blk_off[i2] * blk + r2

    b_idx = jnp.arange(nb_max, dtype=i32)
    b_clamped = jnp.minimum(b_idx, nb - 1)
    blk_e = jnp.minimum(jnp.searchsorted(blk_end, b_clamped, side="right").astype(i32), n_exp - 1)
    prev_e = jnp.concatenate([jnp.full((1,), -1, i32), blk_e[:-1]])
    b_valid = b_idx < nb
    blk_flag = b_valid.astype(i32) + 2 * (b_valid & (blk_e != prev_e)).astype(i32)

    k = jnp.arange(tbn, dtype=i32)[None, :, None]
    lo = jnp.maximum(k * blk, cb[:-1].T[:, None, :])
    hi = jnp.minimum((k + 1) * blk, cb[1:].T[:, None, :])
    inter = lo < hi
    n_steps = jnp.sum(inter).astype(i32)
    s_idx = jnp.arange(s_max, dtype=i32)
    s_clamped = jnp.minimum(s_idx, n_steps - 1)
    s_valid = s_idx < n_steps

    def step_lists(mask, decode):
        f = jnp.nonzero(mask.ravel(), size=s_max, fill_value=0)[0].astype(i32)[s_clamped]
        gb, tb, major = decode(f)
        prev_m = jnp.concatenate([jnp.full((1,), -1, i32), major[:-1]])
        next_m = jnp.concatenate([major[1:], jnp.full((1,), -1, i32)])
        first = s_valid & (prev_m != major)
        last = s_valid & ((next_m != major) | (s_idx == n_steps - 1))
        flag = s_valid.astype(i32) + 2 * first.astype(i32) + 4 * last.astype(i32)
        return gb, tb, flag

    def decode_slot_major(f):
        e, kk, tb = f // (tbn * tbn), (f // tbn) % tbn, f % tbn
        gb = blk_off[e] + kk
        return gb, tb, gb

    def decode_token_major(f):
        tb, e, kk = f // (n_exp * tbn), (f // tbn) % n_exp, f % tbn
        gb = blk_off[e] + kk
        return gb, tb, tb

    g_gb, g_tb, g_flag = step_lists(inter, decode_slot_major)
    c_gb, c_tb, c_flag = step_lists(inter.transpose(2, 0, 1), decode_token_major)
    return dict(slot1=slot1, slot2=slot2, w1=info[:, 2], w2=info[:, 3],
                blk_e=blk_e, blk_row=b_clamped, blk_flag=blk_flag,
                g_gb=g_gb, g_tb=g_tb, g_flag=g_flag, c_gb=c_gb, c_tb=c_tb, c_flag=c_flag,
                nb_max=nb_max, s_max=s_max)


def _moe_gather(h, plan):
    n_tok, d = h.shape
    blk = MOE_BLOCK
    tbn = n_tok // blk
    p_rows = plan["nb_max"] * blk

    def body(gb_ref, tb_ref, fl_ref, h_ref, s1_ref, s2_ref, w1_ref, w2_ref, hg_ref, ws_ref, acc, wacc):
        s = pl.program_id(0)
        flag = fl_ref[s]

        @pl.when((flag & 2) != 0)
        def _():
            acc[...] = jnp.zeros(acc.shape, F32)
            wacc[...] = jnp.zeros(wacc.shape, F32)

        @pl.when((flag & 1) != 0)
        def _():
            slot = gb_ref[s] * blk + lax.broadcasted_iota(jnp.int32, (blk, blk), 0)
            d1 = s1_ref[...] == slot
            d2 = s2_ref[...] == slot
            sel = jnp.where(d1, 1.0, jnp.where(d2, 1.0, 0.0)).astype(BF16)
            acc[...] += jnp.dot(sel, h_ref[...], preferred_element_type=F32)
            wacc[...] += jnp.sum(jnp.where(d1, w1_ref[...], 0.0) + jnp.where(d2, w2_ref[...], 0.0),
                                 axis=1, keepdims=True)

        @pl.when((flag & 4) != 0)
        def _():
            hg_ref[...] = acc[...].astype(BF16)
            ws_ref[...] = wacc[...]

    tok_row = pl.BlockSpec((None, 1, blk), lambda s, gb, tb, fl: (tb[s], 0, 0))
    grid_spec = pltpu.PrefetchScalarGridSpec(
        num_scalar_prefetch=3,
        grid=(plan["s_max"],),
        in_specs=[pl.BlockSpec((blk, d), lambda s, gb, tb, fl: (tb[s], 0)),
                  tok_row, tok_row, tok_row, tok_row],
        out_specs=[pl.BlockSpec((blk, d), lambda s, gb, tb, fl: (gb[s], 0)),
                   pl.BlockSpec((blk, 1), lambda s, gb, tb, fl: (gb[s], 0))],
        scratch_shapes=[pltpu.VMEM((blk, d), F32), pltpu.VMEM((blk, 1), F32)])
    rows = lambda a: a.reshape(tbn, 1, blk)
    return pl.pallas_call(
        body, grid_spec=grid_spec,
        out_shape=[jax.ShapeDtypeStruct((p_rows, d), BF16), jax.ShapeDtypeStruct((p_rows, 1), F32)],
        compiler_params=_params(40, 1), name="moe_gather",
    )(plan["g_gb"], plan["g_tb"], plan["g_flag"], h,
      rows(plan["slot1"]), rows(plan["slot2"]), rows(plan["w1"]), rows(plan["w2"]))


def _grouped_mm(x, weights, lead, plan, row_inputs, out_cols, epilogue, *, tn, vmem_mib, name):
    p_rows, k = x.shape
    blk = MOE_BLOCK
    nw, nr = len(weights), len(row_inputs)

    def body(be_ref, br_ref, fl_ref, x_ref, *refs):
        w_refs, r_refs = refs[:nw], refs[nw:nw + nr]
        o_ref = refs[nw + nr]
        wb_refs = refs[nw + nr + 1:]
        flag = fl_ref[pl.program_id(1)]

        @pl.when((flag & 2) != 0)
        def _():
            for w_ref, wb in zip(w_refs, wb_refs):
                wb[...] = w_ref[...].astype(BF16)

        @pl.when((flag & 1) != 0)
        def _():
            xv = x_ref[...]
            accs = [jnp.dot(xv, wb[...], preferred_element_type=F32) for wb in wb_refs]
            o_ref[...] = epilogue(accs, r_refs).astype(BF16)

    w_spec = pl.BlockSpec((None,) * (len(lead) + 1) + (k, tn),
                          lambda n, b, be, br, fl: (*lead, be[b], 0, n))
    grid_spec = pltpu.PrefetchScalarGridSpec(
        num_scalar_prefetch=3,
        grid=(out_cols // tn, plan["nb_max"]),
        in_specs=[pl.BlockSpec((blk, k), lambda n, b, be, br, fl: (br[b], 0))]
        + [w_spec] * nw
        + [pl.BlockSpec((blk, a.shape[1]), lambda n, b, be, br, fl: (br[b], 0)) for a in row_inputs],
        out_specs=pl.BlockSpec((blk, tn), lambda n, b, be, br, fl: (br[b], n)),
        scratch_shapes=[pltpu.VMEM((k, tn), BF16) for _ in weights])
    return pl.pallas_call(
        body, grid_spec=grid_spec,
        out_shape=jax.ShapeDtypeStruct((p_rows, out_cols), BF16),
        compiler_params=_params(vmem_mib, 2), name=name,
    )(plan["blk_e"], plan["blk_row"], plan["blk_flag"], x, *weights, *row_inputs)


def _moe_combine(yw, plan, x_res, gate, norm):
    n_tok, d = x_res.shape
    blk = MOE_BLOCK
    n_norm = len(norm)

    def body(gb_ref, tb_ref, fl_ref, y_ref, s1_ref, s2_ref, x_ref, g_ref, *refs):
        norm_refs, o_refs, acc = refs[:n_norm], refs[n_norm:-1], refs[-1]
        s = pl.program_id(0)
        flag = fl_ref[s]

        @pl.when((flag & 2) != 0)
        def _():
            acc[...] = jnp.zeros(acc.shape, F32)

        @pl.when((flag & 1) != 0)
        def _():
            slot = gb_ref[s] * blk + lax.broadcasted_iota(jnp.int32, (blk, blk), 1)
            sel = jnp.where(s1_ref[...] == slot, 1.0,
                            jnp.where(s2_ref[...] == slot, 1.0, 0.0)).astype(BF16)
            acc[...] += jnp.dot(sel, y_ref[...], preferred_element_type=F32)

        @pl.when((flag & 4) != 0)
        def _():
            x_new = x_ref[...] + g_ref[...] * acc[...]
            if n_norm == 1:
                o_refs[0][...] = _rms(x_new, norm_refs[0][...])
            else:
                o_refs[0][...] = x_new
                o_refs[1][...] = _rms_mod(x_new, *[r[...] for r in norm_refs]).astype(BF16)

    tok_col = pl.BlockSpec((blk, 1), lambda s, gb, tb, fl: (tb[s], 0))
    tok_blk = pl.BlockSpec((blk, d), lambda s, gb, tb, fl: (tb[s], 0))
    row = pl.BlockSpec((1, d), lambda s, gb, tb, fl: (0, 0))
    out_shape = [jax.ShapeDtypeStruct((n_tok, d), F32)]
    if n_norm > 1:
        out_shape.append(jax.ShapeDtypeStruct((n_tok, d), BF16))
    grid_spec = pltpu.PrefetchScalarGridSpec(
        num_scalar_prefetch=3,
        grid=(plan["s_max"],),
        in_specs=[pl.BlockSpec((blk, d), lambda s, gb, tb, fl: (gb[s], 0)), tok_col, tok_col, tok_blk, row]
        + [row] * n_norm,
        out_specs=[tok_blk] * len(out_shape),
        scratch_shapes=[pltpu.VMEM((blk, d), F32)])
    col = lambda a: a.reshape(n_tok, 1)
    return pl.pallas_call(
        body, grid_spec=grid_spec,
        out_shape=out_shape,
        compiler_params=_params(48, 1), name="moe_combine",
    )(plan["c_gb"], plan["c_tb"], plan["c_flag"], yw, col(plan["slot1"]), col(plan["slot2"]), x_res, gate,
      *norm)


def _moe(h, x_res, gate, w_router, w1, w3, w2, kk, norm):
    w_r = jnp.pad(w_router[kk], ((0, 0), (0, LANES - N_EXPERTS)))
    info, cum = _router(h, w_r)
    plan = _dispatch_plan(info, cum)
    hg, w_slot = _moe_gather(h, plan)
    hid = _grouped_mm(hg, [w1, w3], (kk,), plan, [], w1.shape[-1],
                      lambda accs, _: jax.nn.silu(accs[0]) * accs[1],
                      tn=1024, vmem_mib=58, name="moe_up")
    yw = _grouped_mm(hid, [w2], (kk,), plan, [w_slot], w2.shape[-1],
                     lambda accs, r_refs: accs[0] * r_refs[0][...],
                     tn=1024, vmem_mib=58, name="moe_down")
    return _moe_combine(yw, plan, x_res, gate, norm)


def kernel(x, c, ada_w, ada_b, norm1_g, norm2_g, pool_w_in, pool_w_grp, pool_scale, pool_w_out, conv_w_in, conv_b_in, conv_dw_w, conv_dw_b, conv_ln_g, conv_ln_b, conv_w_out, sgu_w_in, sgu_b_in, sgu_ln_g, sgu_ln_b, sgu_w_s, sgu_b_s, sgu_w_out, sconv_w_in, sconv_w, sconv_w_out, ffn_w1, ffn_w3, ffn_w2, moe_router, moe_w1, moe_w3, moe_w2, final_g):
    batch, seq, d = x.shape
    assert batch == 1
    depth = ada_w.shape[0]
    xs = x.reshape(seq, d)
    mod = _ada_mod(c, ada_w, ada_b)

    mods = [[mod[i, :, q * d:(q + 1) * d] for q in range(6)] for i in range(depth)]
    final_g = final_g.reshape(1, d)
    h = None
    out = None
    for i in range(depth):
        sh1, sc1, g1, sh2, sc2, g2 = mods[i]

        if h is None:
            h = _norm_mod(xs, norm1_g[i:i + 1], sc1, sh1)
        mixer, j = i % 4, i // 4
        if mixer == 0:
            a = _pool_in(h, pool_w_in, pool_w_grp, pool_scale[j:j + 1], j)
            w_out = pool_w_out
        elif mixer == 1:
            z = _glu_in(h, conv_w_in, conv_b_in[j:j + 1], j)
            a = _conv_ln(z, conv_dw_w, conv_dw_b[j:j + 1], conv_ln_g[j:j + 1], conv_ln_b[j:j + 1], j)
            w_out = conv_w_out
        elif mixer == 2:
            a = _gelu_in(h, sgu_w_in, sgu_b_in[j:j + 1], j)
            a = _sgu_gate(a, sgu_ln_g[j:j + 1], sgu_ln_b[j:j + 1], sgu_w_s, sgu_b_s[j].T, j)
            w_out = sgu_w_out
        else:
            a = _sconv_in(h, sconv_w_in, sconv_w, j)
            w_out = sconv_w_out
        xs, h = _mixer_out_norm(a, w_out, j, xs, g1, norm2_g[i:i + 1], sc2, sh2)

        kk = i // 2
        if i % 2 == 0:
            hid = _swiglu_up(h, ffn_w1, ffn_w3, (kk,), tm=1024, tn=512, vmem_mib=48, name="ffn_up")
            xs = _mm_residual(hid, ffn_w2, (kk,), xs, g2, tm=512, tn=512, vmem_mib=56, name="ffn_down")
            h = None
        elif i + 1 < depth:
            sh_n, sc_n = mods[i + 1][0], mods[i + 1][1]
            xs, h = _moe(h, xs, g2, moe_router, moe_w1, moe_w3, moe_w2, kk,
                         (norm1_g[i + 1:i + 2], sc_n, sh_n))
        else:
            (out,) = _moe(h, xs, g2, moe_router, moe_w1, moe_w3, moe_w2, kk, (final_g,))
    if out is None:
        out = _final_norm(xs, final_g)
    return out.reshape(batch, seq, d)
```

```python
import functools
import math

import jax
import jax.numpy as jnp
from jax import lax
from jax.experimental import pallas as pl
from jax.experimental.pallas import tpu as pltpu

EPS = 1e-6
POOL_WINDOWS = (2, 4, 8, 16)
CONV_WIDTH = 31
SGU_CHUNK = 128
SGU_HEADS = 8
SHORT_CONV_WIDTH = 3
N_EXPERTS = 8
MOE_ROWS = 512
MOE_SUB = 256
TOKEN_BLOCK = 512
LANES = 128
SUBLANES = 8
MIB = 1024 * 1024

BF16 = jnp.bfloat16
F32 = jnp.float32


def _params(vmem_mib, ndims):
    return pltpu.CompilerParams(
        dimension_semantics=("arbitrary",) * ndims,
        vmem_limit_bytes=vmem_mib * MIB)


def _ada_body(c_ref, w_ref, b_ref, o_ref):
    c_act = jax.nn.silu(c_ref[...])
    o_ref[0] = jnp.sum(w_ref[0] * c_act, axis=0, keepdims=True) + b_ref[0]


def _ada_mod(c, ada_w, ada_b):
    depth, d, n = ada_w.shape
    tn = 1024
    return pl.pallas_call(
        _ada_body,
        grid=(depth, n // tn),
        in_specs=[
            pl.BlockSpec((d, 1), lambda i, j: (0, 0)),
            pl.BlockSpec((1, d, tn), lambda i, j: (i, 0, j)),
            pl.BlockSpec((1, 1, tn), lambda i, j: (i, 0, j)),
        ],
        out_specs=pl.BlockSpec((1, 1, tn), lambda i, j: (i, 0, j)),
        out_shape=jax.ShapeDtypeStruct((depth, 1, n), F32),
        compiler_params=_params(40, 2),
        name="ada_mod",
    )(c.reshape(d, 1), ada_w, ada_b.reshape(depth, 1, n))


def _rms(x, g):
    return (x * lax.rsqrt(jnp.mean(x * x, axis=-1, keepdims=True) + EPS)) * g


def _rms_mod(x, g, sc, sh):
    return _rms(x, g) * (1.0 + sc) + sh


def _norm_mod_body(x_ref, g_ref, sc_ref, sh_ref, o_ref):
    o_ref[...] = _rms_mod(x_ref[...], g_ref[...], sc_ref[...], sh_ref[...]).astype(o_ref.dtype)


def _norm_mod(x, g, sc, sh):
    m, d = x.shape
    tm = 512
    row = pl.BlockSpec((1, d), lambda i: (0, 0))
    return pl.pallas_call(
        _norm_mod_body,
        grid=(m // tm,),
        in_specs=[pl.BlockSpec((tm, d), lambda i: (i, 0)), row, row, row],
        out_specs=pl.BlockSpec((tm, d), lambda i: (i, 0)),
        out_shape=jax.ShapeDtypeStruct((m, d), BF16),
        compiler_params=_params(40, 1),
        name="norm_mod",
    )(x, g, sc, sh)


def _final_norm_body(x_ref, g_ref, o_ref):
    o_ref[...] = _rms(x_ref[...], g_ref[...])


def _final_norm(x, g):
    m, d = x.shape
    tm = 512
    return pl.pallas_call(
        _final_norm_body,
        grid=(m // tm,),
        in_specs=[pl.BlockSpec((tm, d), lambda i: (i, 0)), pl.BlockSpec((1, d), lambda i: (0, 0))],
        out_specs=pl.BlockSpec((tm, d), lambda i: (i, 0)),
        out_shape=jax.ShapeDtypeStruct((m, d), F32),
        compiler_params=_params(40, 1),
        name="final_norm",
    )(x, g)


def _w_spec(w, lead, k, tn, off):
    assert w.shape[len(lead)] == k
    return pl.BlockSpec((None,) * len(lead) + (k, tn), lambda n, m: (*lead, 0, n + off))


def _row_spec(a, lead, tn, off=0):
    return pl.BlockSpec((None,) * len(lead) + (1, tn), lambda n, m: (*lead, 0, n + off))


def _colmm(x, weights, raw_inputs, outs, epilogue, *, tm, tn, nt, scratch=(), vmem_mib, name):
    m_rows, k = x.shape
    nw, nr, no = len(weights), len(raw_inputs), len(outs)

    def body(*refs):
        x_ref = refs[0]
        w_refs = refs[1:1 + nw]
        r_refs = refs[1 + nw:1 + nw + nr]
        o_refs = refs[1 + nw + nr:1 + nw + nr + no]
        s_refs = refs[1 + nw + nr + no:]
        wb_refs, extra = s_refs[:nw], s_refs[nw:]

        @pl.when(pl.program_id(1) == 0)
        def _():
            for w_ref, wb in zip(w_refs, wb_refs):
                wb[...] = w_ref[...].astype(BF16)

        xv = x_ref[...]
        accs = [jnp.dot(xv, wb[...], preferred_element_type=F32) for wb in wb_refs]
        epilogue(accs, r_refs, o_refs, extra)

    in_specs = [pl.BlockSpec((tm, k), lambda n, m: (m, 0))]
    in_specs += [_w_spec(w, lead, k, tn, off) for (w, lead, off) in weights]
    in_specs += [spec for (_, spec) in raw_inputs]
    return pl.pallas_call(
        body,
        grid=(nt, m_rows // tm),
        in_specs=in_specs,
        out_specs=[spec for (_, spec) in outs],
        out_shape=[sds for (sds, _) in outs],
        scratch_shapes=[pltpu.VMEM((k, tn), BF16) for _ in weights] + list(scratch),
        compiler_params=_params(vmem_mib, 2),
        name=name,
    )(x, *[w for (w, _, _) in weights], *[a for (a, _) in raw_inputs])


def _tile_spec(tm, tn):
    return pl.BlockSpec((tm, tn), lambda n, m: (m, n))


def _mm_residual(a, w, lead, x_res, gate, *, tm, tn, vmem_mib, name):
    m_rows = a.shape[0]
    n_cols = x_res.shape[1]
    raw = [(gate, _row_spec(gate, (), tn)), (x_res, _tile_spec(tm, tn))]

    def epilogue(accs, r_refs, o_refs, _):
        o_refs[0][...] = r_refs[1][...] + accs[0] * r_refs[0][...]

    out = _colmm(a, [(w, lead, 0)], raw,
                 [(jax.ShapeDtypeStruct((m_rows, n_cols), F32), _tile_spec(tm, tn))],
                 epilogue, tm=tm, tn=tn, nt=n_cols // tn, vmem_mib=vmem_mib, name=name)
    return out[0]


def _mixer_out_norm(a, w, j, x_res, gate, g, sc, sh):
    m_rows, k = a.shape
    d = x_res.shape[1]
    tm = 256

    def body(a_ref, w_ref, gate_ref, x_ref, g_ref, sc_ref, sh_ref, xo_ref, ho_ref, wb):
        @pl.when(pl.program_id(0) == 0)
        def _():
            wb[...] = w_ref[...].astype(BF16)

        y = jnp.dot(a_ref[...], wb[...], preferred_element_type=F32)
        x_new = x_ref[...] + gate_ref[...] * y
        xo_ref[...] = x_new
        ho_ref[...] = _rms_mod(x_new, g_ref[...], sc_ref[...], sh_ref[...]).astype(BF16)

    row = pl.BlockSpec((1, d), lambda i: (0, 0))
    blk_in = pl.BlockSpec((tm, k), lambda i: (i, 0))
    blk_d = pl.BlockSpec((tm, d), lambda i: (i, 0))
    return pl.pallas_call(
        body,
        grid=(m_rows // tm,),
        in_specs=[blk_in,
                  pl.BlockSpec((None, k, d), lambda i: (j, 0, 0), pipeline_mode=pl.Buffered(1)),
                  row, blk_d, row, row, row],
        out_specs=[blk_d, blk_d],
        out_shape=[jax.ShapeDtypeStruct((m_rows, d), F32), jax.ShapeDtypeStruct((m_rows, d), BF16)],
        scratch_shapes=[pltpu.VMEM((k, d), BF16)],
        compiler_params=_params(48, 1),
        name="mixer_out",
    )(a, w, gate, x_res, g, sc, sh)


def _swiglu_up(h, w1, w3, lead, *, tm, tn, vmem_mib, name):
    m_rows = h.shape[0]
    n_cols = w1.shape[-1]

    def epilogue(accs, r_refs, o_refs, _):
        o_refs[0][...] = (jax.nn.silu(accs[0]) * accs[1]).astype(BF16)

    out = _colmm(h, [(w1, lead, 0), (w3, lead, 0)], [],
                 [(jax.ShapeDtypeStruct((m_rows, n_cols), BF16), _tile_spec(tm, tn))],
                 epilogue, tm=tm, tn=tn, nt=n_cols // tn, vmem_mib=vmem_mib, name=name)
    return out[0]


def _pool_in(h, w_in, w_grp, scale, j):
    m_rows, k = h.shape
    n_groups = len(POOL_WINDOWS)
    tn = w_in.shape[-1] // n_groups
    tm = 1024
    halo = max(POOL_WINDOWS)

    def epilogue(accs, r_refs, o_refs, s_refs):
        z = accs[0]
        wg_ref, scale_ref = r_refs
        zext, wgb, pooled = s_refs
        n, m = pl.program_id(0), pl.program_id(1)

        @pl.when(m == 0)
        def _():
            zext[0:halo, :] = jnp.zeros((halo, tn), F32)
            wgb[...] = wg_ref[...].astype(BF16)

        zext[halo:halo + tm, :] = z
        t = m * tm + lax.broadcasted_iota(jnp.int32, (tm, tn), 0)
        for gi, win in enumerate(POOL_WINDOWS):
            @pl.when(n == gi)
            def _(win=win):
                wsum = z
                for back in range(1, win):
                    wsum = wsum + zext[halo - back:halo - back + tm, :]
                count = jnp.minimum(t + 1, win).astype(F32)
                pooled[...] = (wsum / count - z).astype(BF16)
        zext[0:halo, :] = zext[tm:tm + halo, :]
        mixed = jnp.dot(pooled[...], wgb[...], preferred_element_type=F32)
        o_refs[0][...] = (mixed * scale_ref[...]).astype(BF16)

    raw = [(w_grp, pl.BlockSpec((None, None, tn, tn), lambda n, m: (j, n, 0, 0))),
           (scale, _row_spec(scale, (), tn))]
    out = _colmm(h, [(w_in, (j,), 0)], raw,
                 [(jax.ShapeDtypeStruct((m_rows, n_groups * tn), BF16), _tile_spec(tm, tn))],
                 epilogue, tm=tm, tn=tn, nt=n_groups,
                 scratch=[pltpu.VMEM((tm + halo, tn), F32), pltpu.VMEM((tn, tn), BF16),
                          pltpu.VMEM((tm, tn), BF16)],
                 vmem_mib=48, name="pool_in")
    return out[0]


def _glu_in(h, w_in, b_in, j):
    m_rows, k = h.shape
    dm = w_in.shape[-1] // 2
    tm, tn = 1024, 512
    nt = dm // tn

    def epilogue(accs, r_refs, o_refs, _):
        a = accs[0] + r_refs[0][...]
        g = accs[1] + r_refs[1][...]
        o_refs[0][...] = (a * jax.nn.sigmoid(g)).astype(BF16)

    raw = [(b_in, _row_spec(b_in, (), tn)), (b_in, _row_spec(b_in, (), tn, off=nt))]
    out = _colmm(h, [(w_in, (j,), 0), (w_in, (j,), nt)], raw,
                 [(jax.ShapeDtypeStruct((m_rows, dm), BF16), _tile_spec(tm, tn))],
                 epilogue, tm=tm, tn=tn, nt=nt, vmem_mib=48, name="glu_in")
    return out[0]


def _conv_ln_body(z_ref, w_ref, cb_ref, g_ref, b_ref, o_ref, zsh, conv, *, tm, halo, rows, cols):
    d = z_ref.shape[1]

    @pl.when(pl.program_id(0) == 0)
    def _():
        zsh[0, 0:halo, :] = jnp.zeros((halo, d), F32)

    zsh[0, halo:halo + tm, :] = z_ref[...].astype(F32)
    for b in range(1, SUBLANES):
        zsh[b, SUBLANES:halo + tm, :] = zsh[0, SUBLANES - b:halo + tm - b, :]
    for c0 in range(0, d, cols):
        for r0 in range(0, tm, rows):
            acc = jnp.zeros((rows, cols), F32)
            for kk in range(CONV_WIDTH):
                a, b = divmod(CONV_WIDTH - 1 - kk, SUBLANES)
                start = halo + r0 - SUBLANES * a
                acc = acc + w_ref[kk:kk + 1, c0:c0 + cols] * zsh[b, start:start + rows, c0:c0 + cols]
            conv[r0:r0 + rows, c0:c0 + cols] = acc + cb_ref[:, c0:c0 + cols]
    zsh[0, 0:halo, :] = zsh[0, tm:tm + halo, :]

    y = conv[...]
    mu = jnp.mean(y, axis=-1, keepdims=True)
    var = jnp.mean(jnp.square(y - mu), axis=-1, keepdims=True)
    yn = (y - mu) * lax.rsqrt(var + EPS) * g_ref[...] + b_ref[...]
    o_ref[...] = jax.nn.silu(yn).astype(o_ref.dtype)


def _conv_ln(z, dw_w, dw_b, ln_g, ln_b, j):
    m_rows, d = z.shape
    tm, halo = 256, 32
    row = pl.BlockSpec((1, d), lambda i: (0, 0))
    return pl.pallas_call(
        functools.partial(_conv_ln_body, tm=tm, halo=halo, rows=32, cols=512),
        grid=(m_rows // tm,),
        in_specs=[pl.BlockSpec((tm, d), lambda i: (i, 0)),
                  pl.BlockSpec((None, CONV_WIDTH, d), lambda i: (j, 0, 0)), row, row, row],
        out_specs=pl.BlockSpec((tm, d), lambda i: (i, 0)),
        out_shape=jax.ShapeDtypeStruct((m_rows, d), BF16),
        scratch_shapes=[pltpu.VMEM((SUBLANES, tm + halo, d), F32), pltpu.VMEM((tm, d), F32)],
        compiler_params=_params(40, 1),
        name="conv_ln",
    )(z, dw_w, dw_b, ln_g, ln_b)


def _gelu_in(h, w_in, b_in, j):
    m_rows, k = h.shape
    n_cols = w_in.shape[-1]
    tm, tn = 512, 1024

    def epilogue(accs, r_refs, o_refs, _):
        a = accs[0] + r_refs[0][...]
        o_refs[0][...] = (0.5 * a * (1.0 + lax.erf(a * math.sqrt(0.5)))).astype(BF16)

    out = _colmm(h, [(w_in, (j,), 0)], [(b_in, _row_spec(b_in, (), tn))],
                 [(jax.ShapeDtypeStruct((m_rows, n_cols), BF16), _tile_spec(tm, tn))],
                 epilogue, tm=tm, tn=tn, nt=n_cols // tn, vmem_mib=48, name="gelu_in")
    return out[0]


def _sgu_gate_body(u_ref, v_ref, g_ref, b_ref, ws_ref, bst_ref, o_ref, *, tm):
    v = v_ref[...].astype(F32)
    mu = jnp.mean(v, axis=-1, keepdims=True)
    var = jnp.mean(jnp.square(v - mu), axis=-1, keepdims=True)
    vn = ((v - mu) * lax.rsqrt(var + EPS) * g_ref[...] + b_ref[...]).astype(BF16)
    hd = v.shape[1] // SGU_HEADS
    tri = (lax.broadcasted_iota(jnp.int32, (SGU_CHUNK, SGU_CHUNK), 0)
           >= lax.broadcasted_iota(jnp.int32, (SGU_CHUNK, SGU_CHUNK), 1))
    for hh in range(SGU_HEADS):
        wc = jnp.where(tri, ws_ref[hh], 0.0).astype(BF16)
        bias = bst_ref[:, hh:hh + 1]
        for ck in range(tm // SGU_CHUNK):
            rs = slice(ck * SGU_CHUNK, (ck + 1) * SGU_CHUNK)
            cs = slice(hh * hd, (hh + 1) * hd)
            sv = jnp.dot(wc, vn[rs, cs], preferred_element_type=F32) + bias
            o_ref[rs, cs] = (u_ref[rs, cs].astype(F32) * sv).astype(o_ref.dtype)


def _sgu_gate(a, ln_g, ln_b, w_s, b_s_t, j):
    m_rows = a.shape[0]
    d = a.shape[1] // 2
    tm = 256
    row = pl.BlockSpec((1, d), lambda i: (0, 0))
    return pl.pallas_call(
        functools.partial(_sgu_gate_body, tm=tm),
        grid=(m_rows // tm,),
        in_specs=[pl.BlockSpec((tm, d), lambda i: (i, 0)), pl.BlockSpec((tm, d), lambda i: (i, 1)),
                  row, row,
                  pl.BlockSpec((None, SGU_HEADS, SGU_CHUNK, SGU_CHUNK), lambda i: (j, 0, 0, 0)),
                  pl.BlockSpec((SGU_CHUNK, SGU_HEADS), lambda i: (0, 0))],
        out_specs=pl.BlockSpec((tm, d), lambda i: (i, 0)),
        out_shape=jax.ShapeDtypeStruct((m_rows, d), BF16),
        compiler_params=_params(40, 1),
        name="sgu_gate",
    )(a, a, ln_g, ln_b, w_s, b_s_t)


def _sconv_in(h, w_in, conv_w, j):
    m_rows, k = h.shape
    dm = w_in.shape[-1] // 3
    tm, tn = 1024, 256
    nt = dm // tn
    halo = SUBLANES

    def epilogue(accs, r_refs, o_refs, s_refs):
        bg, cg, z = accs
        cw_ref = r_refs[0]
        ext = s_refs[0]

        @pl.when(pl.program_id(1) == 0)
        def _():
            ext[0:halo, :] = jnp.zeros((halo, tn), F32)

        cz = cg * z
        ext[halo:halo + tm, :] = cz
        conv = cw_ref[SHORT_CONV_WIDTH - 1:SHORT_CONV_WIDTH, :] * cz
        for back in range(1, SHORT_CONV_WIDTH):
            tap = SHORT_CONV_WIDTH - 1 - back
            conv = conv + cw_ref[tap:tap + 1, :] * ext[halo - back:halo - back + tm, :]
        ext[0:halo, :] = ext[tm:tm + halo, :]
        o_refs[0][...] = (bg * conv).astype(BF16)

    raw = [(conv_w, pl.BlockSpec((None, SHORT_CONV_WIDTH, tn), lambda n, m: (j, 0, n)))]
    out = _colmm(h, [(w_in, (j,), 0), (w_in, (j,), nt), (w_in, (j,), 2 * nt)], raw,
                 [(jax.ShapeDtypeStruct((m_rows, dm), BF16), _tile_spec(tm, tn))],
                 epilogue, tm=tm, tn=tn, nt=nt,
                 scratch=[pltpu.VMEM((tm + halo, tn), F32)], vmem_mib=48, name="sconv_in")
    return out[0]


def _router_body(h_ref, w_ref, info_ref, cum_ref, carry):
    logits = jnp.dot(h_ref[...], w_ref[...].astype(BF16), preferred_element_type=F32)
    lane = lax.broadcasted_iota(jnp.int32, logits.shape, 1)
    neg = jnp.float32(-jnp.inf)
    lg = jnp.where(lane < N_EXPERTS, logits, neg)
    v1 = jnp.max(lg, axis=-1, keepdims=True)
    lane_f = lane.astype(F32)
    i1 = jnp.min(jnp.where(lg == v1, lane_f, float(LANES)), axis=-1, keepdims=True)
    lg2 = jnp.where(lane_f == i1, neg, lg)
    v2 = jnp.max(lg2, axis=-1, keepdims=True)
    i2 = jnp.min(jnp.where(lg2 == v2, lane_f, float(LANES)), axis=-1, keepdims=True)
    e2 = jnp.exp(v2 - v1)
    den = 1.0 + e2
    w1, w2 = 1.0 / den, e2 / den

    @pl.when(pl.program_id(0) == 0)
    def _():
        carry[...] = jnp.zeros(carry.shape, F32)

    tm = logits.shape[0]
    cnt = jnp.where(lane_f == i1, 1.0, 0.0) + jnp.where(lane_f == i2, 1.0, 0.0)
    strict = jnp.where(lax.broadcasted_iota(jnp.int32, (tm, tm), 1)
                       < lax.broadcasted_iota(jnp.int32, (tm, tm), 0), 1.0, 0.0).astype(BF16)
    before = jnp.dot(strict, cnt.astype(BF16), preferred_element_type=F32) + carry[...]
    r1 = jnp.sum(jnp.where(lane_f == i1, before, 0.0), axis=-1, keepdims=True)
    r2 = jnp.sum(jnp.where(lane_f == i2, before, 0.0), axis=-1, keepdims=True)
    fields = (i1, i2, w1, w2, r1, r2)
    info = jnp.zeros(logits.shape, F32)
    for q, val in enumerate(fields):
        info = jnp.where(lane == q, val, info)
    info_ref[...] = info[:, :info_ref.shape[1]]
    total = carry[...] + jnp.sum(cnt, axis=0, keepdims=True)
    carry[...] = total
    cum_ref[...] = total


def _router(h, w_router_padded):
    m_rows, k = h.shape
    tm = TOKEN_BLOCK
    return pl.pallas_call(
        _router_body,
        grid=(m_rows // tm,),
        in_specs=[pl.BlockSpec((tm, k), lambda i: (i, 0)), pl.BlockSpec((k, LANES), lambda i: (0, 0))],
        out_specs=[pl.BlockSpec((tm, SUBLANES), lambda i: (i, 0)),
                   pl.BlockSpec((None, 1, LANES), lambda i: (i, 0, 0))],
        out_shape=[jax.ShapeDtypeStruct((m_rows, SUBLANES), F32),
                   jax.ShapeDtypeStruct((m_rows // tm, 1, LANES), F32)],
        scratch_shapes=[pltpu.VMEM((1, LANES), F32)],
        compiler_params=_params(40, 1),
        name="router",
    )(h, w_router_padded)


def _dispatch_plan(info, cum):
    rows, sub, tok, n_exp = MOE_ROWS, MOE_SUB, TOKEN_BLOCK, N_EXPERTS
    per = rows // sub
    n_tok = info.shape[0]
    tbn = n_tok // tok
    nb_max = (2 * n_tok) // rows + n_exp
    nq = per * nb_max
    s_max = nq + n_exp * (tbn - 1)
    i32 = jnp.int32
    ids = info.astype(i32)
    i1, i2, r1, r2 = ids[:, 0], ids[:, 1], ids[:, 4], ids[:, 5]
    cb = jnp.concatenate([jnp.zeros((1, n_exp), i32), cum[:, 0, :n_exp].astype(i32)])
    counts = cb[-1]
    nblk = (counts + rows - 1) // rows
    blk_end = jnp.cumsum(nblk)
    blk_off = blk_end - nblk
    nb = blk_end[-1]
    slot1 = blk_off[i1] * rows + r1
    slot2 = blk_off[i2] * rows + r2

    b_idx = jnp.arange(nb_max, dtype=i32)
    b_used = b_idx < nb
    b_clamped = jnp.minimum(b_idx, nb - 1)
    blk_e = jnp.minimum(jnp.sum(b_clamped[:, None] >= blk_end[None, :], axis=1), n_exp - 1).astype(i32)
    prev_e = jnp.concatenate([jnp.full((1,), -1, i32), blk_e[:-1]])
    tail_rows = counts[blk_e] - (blk_end[blk_e] - 1 - blk_off[blk_e]) * rows
    half = (b_clamped == blk_end[blk_e] - 1) & (tail_rows <= rows // 2)
    blk_flag = (b_used.astype(i32) + 2 * (b_used & (blk_e != prev_e)).astype(i32)
                + 4 * (b_used & half).astype(i32))

    q = jnp.arange(nq, dtype=i32)
    e_q = jnp.minimum(jnp.sum(q[:, None] >= per * blk_end[None, :], axis=1), n_exp - 1)
    k_q = (q - per * blk_off[e_q])[:, None]
    lo = jnp.maximum(k_q * sub, cb[:-1].T[e_q])
    hi = jnp.minimum((k_q + 1) * sub, cb[1:].T[e_q])
    inter = (lo < hi) & (q < per * nb)[:, None]
    first_tb = jnp.arange(tbn, dtype=i32)[None, :] == 0
    touch = inter | (~jnp.any(inter, axis=1, keepdims=True) & first_tb)
    s_idx = jnp.arange(s_max, dtype=i32)

    def step_lists(mask, inner):
        flat = mask.ravel()
        running = jnp.cumsum(flat.astype(i32))
        n_steps = running[-1]
        s_clamped = jnp.minimum(s_idx, n_steps - 1)
        f = jnp.sum(running[None, :] <= s_clamped[:, None], axis=1).astype(i32)
        major, minor = f // inner, f % inner
        s_valid = s_idx < n_steps
        prev_m = jnp.concatenate([jnp.full((1,), -1, i32), major[:-1]])
        next_m = jnp.concatenate([major[1:], jnp.full((1,), -1, i32)])
        first = s_valid & (prev_m != major)
        last = s_valid & ((next_m != major) | (s_idx == n_steps - 1))
        return major, minor, s_valid, 2 * first.astype(i32) + 4 * last.astype(i32)

    g_q, g_tb, g_valid, g_edge = step_lists(touch, tbn)
    g_flag = (g_valid & inter[g_q, g_tb]).astype(i32) + g_edge
    c_tb, c_q, c_valid, c_edge = step_lists(inter.T, nq)
    c_flag = c_valid.astype(i32) + c_edge
    return dict(slot1=slot1, slot2=slot2, w1=info[:, 2], w2=info[:, 3],
                blk_e=blk_e, blk_row=b_clamped, blk_flag=blk_flag,
                g_q=g_q, g_tb=g_tb, g_flag=g_flag, c_q=c_q, c_tb=c_tb, c_flag=c_flag,
                nb_max=nb_max, s_max=s_max)


def _moe_gather(h, plan):
    n_tok, d = h.shape
    sub, tok = MOE_SUB, TOKEN_BLOCK
    p_rows = plan["nb_max"] * MOE_ROWS

    def body(q_ref, tb_ref, fl_ref, h_ref, s1_ref, s2_ref, w1_ref, w2_ref, hg_ref, ws_ref, acc, wacc):
        s = pl.program_id(0)
        flag = fl_ref[s]

        @pl.when((flag & 2) != 0)
        def _():
            acc[...] = jnp.zeros(acc.shape, F32)
            wacc[...] = jnp.zeros(wacc.shape, F32)

        @pl.when((flag & 1) != 0)
        def _():
            slot = q_ref[s] * sub + lax.broadcasted_iota(jnp.int32, (sub, tok), 0)
            d1 = s1_ref[...] == slot
            d2 = s2_ref[...] == slot
            sel = jnp.where(d1, 1.0, jnp.where(d2, 1.0, 0.0)).astype(BF16)
            acc[...] += jnp.dot(sel, h_ref[...], preferred_element_type=F32)
            wacc[...] += jnp.sum(jnp.where(d1, w1_ref[...], 0.0) + jnp.where(d2, w2_ref[...], 0.0),
                                 axis=1, keepdims=True)

        @pl.when((flag & 4) != 0)
        def _():
            hg_ref[...] = acc[...].astype(BF16)
            ws_ref[...] = wacc[...]

    tok_row = pl.BlockSpec((None, 1, tok), lambda s, q, tb, fl: (tb[s], 0, 0))
    grid_spec = pltpu.PrefetchScalarGridSpec(
        num_scalar_prefetch=3,
        grid=(plan["s_max"],),
        in_specs=[pl.BlockSpec((tok, d), lambda s, q, tb, fl: (tb[s], 0)),
                  tok_row, tok_row, tok_row, tok_row],
        out_specs=[pl.BlockSpec((sub, d), lambda s, q, tb, fl: (q[s], 0)),
                   pl.BlockSpec((sub, 1), lambda s, q, tb, fl: (q[s], 0))],
        scratch_shapes=[pltpu.VMEM((sub, d), F32), pltpu.VMEM((sub, 1), F32)])
    rows = lambda a: a.reshape(n_tok // tok, 1, tok)
    return pl.pallas_call(
        body, grid_spec=grid_spec,
        out_shape=[jax.ShapeDtypeStruct((p_rows, d), BF16), jax.ShapeDtypeStruct((p_rows, 1), F32)],
        compiler_params=_params(40, 1), name="moe_gather",
    )(plan["g_q"], plan["g_tb"], plan["g_flag"], h,
      rows(plan["slot1"]), rows(plan["slot2"]), rows(plan["w1"]), rows(plan["w2"]))


def _grouped_mm(x, weights, lead, plan, row_inputs, out_cols, epilogue, *, tn, vmem_mib, name):
    p_rows, k = x.shape
    blk = MOE_ROWS
    half = blk // 2
    nw, nr = len(weights), len(row_inputs)

    def body(be_ref, br_ref, fl_ref, x_ref, *refs):
        w_refs, r_refs = refs[:nw], refs[nw:nw + nr]
        o_ref = refs[nw + nr]
        wb_refs = refs[nw + nr + 1:]
        flag = fl_ref[pl.program_id(1)]

        @pl.when((flag & 2) != 0)
        def _():
            for w_ref, wb in zip(w_refs, wb_refs):
                wb[...] = w_ref[...].astype(BF16)

        def compute(n_rows):
            xv = x_ref[0:n_rows, :]
            accs = [jnp.dot(xv, wb[...], preferred_element_type=F32) for wb in wb_refs]
            o_ref[0:n_rows, :] = epilogue(accs, [r[0:n_rows, :] for r in r_refs]).astype(BF16)

        @pl.when((flag & 5) == 1)
        def _():
            compute(blk)

        @pl.when((flag & 5) == 5)
        def _():
            compute(half)
            o_ref[half:blk, :] = jnp.zeros((blk - half, tn), BF16)

        @pl.when((flag & 1) == 0)
        def _():
            o_ref[...] = jnp.zeros((blk, tn), BF16)

    w_spec = pl.BlockSpec((None,) * (len(lead) + 1) + (k, tn),
                          lambda n, b, be, br, fl: (*lead, be[b], 0, n))
    grid_spec = pltpu.PrefetchScalarGridSpec(
        num_scalar_prefetch=3,
        grid=(out_cols // tn, plan["nb_max"]),
        in_specs=[pl.BlockSpec((blk, k), lambda n, b, be, br, fl: (br[b], 0))]
        + [w_spec] * nw
        + [pl.BlockSpec((blk, a.shape[1]), lambda n, b, be, br, fl: (br[b], 0)) for a in row_inputs],
        out_specs=pl.BlockSpec((blk, tn), lambda n, b, be, br, fl: (b, n)),
        scratch_shapes=[pltpu.VMEM((k, tn), BF16) for _ in weights])
    return pl.pallas_call(
        body, grid_spec=grid_spec,
        out_shape=jax.ShapeDtypeStruct((p_rows, out_cols), BF16),
        compiler_params=_params(vmem_mib, 2), name=name,
    )(plan["blk_e"], plan["blk_row"], plan["blk_flag"], x, *weights, *row_inputs)


def _moe_combine(yw, plan, x_res, gate, norm):
    n_tok, d = x_res.shape
    sub, tok = MOE_SUB, TOKEN_BLOCK
    n_norm = len(norm)

    def body(q_ref, tb_ref, fl_ref, y_ref, s1_ref, s2_ref, x_ref, g_ref, *refs):
        norm_refs, o_refs, acc = refs[:n_norm], refs[n_norm:-1], refs[-1]
        s = pl.program_id(0)
        flag = fl_ref[s]

        @pl.when((flag & 2) != 0)
        def _():
            acc[...] = jnp.zeros(acc.shape, F32)

        @pl.when((flag & 1) != 0)
        def _():
            slot = q_ref[s] * sub + lax.broadcasted_iota(jnp.int32, (tok, sub), 1)
            sel = jnp.where(s1_ref[...] == slot, 1.0,
                            jnp.where(s2_ref[...] == slot, 1.0, 0.0)).astype(BF16)
            acc[...] += jnp.dot(sel, y_ref[...], preferred_element_type=F32)

        @pl.when((flag & 4) != 0)
        def _():
            x_new = x_ref[...] + g_ref[...] * acc[...]
            if n_norm == 1:
                o_refs[0][...] = _rms(x_new, norm_refs[0][...])
            else:
                o_refs[0][...] = x_new
                o_refs[1][...] = _rms_mod(x_new, *[r[...] for r in norm_refs]).astype(BF16)

    tok_col = pl.BlockSpec((tok, 1), lambda s, q, tb, fl: (tb[s], 0))
    tok_blk = pl.BlockSpec((tok, d), lambda s, q, tb, fl: (tb[s], 0))
    row = pl.BlockSpec((1, d), lambda s, q, tb, fl: (0, 0))
    out_shape = [jax.ShapeDtypeStruct((n_tok, d), F32)]
    if n_norm > 1:
        out_shape.append(jax.ShapeDtypeStruct((n_tok, d), BF16))
    grid_spec = pltpu.PrefetchScalarGridSpec(
        num_scalar_prefetch=3,
        grid=(plan["s_max"],),
        in_specs=[pl.BlockSpec((sub, d), lambda s, q, tb, fl: (q[s], 0)), tok_col, tok_col, tok_blk, row]
        + [row] * n_norm,
        out_specs=[tok_blk] * len(out_shape),
        scratch_shapes=[pltpu.VMEM((tok, d), F32)])
    col = lambda a: a.reshape(n_tok, 1)
    return pl.pallas_call(
        body, grid_spec=grid_spec,
        out_shape=out_shape,
        compiler_params=_params(48, 1), name="moe_combine",
    )(plan["c_q"], plan["c_tb"], plan["c_flag"], yw, col(plan["slot1"]), col(plan["slot2"]), x_res, gate,
      *norm)


def _moe(h, x_res, gate, w_router, w1, w3, w2, kk, norm):
    w_r = jnp.pad(w_router[kk], ((0, 0), (0, LANES - N_EXPERTS)))
    info, cum = _router(h, w_r)
    plan = _dispatch_plan(info, cum)
    hg, w_slot = _moe_gather(h, plan)
    hid = _grouped_mm(hg, [w1, w3], (kk,), plan, [], w1.shape[-1],
                      lambda accs, _: jax.nn.silu(accs[0]) * accs[1],
                      tn=1024, vmem_mib=58, name="moe_up")
    yw = _grouped_mm(hid, [w2], (kk,), plan, [w_slot], w2.shape[-1],
                     lambda accs, rows: accs[0] * rows[0],
                     tn=1024, vmem_mib=58, name="moe_down")
    return _moe_combine(yw, plan, x_res, gate, norm)


def kernel(x, c, ada_w, ada_b, norm1_g, norm2_g, pool_w_in, pool_w_grp, pool_scale, pool_w_out, conv_w_in, conv_b_in, conv_dw_w, conv_dw_b, conv_ln_g, conv_ln_b, conv_w_out, sgu_w_in, sgu_b_in, sgu_ln_g, sgu_ln_b, sgu_w_s, sgu_b_s, sgu_w_out, sconv_w_in, sconv_w, sconv_w_out, ffn_w1, ffn_w3, ffn_w2, moe_router, moe_w1, moe_w3, moe_w2, final_g):
    batch, seq, d = x.shape
    assert batch == 1
    depth = ada_w.shape[0]
    xs = x.reshape(seq, d)
    mod = _ada_mod(c, ada_w, ada_b)

    mods = [[mod[i, :, q * d:(q + 1) * d] for q in range(6)] for i in range(depth)]
    final_g = final_g.reshape(1, d)
    h = None
    out = None
    for i in range(depth):
        sh1, sc1, g1, sh2, sc2, g2 = mods[i]

        if h is None:
            h = _norm_mod(xs, norm1_g[i:i + 1], sc1, sh1)
        mixer, j = i % 4, i // 4
        if mixer == 0:
            a = _pool_in(h, pool_w_in, pool_w_grp, pool_scale[j:j + 1], j)
            w_out = pool_w_out
        elif mixer == 1:
            z = _glu_in(h, conv_w_in, conv_b_in[j:j + 1], j)
            a = _conv_ln(z, conv_dw_w, conv_dw_b[j:j + 1], conv_ln_g[j:j + 1], conv_ln_b[j:j + 1], j)
            w_out = conv_w_out
        elif mixer == 2:
            a = _gelu_in(h, sgu_w_in, sgu_b_in[j:j + 1], j)
            a = _sgu_gate(a, sgu_ln_g[j:j + 1], sgu_ln_b[j:j + 1], sgu_w_s, sgu_b_s[j].T, j)
            w_out = sgu_w_out
        else:
            a = _sconv_in(h, sconv_w_in, sconv_w, j)
            w_out = sconv_w_out
        xs, h = _mixer_out_norm(a, w_out, j, xs, g1, norm2_g[i:i + 1], sc2, sh2)

        kk = i // 2
        if i % 2 == 0:
            hid = _swiglu_up(h, ffn_w1, ffn_w3, (kk,), tm=1024, tn=512, vmem_mib=48, name="ffn_up")
            xs = _mm_residual(hid, ffn_w2, (kk,), xs, g2, tm=512, tn=512, vmem_mib=56, name="ffn_down")
            h = None
        elif i + 1 < depth:
            sh_n, sc_n = mods[i + 1][0], mods[i + 1][1]
            xs, h = _moe(h, xs, g2, moe_router, moe_w1, moe_w3, moe_w2, kk,
                         (norm1_g[i + 1:i + 2], sc_n, sh_n))
        else:
            (out,) = _moe(h, xs, g2, moe_router, moe_w1, moe_w3, moe_w2, kk, (final_g,))
    if out is None:
        out = _final_norm(xs, final_g)
    return out.reshape(batch, seq, d)
```

```python
import functools
import math

import jax
import jax.numpy as jnp
from jax import lax
from jax.experimental import pallas as pl
from jax.experimental.pallas import tpu as pltpu

EPS = 1e-6
POOL_WINDOWS = (2, 4, 8, 16)
CONV_WIDTH = 31
SGU_CHUNK = 128
SGU_HEADS = 8
SHORT_CONV_WIDTH = 3
N_EXPERTS = 8
MOE_ROWS = 512
MOE_SUB = 256
TOKEN_BLOCK = 512
LANES = 128
SUBLANES = 8
MIB = 1024 * 1024

BF16 = jnp.bfloat16
F32 = jnp.float32


def _params(vmem_mib, ndims):
    return pltpu.CompilerParams(
        dimension_semantics=("arbitrary",) * ndims,
        vmem_limit_bytes=vmem_mib * MIB)


def _ada_body(c_ref, w_ref, b_ref, o_ref):
    c_act = jax.nn.silu(c_ref[...])
    o_ref[0] = jnp.sum(w_ref[0] * c_act, axis=0, keepdims=True) + b_ref[0]


def _ada_mod(c, ada_w, ada_b):
    depth, d, n = ada_w.shape
    tn = 1024
    return pl.pallas_call(
        _ada_body,
        grid=(depth, n // tn),
        in_specs=[
            pl.BlockSpec((d, 1), lambda i, j: (0, 0)),
            pl.BlockSpec((1, d, tn), lambda i, j: (i, 0, j)),
            pl.BlockSpec((1, 1, tn), lambda i, j: (i, 0, j)),
        ],
        out_specs=pl.BlockSpec((1, 1, tn), lambda i, j: (i, 0, j)),
        out_shape=jax.ShapeDtypeStruct((depth, 1, n), F32),
        compiler_params=_params(40, 2),
        name="ada_mod",
    )(c.reshape(d, 1), ada_w, ada_b.reshape(depth, 1, n))


def _rms(x, g):
    return (x * lax.rsqrt(jnp.mean(x * x, axis=-1, keepdims=True) + EPS)) * g


def _rms_mod(x, g, sc, sh):
    return _rms(x, g) * (1.0 + sc) + sh


def _norm_mod_body(x_ref, g_ref, sc_ref, sh_ref, o_ref):
    o_ref[...] = _rms_mod(x_ref[...], g_ref[...], sc_ref[...], sh_ref[...]).astype(o_ref.dtype)


def _norm_mod(x, g, sc, sh):
    m, d = x.shape
    tm = 512
    row = pl.BlockSpec((1, d), lambda i: (0, 0))
    return pl.pallas_call(
        _norm_mod_body,
        grid=(m // tm,),
        in_specs=[pl.BlockSpec((tm, d), lambda i: (i, 0)), row, row, row],
        out_specs=pl.BlockSpec((tm, d), lambda i: (i, 0)),
        out_shape=jax.ShapeDtypeStruct((m, d), BF16),
        compiler_params=_params(40, 1),
        name="norm_mod",
    )(x, g, sc, sh)


def _final_norm_body(x_ref, g_ref, o_ref):
    o_ref[...] = _rms(x_ref[...], g_ref[...])


def _final_norm(x, g):
    m, d = x.shape
    tm = 512
    return pl.pallas_call(
        _final_norm_body,
        grid=(m // tm,),
        in_specs=[pl.BlockSpec((tm, d), lambda i: (i, 0)), pl.BlockSpec((1, d), lambda i: (0, 0))],
        out_specs=pl.BlockSpec((tm, d), lambda i: (i, 0)),
        out_shape=jax.ShapeDtypeStruct((m, d), F32),
        compiler_params=_params(40, 1),
        name="final_norm",
    )(x, g)


def _w_spec(w, lead, k, tn, off):
    assert w.shape[len(lead)] == k
    return pl.BlockSpec((None,) * len(lead) + (k, tn), lambda n, m: (*lead, 0, n + off))


def _row_spec(a, lead, tn, off=0):
    return pl.BlockSpec((None,) * len(lead) + (1, tn), lambda n, m: (*lead, 0, n + off))


def _colmm(x, weights, raw_inputs, outs, epilogue, *, tm, tn, nt, scratch=(), vmem_mib, name):
    m_rows, k = x.shape
    nw, nr, no = len(weights), len(raw_inputs), len(outs)

    def body(*refs):
        x_ref = refs[0]
        w_refs = refs[1:1 + nw]
        r_refs = refs[1 + nw:1 + nw + nr]
        o_refs = refs[1 + nw + nr:1 + nw + nr + no]
        s_refs = refs[1 + nw + nr + no:]
        wb_refs, extra = s_refs[:nw], s_refs[nw:]

        @pl.when(pl.program_id(1) == 0)
        def _():
            for w_ref, wb in zip(w_refs, wb_refs):
                wb[...] = w_ref[...].astype(BF16)

        xv = x_ref[...]
        accs = [jnp.dot(xv, wb[...], preferred_element_type=F32) for wb in wb_refs]
        epilogue(accs, r_refs, o_refs, extra)

    in_specs = [pl.BlockSpec((tm, k), lambda n, m: (m, 0))]
    in_specs += [_w_spec(w, lead, k, tn, off) for (w, lead, off) in weights]
    in_specs += [spec for (_, spec) in raw_inputs]
    return pl.pallas_call(
        body,
        grid=(nt, m_rows // tm),
        in_specs=in_specs,
        out_specs=[spec for (_, spec) in outs],
        out_shape=[sds for (sds, _) in outs],
        scratch_shapes=[pltpu.VMEM((k, tn), BF16) for _ in weights] + list(scratch),
        compiler_params=_params(vmem_mib, 2),
        name=name,
    )(x, *[w for (w, _, _) in weights], *[a for (a, _) in raw_inputs])


def _tile_spec(tm, tn):
    return pl.BlockSpec((tm, tn), lambda n, m: (m, n))


def _mm_residual(a, w, lead, x_res, gate, *, tm, tn, vmem_mib, name):
    m_rows = a.shape[0]
    n_cols = x_res.shape[1]
    raw = [(gate, _row_spec(gate, (), tn)), (x_res, _tile_spec(tm, tn))]

    def epilogue(accs, r_refs, o_refs, _):
        o_refs[0][...] = r_refs[1][...] + accs[0] * r_refs[0][...]

    out = _colmm(a, [(w, lead, 0)], raw,
                 [(jax.ShapeDtypeStruct((m_rows, n_cols), F32), _tile_spec(tm, tn))],
                 epilogue, tm=tm, tn=tn, nt=n_cols // tn, vmem_mib=vmem_mib, name=name)
    return out[0]


def _mixer_out_norm(a, w, j, x_res, gate, g, sc, sh):
    m_rows, k = a.shape
    d = x_res.shape[1]
    tm = 256

    def body(a_ref, w_ref, gate_ref, x_ref, g_ref, sc_ref, sh_ref, xo_ref, ho_ref, wb):
        @pl.when(pl.program_id(0) == 0)
        def _():
            wb[...] = w_ref[...].astype(BF16)

        y = jnp.dot(a_ref[...], wb[...], preferred_element_type=F32)
        x_new = x_ref[...] + gate_ref[...] * y
        xo_ref[...] = x_new
        ho_ref[...] = _rms_mod(x_new, g_ref[...], sc_ref[...], sh_ref[...]).astype(BF16)

    row = pl.BlockSpec((1, d), lambda i: (0, 0))
    blk_in = pl.BlockSpec((tm, k), lambda i: (i, 0))
    blk_d = pl.BlockSpec((tm, d), lambda i: (i, 0))
    return pl.pallas_call(
        body,
        grid=(m_rows // tm,),
        in_specs=[blk_in,
                  pl.BlockSpec((None, k, d), lambda i: (j, 0, 0), pipeline_mode=pl.Buffered(1)),
                  row, blk_d, row, row, row],
        out_specs=[blk_d, blk_d],
        out_shape=[jax.ShapeDtypeStruct((m_rows, d), F32), jax.ShapeDtypeStruct((m_rows, d), BF16)],
        scratch_shapes=[pltpu.VMEM((k, d), BF16)],
        compiler_params=_params(48, 1),
        name="mixer_out",
    )(a, w, gate, x_res, g, sc, sh)


def _swiglu_up(h, w1, w3, lead, *, tm, tn, vmem_mib, name):
    m_rows = h.shape[0]
    n_cols = w1.shape[-1]

    def epilogue(accs, r_refs, o_refs, _):
        o_refs[0][...] = (jax.nn.silu(accs[0]) * accs[1]).astype(BF16)

    out = _colmm(h, [(w1, lead, 0), (w3, lead, 0)], [],
                 [(jax.ShapeDtypeStruct((m_rows, n_cols), BF16), _tile_spec(tm, tn))],
                 epilogue, tm=tm, tn=tn, nt=n_cols // tn, vmem_mib=vmem_mib, name=name)
    return out[0]


def _pool_in(h, w_in, w_grp, scale, j):
    m_rows, k = h.shape
    n_groups = len(POOL_WINDOWS)
    tn = w_in.shape[-1] // n_groups
    tm = 1024
    halo = max(POOL_WINDOWS)

    def epilogue(accs, r_refs, o_refs, s_refs):
        z = accs[0]
        wg_ref, scale_ref = r_refs
        zext, wgb, pooled = s_refs
        n, m = pl.program_id(0), pl.program_id(1)

        @pl.when(m == 0)
        def _():
            zext[0:halo, :] = jnp.zeros((halo, tn), F32)
            wgb[...] = wg_ref[...].astype(BF16)

        zext[halo:halo + tm, :] = z
        t = m * tm + lax.broadcasted_iota(jnp.int32, (tm, tn), 0)
        for gi, win in enumerate(POOL_WINDOWS):
            @pl.when(n == gi)
            def _(win=win):
                wsum = z
                for back in range(1, win):
                    wsum = wsum + zext[halo - back:halo - back + tm, :]
                count = jnp.minimum(t + 1, win).astype(F32)
                pooled[...] = (wsum / count - z).astype(BF16)
        zext[0:halo, :] = zext[tm:tm + halo, :]
        mixed = jnp.dot(pooled[...], wgb[...], preferred_element_type=F32)
        o_refs[0][...] = (mixed * scale_ref[...]).astype(BF16)

    raw = [(w_grp, pl.BlockSpec((None, None, tn, tn), lambda n, m: (j, n, 0, 0))),
           (scale, _row_spec(scale, (), tn))]
    out = _colmm(h, [(w_in, (j,), 0)], raw,
                 [(jax.ShapeDtypeStruct((m_rows, n_groups * tn), BF16), _tile_spec(tm, tn))],
                 epilogue, tm=tm, tn=tn, nt=n_groups,
                 scratch=[pltpu.VMEM((tm + halo, tn), F32), pltpu.VMEM((tn, tn), BF16),
                          pltpu.VMEM((tm, tn), BF16)],
                 vmem_mib=48, name="pool_in")
    return out[0]


def _glu_in(h, w_in, b_in, j):
    m_rows, k = h.shape
    dm = w_in.shape[-1] // 2
    tm, tn = 1024, 512
    nt = dm // tn

    def epilogue(accs, r_refs, o_refs, _):
        a = accs[0] + r_refs[0][...]
        g = accs[1] + r_refs[1][...]
        o_refs[0][...] = (a * jax.nn.sigmoid(g)).astype(BF16)

    raw = [(b_in, _row_spec(b_in, (), tn)), (b_in, _row_spec(b_in, (), tn, off=nt))]
    out = _colmm(h, [(w_in, (j,), 0), (w_in, (j,), nt)], raw,
                 [(jax.ShapeDtypeStruct((m_rows, dm), BF16), _tile_spec(tm, tn))],
                 epilogue, tm=tm, tn=tn, nt=nt, vmem_mib=48, name="glu_in")
    return out[0]


def _conv_ln_body(z_ref, w_ref, cb_ref, g_ref, b_ref, o_ref, zsh, wbc, conv, *, tm, halo, rows, cols):
    d = z_ref.shape[1]
    sl = SUBLANES

    @pl.when(pl.program_id(0) == 0)
    def _():
        zsh[0, 0:halo, :] = jnp.zeros((halo, d), F32)
        for kk in range(CONV_WIDTH):
            wbc[kk] = jnp.broadcast_to(w_ref[kk:kk + 1, :], (sl, d))

    zsh[0, halo:halo + tm, :] = z_ref[...].astype(F32)
    for b in range(1, sl):
        zsh[b, sl:halo + tm, :] = zsh[0, sl - b:halo + tm - b, :]

    for c0 in range(0, d, cols):
        cs = slice(c0, c0 + cols)
        for r0 in range(0, tm, rows):
            acc = jnp.zeros((rows, cols), F32)
            for kk in range(CONV_WIDTH):
                a, b = divmod(CONV_WIDTH - 1 - kk, sl)
                start = halo + r0 - sl * a
                wv = jnp.tile(wbc[kk, :, cs], (rows // sl, 1))
                acc = acc + wv * zsh[b, start:start + rows, cs]
            conv[r0:r0 + rows, cs] = acc + cb_ref[:, cs]
    zsh[0, 0:halo, :] = zsh[0, tm:tm + halo, :]

    y = conv[...]
    mu = jnp.mean(y, axis=-1, keepdims=True)
    var = jnp.mean(jnp.square(y - mu), axis=-1, keepdims=True)
    yn = (y - mu) * lax.rsqrt(var + EPS) * g_ref[...] + b_ref[...]
    o_ref[...] = jax.nn.silu(yn).astype(o_ref.dtype)


def _conv_ln(z, dw_w, dw_b, ln_g, ln_b, j):
    m_rows, d = z.shape
    tm, halo = 256, 32
    row = pl.BlockSpec((1, d), lambda i: (0, 0))
    return pl.pallas_call(
        functools.partial(_conv_ln_body, tm=tm, halo=halo, rows=64, cols=512),
        grid=(m_rows // tm,),
        in_specs=[pl.BlockSpec((tm, d), lambda i: (i, 0)),
                  pl.BlockSpec((None, CONV_WIDTH, d), lambda i: (j, 0, 0)), row, row, row],
        out_specs=pl.BlockSpec((tm, d), lambda i: (i, 0)),
        out_shape=jax.ShapeDtypeStruct((m_rows, d), BF16),
        scratch_shapes=[pltpu.VMEM((SUBLANES, tm + halo, d), F32),
                        pltpu.VMEM((CONV_WIDTH, SUBLANES, d), F32), pltpu.VMEM((tm, d), F32)],
        compiler_params=_params(40, 1),
        name="conv_ln",
    )(z, dw_w, dw_b, ln_g, ln_b)


def _gelu_in(h, w_in, b_in, j):
    m_rows, k = h.shape
    n_cols = w_in.shape[-1]
    tm, tn = 512, 1024

    def epilogue(accs, r_refs, o_refs, _):
        a = accs[0] + r_refs[0][...]
        o_refs[0][...] = (0.5 * a * (1.0 + lax.erf(a * math.sqrt(0.5)))).astype(BF16)

    out = _colmm(h, [(w_in, (j,), 0)], [(b_in, _row_spec(b_in, (), tn))],
                 [(jax.ShapeDtypeStruct((m_rows, n_cols), BF16), _tile_spec(tm, tn))],
                 epilogue, tm=tm, tn=tn, nt=n_cols // tn, vmem_mib=48, name="gelu_in")
    return out[0]


def _sgu_gate_body(u_ref, v_ref, g_ref, b_ref, ws_ref, bst_ref, o_ref, *, tm):
    v = v_ref[...].astype(F32)
    mu = jnp.mean(v, axis=-1, keepdims=True)
    var = jnp.mean(jnp.square(v - mu), axis=-1, keepdims=True)
    vn = ((v - mu) * lax.rsqrt(var + EPS) * g_ref[...] + b_ref[...]).astype(BF16)
    hd = v.shape[1] // SGU_HEADS
    tri = (lax.broadcasted_iota(jnp.int32, (SGU_CHUNK, SGU_CHUNK), 0)
           >= lax.broadcasted_iota(jnp.int32, (SGU_CHUNK, SGU_CHUNK), 1))
    for hh in range(SGU_HEADS):
        wc = jnp.where(tri, ws_ref[hh], 0.0).astype(BF16)
        bias = bst_ref[:, hh:hh + 1]
        for ck in range(tm // SGU_CHUNK):
            rs = slice(ck * SGU_CHUNK, (ck + 1) * SGU_CHUNK)
            cs = slice(hh * hd, (hh + 1) * hd)
            sv = jnp.dot(wc, vn[rs, cs], preferred_element_type=F32) + bias
            o_ref[rs, cs] = (u_ref[rs, cs].astype(F32) * sv).astype(o_ref.dtype)


def _sgu_gate(a, ln_g, ln_b, w_s, b_s_t, j):
    m_rows = a.shape[0]
    d = a.shape[1] // 2
    tm = 256
    row = pl.BlockSpec((1, d), lambda i: (0, 0))
    return pl.pallas_call(
        functools.partial(_sgu_gate_body, tm=tm),
        grid=(m_rows // tm,),
        in_specs=[pl.BlockSpec((tm, d), lambda i: (i, 0)), pl.BlockSpec((tm, d), lambda i: (i, 1)),
                  row, row,
                  pl.BlockSpec((None, SGU_HEADS, SGU_CHUNK, SGU_CHUNK), lambda i: (j, 0, 0, 0)),
                  pl.BlockSpec((SGU_CHUNK, SGU_HEADS), lambda i: (0, 0))],
        out_specs=pl.BlockSpec((tm, d), lambda i: (i, 0)),
        out_shape=jax.ShapeDtypeStruct((m_rows, d), BF16),
        compiler_params=_params(40, 1),
        name="sgu_gate",
    )(a, a, ln_g, ln_b, w_s, b_s_t)


def _sconv_in(h, w_in, conv_w, j):
    m_rows, k = h.shape
    dm = w_in.shape[-1] // 3
    tm, tn = 1024, 256
    nt = dm // tn
    halo = SUBLANES

    def epilogue(accs, r_refs, o_refs, s_refs):
        bg, cg, z = accs
        cw_ref = r_refs[0]
        ext = s_refs[0]

        @pl.when(pl.program_id(1) == 0)
        def _():
            ext[0:halo, :] = jnp.zeros((halo, tn), F32)

        cz = cg * z
        ext[halo:halo + tm, :] = cz
        conv = cw_ref[SHORT_CONV_WIDTH - 1:SHORT_CONV_WIDTH, :] * cz
        for back in range(1, SHORT_CONV_WIDTH):
            tap = SHORT_CONV_WIDTH - 1 - back
            conv = conv + cw_ref[tap:tap + 1, :] * ext[halo - back:halo - back + tm, :]
        ext[0:halo, :] = ext[tm:tm + halo, :]
        o_refs[0][...] = (bg * conv).astype(BF16)

    raw = [(conv_w, pl.BlockSpec((None, SHORT_CONV_WIDTH, tn), lambda n, m: (j, 0, n)))]
    out = _colmm(h, [(w_in, (j,), 0), (w_in, (j,), nt), (w_in, (j,), 2 * nt)], raw,
                 [(jax.ShapeDtypeStruct((m_rows, dm), BF16), _tile_spec(tm, tn))],
                 epilogue, tm=tm, tn=tn, nt=nt,
                 scratch=[pltpu.VMEM((tm + halo, tn), F32)], vmem_mib=48, name="sconv_in")
    return out[0]


def _router_body(h_ref, w_ref, info_ref, cum_ref, carry):
    logits = jnp.dot(h_ref[...], w_ref[...].astype(BF16), preferred_element_type=F32)
    lane = lax.broadcasted_iota(jnp.int32, logits.shape, 1)
    neg = jnp.float32(-jnp.inf)
    lg = jnp.where(lane < N_EXPERTS, logits, neg)
    v1 = jnp.max(lg, axis=-1, keepdims=True)
    lane_f = lane.astype(F32)
    i1 = jnp.min(jnp.where(lg == v1, lane_f, float(LANES)), axis=-1, keepdims=True)
    lg2 = jnp.where(lane_f == i1, neg, lg)
    v2 = jnp.max(lg2, axis=-1, keepdims=True)
    i2 = jnp.min(jnp.where(lg2 == v2, lane_f, float(LANES)), axis=-1, keepdims=True)
    e2 = jnp.exp(v2 - v1)
    den = 1.0 + e2
    w1, w2 = 1.0 / den, e2 / den

    @pl.when(pl.program_id(0) == 0)
    def _():
        carry[...] = jnp.zeros(carry.shape, F32)

    tm = logits.shape[0]
    cnt = jnp.where(lane_f == i1, 1.0, 0.0) + jnp.where(lane_f == i2, 1.0, 0.0)
    strict = jnp.where(lax.broadcasted_iota(jnp.int32, (tm, tm), 1)
                       < lax.broadcasted_iota(jnp.int32, (tm, tm), 0), 1.0, 0.0).astype(BF16)
    before = jnp.dot(strict, cnt.astype(BF16), preferred_element_type=F32) + carry[...]
    r1 = jnp.sum(jnp.where(lane_f == i1, before, 0.0), axis=-1, keepdims=True)
    r2 = jnp.sum(jnp.where(lane_f == i2, before, 0.0), axis=-1, keepdims=True)
    fields = (i1, i2, w1, w2, r1, r2)
    info = jnp.zeros(logits.shape, F32)
    for q, val in enumerate(fields):
        info = jnp.where(lane == q, val, info)
    info_ref[...] = info[:, :info_ref.shape[1]]
    total = carry[...] + jnp.sum(cnt, axis=0, keepdims=True)
    carry[...] = total
    cum_ref[...] = total


def _router(h, w_router_padded):
    m_rows, k = h.shape
    tm = TOKEN_BLOCK
    return pl.pallas_call(
        _router_body,
        grid=(m_rows // tm,),
        in_specs=[pl.BlockSpec((tm, k), lambda i: (i, 0)), pl.BlockSpec((k, LANES), lambda i: (0, 0))],
        out_specs=[pl.BlockSpec((tm, SUBLANES), lambda i: (i, 0)),
                   pl.BlockSpec((None, 1, LANES), lambda i: (i, 0, 0))],
        out_shape=[jax.ShapeDtypeStruct((m_rows, SUBLANES), F32),
                   jax.ShapeDtypeStruct((m_rows // tm, 1, LANES), F32)],
        scratch_shapes=[pltpu.VMEM((1, LANES), F32)],
        compiler_params=_params(40, 1),
        name="router",
    )(h, w_router_padded)


def _dispatch_plan(info, cum):
    rows, sub, tok, n_exp = MOE_ROWS, MOE_SUB, TOKEN_BLOCK, N_EXPERTS
    per = rows // sub
    n_tok = info.shape[0]
    tbn = n_tok // tok
    nb_max = (2 * n_tok) // rows + n_exp
    nq = per * nb_max
    s_max = nq + n_exp * (tbn - 1)
    i32 = jnp.int32
    ids = info.astype(i32)
    i1, i2, r1, r2 = ids[:, 0], ids[:, 1], ids[:, 4], ids[:, 5]
    cb = jnp.concatenate([jnp.zeros((1, n_exp), i32), cum[:, 0, :n_exp].astype(i32)])
    counts = cb[-1]
    nblk = (counts + rows - 1) // rows
    blk_end = jnp.cumsum(nblk)
    blk_off = blk_end - nblk
    nb = blk_end[-1]
    slot1 = blk_off[i1] * rows + r1
    slot2 = blk_off[i2] * rows + r2

    b_idx = jnp.arange(nb_max, dtype=i32)
    b_used = b_idx < nb
    b_clamped = jnp.minimum(b_idx, nb - 1)
    blk_e = jnp.minimum(jnp.sum(b_clamped[:, None] >= blk_end[None, :], axis=1), n_exp - 1).astype(i32)
    prev_e = jnp.concatenate([jnp.full((1,), -1, i32), blk_e[:-1]])
    tail_rows = counts[blk_e] - (blk_end[blk_e] - 1 - blk_off[blk_e]) * rows
    half = (b_clamped == blk_end[blk_e] - 1) & (tail_rows <= rows // 2)
    blk_flag = (b_used.astype(i32) + 2 * (b_used & (blk_e != prev_e)).astype(i32)
                + 4 * (b_used & half).astype(i32))

    q = jnp.arange(nq, dtype=i32)
    e_q = jnp.minimum(jnp.sum(q[:, None] >= per * blk_end[None, :], axis=1), n_exp - 1)
    k_q = (q - per * blk_off[e_q])[:, None]
    lo = jnp.maximum(k_q * sub, cb[:-1].T[e_q])
    hi = jnp.minimum((k_q + 1) * sub, cb[1:].T[e_q])
    inter = (lo < hi) & (q < per * nb)[:, None]
    first_tb = jnp.arange(tbn, dtype=i32)[None, :] == 0
    touch = inter | (~jnp.any(inter, axis=1, keepdims=True) & first_tb)
    s_idx = jnp.arange(s_max, dtype=i32)

    def step_lists(mask, inner):
        flat = mask.ravel()
        running = jnp.cumsum(flat.astype(i32))
        n_steps = running[-1]
        s_clamped = jnp.minimum(s_idx, n_steps - 1)
        f = jnp.sum(running[None, :] <= s_clamped[:, None], axis=1).astype(i32)
        major, minor = f // inner, f % inner
        s_valid = s_idx < n_steps
        prev_m = jnp.concatenate([jnp.full((1,), -1, i32), major[:-1]])
        next_m = jnp.concatenate([major[1:], jnp.full((1,), -1, i32)])
        first = s_valid & (prev_m != major)
        last = s_valid & ((next_m != major) | (s_idx == n_steps - 1))
        return major, minor, s_valid, 2 * first.astype(i32) + 4 * last.astype(i32)

    g_q, g_tb, g_valid, g_edge = step_lists(touch, tbn)
    g_flag = (g_valid & inter[g_q, g_tb]).astype(i32) + g_edge
    c_tb, c_q, c_valid, c_edge = step_lists(inter.T, nq)
    c_flag = c_valid.astype(i32) + c_edge
    return dict(slot1=slot1, slot2=slot2, w1=info[:, 2], w2=info[:, 3],
                blk_e=blk_e, blk_row=b_clamped, blk_flag=blk_flag,
                g_q=g_q, g_tb=g_tb, g_flag=g_flag, c_q=c_q, c_tb=c_tb, c_flag=c_flag,
                nb_max=nb_max, s_max=s_max)


def _moe_gather(h, plan):
    n_tok, d = h.shape
    sub, tok = MOE_SUB, TOKEN_BLOCK
    p_rows = plan["nb_max"] * MOE_ROWS

    def body(q_ref, tb_ref, fl_ref, h_ref, s1_ref, s2_ref, w1_ref, w2_ref, hg_ref, ws_ref, acc, wacc):
        s = pl.program_id(0)
        flag = fl_ref[s]

        @pl.when((flag & 2) != 0)
        def _():
            acc[...] = jnp.zeros(acc.shape, F32)
            wacc[...] = jnp.zeros(wacc.shape, F32)

        @pl.when((flag & 1) != 0)
        def _():
            slot = q_ref[s] * sub + lax.broadcasted_iota(jnp.int32, (sub, tok), 0)
            d1 = s1_ref[...] == slot
            d2 = s2_ref[...] == slot
            sel = jnp.where(d1, 1.0, jnp.where(d2, 1.0, 0.0)).astype(BF16)
            t0 = pl.multiple_of(tb_ref[s] * tok, tok)
            acc[...] += jnp.dot(sel, h_ref[pl.ds(t0, tok), :], preferred_element_type=F32)
            wacc[...] += jnp.sum(jnp.where(d1, w1_ref[...], 0.0) + jnp.where(d2, w2_ref[...], 0.0),
                                 axis=1, keepdims=True)

        @pl.when((flag & 4) != 0)
        def _():
            hg_ref[...] = acc[...].astype(BF16)
            ws_ref[...] = wacc[...]

    tok_row = pl.BlockSpec((None, 1, tok), lambda s, q, tb, fl: (tb[s], 0, 0))
    grid_spec = pltpu.PrefetchScalarGridSpec(
        num_scalar_prefetch=3,
        grid=(plan["s_max"],),
        in_specs=[pl.BlockSpec((n_tok, d), lambda s, q, tb, fl: (0, 0), pipeline_mode=pl.Buffered(1)),
                  tok_row, tok_row, tok_row, tok_row],
        out_specs=[pl.BlockSpec((sub, d), lambda s, q, tb, fl: (q[s], 0)),
                   pl.BlockSpec((sub, 1), lambda s, q, tb, fl: (q[s], 0))],
        scratch_shapes=[pltpu.VMEM((sub, d), F32), pltpu.VMEM((sub, 1), F32)])
    rows = lambda a: a.reshape(n_tok // tok, 1, tok)
    return pl.pallas_call(
        body, grid_spec=grid_spec,
        out_shape=[jax.ShapeDtypeStruct((p_rows, d), BF16), jax.ShapeDtypeStruct((p_rows, 1), F32)],
        compiler_params=_params(48, 1), name="moe_gather",
    )(plan["g_q"], plan["g_tb"], plan["g_flag"], h,
      rows(plan["slot1"]), rows(plan["slot2"]), rows(plan["w1"]), rows(plan["w2"]))


def _grouped_mm(x, weights, lead, plan, row_inputs, out_cols, epilogue, *, tn, vmem_mib, name):
    p_rows, k = x.shape
    blk = MOE_ROWS
    half = blk // 2
    nw, nr = len(weights), len(row_inputs)

    def body(be_ref, br_ref, fl_ref, x_ref, *refs):
        w_refs, r_refs = refs[:nw], refs[nw:nw + nr]
        o_ref = refs[nw + nr]
        wb_refs = refs[nw + nr + 1:]
        flag = fl_ref[pl.program_id(1)]

        @pl.when((flag & 2) != 0)
        def _():
            for w_ref, wb in zip(w_refs, wb_refs):
                wb[...] = w_ref[...].astype(BF16)

        def compute(n_rows):
            xv = x_ref[0:n_rows, :]
            accs = [jnp.dot(xv, wb[...], preferred_element_type=F32) for wb in wb_refs]
            o_ref[0:n_rows, :] = epilogue(accs, [r[0:n_rows, :] for r in r_refs]).astype(BF16)

        @pl.when((flag & 5) == 1)
        def _():
            compute(blk)

        @pl.when((flag & 5) == 5)
        def _():
            compute(half)
            o_ref[half:blk, :] = jnp.zeros((blk - half, tn), BF16)

        @pl.when((flag & 1) == 0)
        def _():
            o_ref[...] = jnp.zeros((blk, tn), BF16)

    w_spec = pl.BlockSpec((None,) * (len(lead) + 1) + (k, tn),
                          lambda n, b, be, br, fl: (*lead, be[b], 0, n))
    grid_spec = pltpu.PrefetchScalarGridSpec(
        num_scalar_prefetch=3,
        grid=(out_cols // tn, plan["nb_max"]),
        in_specs=[pl.BlockSpec((blk, k), lambda n, b, be, br, fl: (br[b], 0))]
        + [w_spec] * nw
        + [pl.BlockSpec((blk, a.shape[1]), lambda n, b, be, br, fl: (br[b], 0)) for a in row_inputs],
        out_specs=pl.BlockSpec((blk, tn), lambda n, b, be, br, fl: (b, n)),
        scratch_shapes=[pltpu.VMEM((k, tn), BF16) for _ in weights])
    return pl.pallas_call(
        body, grid_spec=grid_spec,
        out_shape=jax.ShapeDtypeStruct((p_rows, out_cols), BF16),
        compiler_params=_params(vmem_mib, 2), name=name,
    )(plan["blk_e"], plan["blk_row"], plan["blk_flag"], x, *weights, *row_inputs)


def _moe_combine(yw, plan, x_res, gate, norm):
    n_tok, d = x_res.shape
    sub, tok = MOE_SUB, TOKEN_BLOCK
    n_norm = len(norm)

    def body(q_ref, tb_ref, fl_ref, y_ref, s1_ref, s2_ref, x_ref, g_ref, *refs):
        norm_refs, o_refs, acc = refs[:n_norm], refs[n_norm:-1], refs[-1]
        s = pl.program_id(0)
        flag = fl_ref[s]

        @pl.when((flag & 2) != 0)
        def _():
            acc[...] = jnp.zeros(acc.shape, F32)

        @pl.when((flag & 1) != 0)
        def _():
            slot = q_ref[s] * sub + lax.broadcasted_iota(jnp.int32, (tok, sub), 1)
            sel = jnp.where(s1_ref[...] == slot, 1.0,
                            jnp.where(s2_ref[...] == slot, 1.0, 0.0)).astype(BF16)
            acc[...] += jnp.dot(sel, y_ref[...], preferred_element_type=F32)

        @pl.when((flag & 4) != 0)
        def _():
            x_new = x_ref[...] + g_ref[...] * acc[...]
            if n_norm == 1:
                o_refs[0][...] = _rms(x_new, norm_refs[0][...])
            else:
                o_refs[0][...] = x_new
                o_refs[1][...] = _rms_mod(x_new, *[r[...] for r in norm_refs]).astype(BF16)

    tok_col = pl.BlockSpec((tok, 1), lambda s, q, tb, fl: (tb[s], 0))
    tok_blk = pl.BlockSpec((tok, d), lambda s, q, tb, fl: (tb[s], 0))
    row = pl.BlockSpec((1, d), lambda s, q, tb, fl: (0, 0))
    out_shape = [jax.ShapeDtypeStruct((n_tok, d), F32)]
    if n_norm > 1:
        out_shape.append(jax.ShapeDtypeStruct((n_tok, d), BF16))
    grid_spec = pltpu.PrefetchScalarGridSpec(
        num_scalar_prefetch=3,
        grid=(plan["s_max"],),
        in_specs=[pl.BlockSpec((sub, d), lambda s, q, tb, fl: (q[s], 0)), tok_col, tok_col, tok_blk, row]
        + [row] * n_norm,
        out_specs=[tok_blk] * len(out_shape),
        scratch_shapes=[pltpu.VMEM((tok, d), F32)])
    col = lambda a: a.reshape(n_tok, 1)
    return pl.pallas_call(
        body, grid_spec=grid_spec,
        out_shape=out_shape,
        compiler_params=_params(48, 1), name="moe_combine",
    )(plan["c_q"], plan["c_tb"], plan["c_flag"], yw, col(plan["slot1"]), col(plan["slot2"]), x_res, gate,
      *norm)


def _moe(h, x_res, gate, w_router, w1, w3, w2, kk, norm):
    w_r = jnp.pad(w_router[kk], ((0, 0), (0, LANES - N_EXPERTS)))
    info, cum = _router(h, w_r)
    plan = _dispatch_plan(info, cum)
    hg, w_slot = _moe_gather(h, plan)
    hid = _grouped_mm(hg, [w1, w3], (kk,), plan, [], w1.shape[-1],
                      lambda accs, _: jax.nn.silu(accs[0]) * accs[1],
                      tn=1024, vmem_mib=58, name="moe_up")
    yw = _grouped_mm(hid, [w2], (kk,), plan, [w_slot], w2.shape[-1],
                     lambda accs, rows: accs[0] * rows[0],
                     tn=1024, vmem_mib=58, name="moe_down")
    return _moe_combine(yw, plan, x_res, gate, norm)


def kernel(x, c, ada_w, ada_b, norm1_g, norm2_g, pool_w_in, pool_w_grp, pool_scale, pool_w_out, conv_w_in, conv_b_in, conv_dw_w, conv_dw_b, conv_ln_g, conv_ln_b, conv_w_out, sgu_w_in, sgu_b_in, sgu_ln_g, sgu_ln_b, sgu_w_s, sgu_b_s, sgu_w_out, sconv_w_in, sconv_w, sconv_w_out, ffn_w1, ffn_w3, ffn_w2, moe_router, moe_w1, moe_w3, moe_w2, final_g):
    batch, seq, d = x.shape
    assert batch == 1
    depth = ada_w.shape[0]
    xs = x.reshape(seq, d)
    mod = _ada_mod(c, ada_w, ada_b)

    mods = [[mod[i, :, q * d:(q + 1) * d] for q in range(6)] for i in range(depth)]
    final_g = final_g.reshape(1, d)
    h = None
    out = None
    for i in range(depth):
        sh1, sc1, g1, sh2, sc2, g2 = mods[i]

        if h is None:
            h = _norm_mod(xs, norm1_g[i:i + 1], sc1, sh1)
        mixer, j = i % 4, i // 4
        if mixer == 0:
            a = _pool_in(h, pool_w_in, pool_w_grp, pool_scale[j:j + 1], j)
            w_out = pool_w_out
        elif mixer == 1:
            z = _glu_in(h, conv_w_in, conv_b_in[j:j + 1], j)
            a = _conv_ln(z, conv_dw_w, conv_dw_b[j:j + 1], conv_ln_g[j:j + 1], conv_ln_b[j:j + 1], j)
            w_out = conv_w_out
        elif mixer == 2:
            a = _gelu_in(h, sgu_w_in, sgu_b_in[j:j + 1], j)
            a = _sgu_gate(a, sgu_ln_g[j:j + 1], sgu_ln_b[j:j + 1], sgu_w_s, sgu_b_s[j].T, j)
            w_out = sgu_w_out
        else:
            a = _sconv_in(h, sconv_w_in, sconv_w, j)
            w_out = sconv_w_out
        xs, h = _mixer_out_norm(a, w_out, j, xs, g1, norm2_g[i:i + 1], sc2, sh2)

        kk = i // 2
        if i % 2 == 0:
            hid = _swiglu_up(h, ffn_w1, ffn_w3, (kk,), tm=1024, tn=512, vmem_mib=48, name="ffn_up")
            xs = _mm_residual(hid, ffn_w2, (kk,), xs, g2, tm=512, tn=512, vmem_mib=56, name="ffn_down")
            h = None
        elif i + 1 < depth:
            sh_n, sc_n = mods[i + 1][0], mods[i + 1][1]
            xs, h = _moe(h, xs, g2, moe_router, moe_w1, moe_w3, moe_w2, kk,
                         (norm1_g[i + 1:i + 2], sc_n, sh_n))
        else:
            (out,) = _moe(h, xs, g2, moe_router, moe_w1, moe_w3, moe_w2, kk, (final_g,))
    if out is None:
        out = _final_norm(xs, final_g)
    return out.reshape(batch, seq, d)
```

```python
import functools
import math

import jax
import jax.numpy as jnp
from jax import lax
from jax.experimental import pallas as pl
from jax.experimental.pallas import tpu as pltpu

EPS = 1e-6
POOL_WINDOWS = (2, 4, 8, 16)
CONV_WIDTH = 31
SGU_CHUNK = 128
SGU_HEADS = 8
SHORT_CONV_WIDTH = 3
N_EXPERTS = 8
MOE_ROWS = 512
MOE_SUB = 256
TOKEN_BLOCK = 512
LANES = 128
SUBLANES = 8
MIB = 1024 * 1024

BF16 = jnp.bfloat16
F32 = jnp.float32


def _params(vmem_mib, ndims):
    return pltpu.CompilerParams(
        dimension_semantics=("arbitrary",) * ndims,
        vmem_limit_bytes=vmem_mib * MIB)


def _ada_body(c_ref, w_ref, b_ref, o_ref):
    c_act = jax.nn.silu(c_ref[...])
    o_ref[0] = jnp.sum(w_ref[0] * c_act, axis=0, keepdims=True) + b_ref[0]


def _ada_mod(c, ada_w, ada_b):
    depth, d, n = ada_w.shape
    tn = 1024
    return pl.pallas_call(
        _ada_body,
        grid=(depth, n // tn),
        in_specs=[
            pl.BlockSpec((d, 1), lambda i, j: (0, 0)),
            pl.BlockSpec((1, d, tn), lambda i, j: (i, 0, j)),
            pl.BlockSpec((1, 1, tn), lambda i, j: (i, 0, j)),
        ],
        out_specs=pl.BlockSpec((1, 1, tn), lambda i, j: (i, 0, j)),
        out_shape=jax.ShapeDtypeStruct((depth, 1, n), F32),
        compiler_params=_params(40, 2),
        name="ada_mod",
    )(c.reshape(d, 1), ada_w, ada_b.reshape(depth, 1, n))


def _rms(x, g):
    return (x * lax.rsqrt(jnp.mean(x * x, axis=-1, keepdims=True) + EPS)) * g


def _rms_mod(x, g, sc, sh):
    return _rms(x, g) * (1.0 + sc) + sh


def _norm_mod_body(x_ref, g_ref, sc_ref, sh_ref, o_ref):
    o_ref[...] = _rms_mod(x_ref[...], g_ref[...], sc_ref[...], sh_ref[...]).astype(o_ref.dtype)


def _norm_mod(x, g, sc, sh):
    m, d = x.shape
    tm = 512
    row = pl.BlockSpec((1, d), lambda i: (0, 0))
    return pl.pallas_call(
        _norm_mod_body,
        grid=(m // tm,),
        in_specs=[pl.BlockSpec((tm, d), lambda i: (i, 0)), row, row, row],
        out_specs=pl.BlockSpec((tm, d), lambda i: (i, 0)),
        out_shape=jax.ShapeDtypeStruct((m, d), BF16),
        compiler_params=_params(40, 1),
        name="norm_mod",
    )(x, g, sc, sh)


def _final_norm_body(x_ref, g_ref, o_ref):
    o_ref[...] = _rms(x_ref[...], g_ref[...])


def _final_norm(x, g):
    m, d = x.shape
    tm = 512
    return pl.pallas_call(
        _final_norm_body,
        grid=(m // tm,),
        in_specs=[pl.BlockSpec((tm, d), lambda i: (i, 0)), pl.BlockSpec((1, d), lambda i: (0, 0))],
        out_specs=pl.BlockSpec((tm, d), lambda i: (i, 0)),
        out_shape=jax.ShapeDtypeStruct((m, d), F32),
        compiler_params=_params(40, 1),
        name="final_norm",
    )(x, g)


def _w_spec(w, lead, k, tn, off):
    assert w.shape[len(lead)] == k
    return pl.BlockSpec((None,) * len(lead) + (k, tn), lambda n, m: (*lead, 0, n + off))


def _row_spec(a, lead, tn, off=0):
    return pl.BlockSpec((None,) * len(lead) + (1, tn), lambda n, m: (*lead, 0, n + off))


def _colmm(x, weights, raw_inputs, outs, epilogue, *, tm, tn, nt, scratch=(), vmem_mib, name):
    m_rows, k = x.shape
    nw, nr, no = len(weights), len(raw_inputs), len(outs)

    def body(*refs):
        x_ref = refs[0]
        w_refs = refs[1:1 + nw]
        r_refs = refs[1 + nw:1 + nw + nr]
        o_refs = refs[1 + nw + nr:1 + nw + nr + no]
        s_refs = refs[1 + nw + nr + no:]
        wb_refs, extra = s_refs[:nw], s_refs[nw:]

        @pl.when(pl.program_id(1) == 0)
        def _():
            for w_ref, wb in zip(w_refs, wb_refs):
                wb[...] = w_ref[...].astype(BF16)

        xv = x_ref[...]
        accs = [jnp.dot(xv, wb[...], preferred_element_type=F32) for wb in wb_refs]
        epilogue(accs, r_refs, o_refs, extra)

    in_specs = [pl.BlockSpec((tm, k), lambda n, m: (m, 0))]
    in_specs += [_w_spec(w, lead, k, tn, off) for (w, lead, off) in weights]
    in_specs += [spec for (_, spec) in raw_inputs]
    return pl.pallas_call(
        body,
        grid=(nt, m_rows // tm),
        in_specs=in_specs,
        out_specs=[spec for (_, spec) in outs],
        out_shape=[sds for (sds, _) in outs],
        scratch_shapes=[pltpu.VMEM((k, tn), BF16) for _ in weights] + list(scratch),
        compiler_params=_params(vmem_mib, 2),
        name=name,
    )(x, *[w for (w, _, _) in weights], *[a for (a, _) in raw_inputs])


def _tile_spec(tm, tn):
    return pl.BlockSpec((tm, tn), lambda n, m: (m, n))


def _mm_residual(a, w, lead, x_res, gate, *, tm, tn, vmem_mib, name):
    m_rows = a.shape[0]
    n_cols = x_res.shape[1]
    raw = [(gate, _row_spec(gate, (), tn)), (x_res, _tile_spec(tm, tn))]

    def epilogue(accs, r_refs, o_refs, _):
        o_refs[0][...] = r_refs[1][...] + accs[0] * r_refs[0][...]

    out = _colmm(a, [(w, lead, 0)], raw,
                 [(jax.ShapeDtypeStruct((m_rows, n_cols), F32), _tile_spec(tm, tn))],
                 epilogue, tm=tm, tn=tn, nt=n_cols // tn, vmem_mib=vmem_mib, name=name)
    return out[0]


def _mixer_out_norm(a, w, j, x_res, gate, g, sc, sh):
    m_rows, k = a.shape
    d = x_res.shape[1]
    tm = 256

    def body(a_ref, w_ref, gate_ref, x_ref, g_ref, sc_ref, sh_ref, xo_ref, ho_ref, wb):
        @pl.when(pl.program_id(0) == 0)
        def _():
            wb[...] = w_ref[...].astype(BF16)

        y = jnp.dot(a_ref[...], wb[...], preferred_element_type=F32)
        x_new = x_ref[...] + gate_ref[...] * y
        xo_ref[...] = x_new
        ho_ref[...] = _rms_mod(x_new, g_ref[...], sc_ref[...], sh_ref[...]).astype(BF16)

    row = pl.BlockSpec((1, d), lambda i: (0, 0))
    blk_in = pl.BlockSpec((tm, k), lambda i: (i, 0))
    blk_d = pl.BlockSpec((tm, d), lambda i: (i, 0))
    return pl.pallas_call(
        body,
        grid=(m_rows // tm,),
        in_specs=[blk_in,
                  pl.BlockSpec((None, k, d), lambda i: (j, 0, 0), pipeline_mode=pl.Buffered(1)),
                  row, blk_d, row, row, row],
        out_specs=[blk_d, blk_d],
        out_shape=[jax.ShapeDtypeStruct((m_rows, d), F32), jax.ShapeDtypeStruct((m_rows, d), BF16)],
        scratch_shapes=[pltpu.VMEM((k, d), BF16)],
        compiler_params=_params(48, 1),
        name="mixer_out",
    )(a, w, gate, x_res, g, sc, sh)


def _swiglu_up(h, w1, w3, lead, *, tm, tn, vmem_mib, name):
    m_rows = h.shape[0]
    n_cols = w1.shape[-1]

    def epilogue(accs, r_refs, o_refs, _):
        o_refs[0][...] = (jax.nn.silu(accs[0]) * accs[1]).astype(BF16)

    out = _colmm(h, [(w1, lead, 0), (w3, lead, 0)], [],
                 [(jax.ShapeDtypeStruct((m_rows, n_cols), BF16), _tile_spec(tm, tn))],
                 epilogue, tm=tm, tn=tn, nt=n_cols // tn, vmem_mib=vmem_mib, name=name)
    return out[0]


def _pool_in(h, w_in, w_grp, scale, j):
    m_rows, k = h.shape
    n_groups = len(POOL_WINDOWS)
    tn = w_in.shape[-1] // n_groups
    tm = 1024
    halo = max(POOL_WINDOWS)

    def epilogue(accs, r_refs, o_refs, s_refs):
        z = accs[0]
        wg_ref, scale_ref = r_refs
        zext, wgb, pooled = s_refs
        n, m = pl.program_id(0), pl.program_id(1)

        @pl.when(m == 0)
        def _():
            zext[0:halo, :] = jnp.zeros((halo, tn), F32)
            wgb[...] = wg_ref[...].astype(BF16)

        zext[halo:halo + tm, :] = z
        t = m * tm + lax.broadcasted_iota(jnp.int32, (tm, tn), 0)
        for gi, win in enumerate(POOL_WINDOWS):
            @pl.when(n == gi)
            def _(win=win):
                wsum = z
                for back in range(1, win):
                    wsum = wsum + zext[halo - back:halo - back + tm, :]
                count = jnp.minimum(t + 1, win).astype(F32)
                pooled[...] = (wsum / count - z).astype(BF16)
        zext[0:halo, :] = zext[tm:tm + halo, :]
        mixed = jnp.dot(pooled[...], wgb[...], preferred_element_type=F32)
        o_refs[0][...] = (mixed * scale_ref[...]).astype(BF16)

    raw = [(w_grp, pl.BlockSpec((None, None, tn, tn), lambda n, m: (j, n, 0, 0))),
           (scale, _row_spec(scale, (), tn))]
    out = _colmm(h, [(w_in, (j,), 0)], raw,
                 [(jax.ShapeDtypeStruct((m_rows, n_groups * tn), BF16), _tile_spec(tm, tn))],
                 epilogue, tm=tm, tn=tn, nt=n_groups,
                 scratch=[pltpu.VMEM((tm + halo, tn), F32), pltpu.VMEM((tn, tn), BF16),
                          pltpu.VMEM((tm, tn), BF16)],
                 vmem_mib=48, name="pool_in")
    return out[0]


def _glu_in(h, w_in, b_in, j):
    m_rows, k = h.shape
    dm = w_in.shape[-1] // 2
    tm, tn = 1024, 512
    nt = dm // tn

    def epilogue(accs, r_refs, o_refs, _):
        a = accs[0] + r_refs[0][...]
        g = accs[1] + r_refs[1][...]
        o_refs[0][...] = (a * jax.nn.sigmoid(g)).astype(BF16)

    raw = [(b_in, _row_spec(b_in, (), tn)), (b_in, _row_spec(b_in, (), tn, off=nt))]
    out = _colmm(h, [(w_in, (j,), 0), (w_in, (j,), nt)], raw,
                 [(jax.ShapeDtypeStruct((m_rows, dm), BF16), _tile_spec(tm, tn))],
                 epilogue, tm=tm, tn=tn, nt=nt, vmem_mib=48, name="glu_in")
    return out[0]


def _conv_ln_body(z_ref, w_ref, cb_ref, g_ref, b_ref, o_ref, zsh, wbc, conv, *, tm, halo, rows, cols):
    d = z_ref.shape[1]
    sl = SUBLANES

    @pl.when(pl.program_id(0) == 0)
    def _():
        zsh[0, 0:halo, :] = jnp.zeros((halo, d), F32)
        for kk in range(CONV_WIDTH):
            wbc[kk] = jnp.broadcast_to(w_ref[kk:kk + 1, :], (sl, d))

    zsh[0, halo:halo + tm, :] = z_ref[...].astype(F32)
    for b in range(1, sl):
        zsh[b, sl:halo + tm, :] = zsh[0, sl - b:halo + tm - b, :]

    for c0 in range(0, d, cols):
        cs = slice(c0, c0 + cols)
        for r0 in range(0, tm, rows):
            acc = jnp.zeros((rows, cols), F32)
            for kk in range(CONV_WIDTH):
                a, b = divmod(CONV_WIDTH - 1 - kk, sl)
                start = halo + r0 - sl * a
                wv = jnp.tile(wbc[kk, :, cs], (rows // sl, 1))
                acc = acc + wv * zsh[b, start:start + rows, cs]
            conv[r0:r0 + rows, cs] = acc + cb_ref[:, cs]
    zsh[0, 0:halo, :] = zsh[0, tm:tm + halo, :]

    y = conv[...]
    mu = jnp.mean(y, axis=-1, keepdims=True)
    var = jnp.mean(jnp.square(y - mu), axis=-1, keepdims=True)
    yn = (y - mu) * lax.rsqrt(var + EPS) * g_ref[...] + b_ref[...]
    o_ref[...] = jax.nn.silu(yn).astype(o_ref.dtype)


def _conv_ln(z, dw_w, dw_b, ln_g, ln_b, j):
    m_rows, d = z.shape
    tm, halo = 256, 32
    row = pl.BlockSpec((1, d), lambda i: (0, 0))
    return pl.pallas_call(
        functools.partial(_conv_ln_body, tm=tm, halo=halo, rows=64, cols=512),
        grid=(m_rows // tm,),
        in_specs=[pl.BlockSpec((tm, d), lambda i: (i, 0)),
                  pl.BlockSpec((None, CONV_WIDTH, d), lambda i: (j, 0, 0)), row, row, row],
        out_specs=pl.BlockSpec((tm, d), lambda i: (i, 0)),
        out_shape=jax.ShapeDtypeStruct((m_rows, d), BF16),
        scratch_shapes=[pltpu.VMEM((SUBLANES, tm + halo, d), F32),
                        pltpu.VMEM((CONV_WIDTH, SUBLANES, d), F32), pltpu.VMEM((tm, d), F32)],
        compiler_params=_params(40, 1),
        name="conv_ln",
    )(z, dw_w, dw_b, ln_g, ln_b)


def _gelu_in(h, w_in, b_in, j):
    m_rows, k = h.shape
    n_cols = w_in.shape[-1]
    tm, tn = 512, 1024

    def epilogue(accs, r_refs, o_refs, _):
        a = accs[0] + r_refs[0][...]
        o_refs[0][...] = (0.5 * a * (1.0 + lax.erf(a * math.sqrt(0.5)))).astype(BF16)

    out = _colmm(h, [(w_in, (j,), 0)], [(b_in, _row_spec(b_in, (), tn))],
                 [(jax.ShapeDtypeStruct((m_rows, n_cols), BF16), _tile_spec(tm, tn))],
                 epilogue, tm=tm, tn=tn, nt=n_cols // tn, vmem_mib=48, name="gelu_in")
    return out[0]


def _sgu_gate_body(u_ref, v_ref, g_ref, b_ref, ws_ref, bst_ref, o_ref, *, tm):
    v = v_ref[...].astype(F32)
    mu = jnp.mean(v, axis=-1, keepdims=True)
    var = jnp.mean(jnp.square(v - mu), axis=-1, keepdims=True)
    vn = ((v - mu) * lax.rsqrt(var + EPS) * g_ref[...] + b_ref[...]).astype(BF16)
    hd = v.shape[1] // SGU_HEADS
    tri = (lax.broadcasted_iota(jnp.int32, (SGU_CHUNK, SGU_CHUNK), 0)
           >= lax.broadcasted_iota(jnp.int32, (SGU_CHUNK, SGU_CHUNK), 1))
    for hh in range(SGU_HEADS):
        wc = jnp.where(tri, ws_ref[hh], 0.0).astype(BF16)
        bias = bst_ref[:, hh:hh + 1]
        for ck in range(tm // SGU_CHUNK):
            rs = slice(ck * SGU_CHUNK, (ck + 1) * SGU_CHUNK)
            cs = slice(hh * hd, (hh + 1) * hd)
            sv = jnp.dot(wc, vn[rs, cs], preferred_element_type=F32) + bias
            o_ref[rs, cs] = (u_ref[rs, cs].astype(F32) * sv).astype(o_ref.dtype)


def _sgu_gate(a, ln_g, ln_b, w_s, b_s_t, j):
    m_rows = a.shape[0]
    d = a.shape[1] // 2
    tm = 256
    row = pl.BlockSpec((1, d), lambda i: (0, 0))
    return pl.pallas_call(
        functools.partial(_sgu_gate_body, tm=tm),
        grid=(m_rows // tm,),
        in_specs=[pl.BlockSpec((tm, d), lambda i: (i, 0)), pl.BlockSpec((tm, d), lambda i: (i, 1)),
                  row, row,
                  pl.BlockSpec((None, SGU_HEADS, SGU_CHUNK, SGU_CHUNK), lambda i: (j, 0, 0, 0)),
                  pl.BlockSpec((SGU_CHUNK, SGU_HEADS), lambda i: (0, 0))],
        out_specs=pl.BlockSpec((tm, d), lambda i: (i, 0)),
        out_shape=jax.ShapeDtypeStruct((m_rows, d), BF16),
        compiler_params=_params(40, 1),
        name="sgu_gate",
    )(a, a, ln_g, ln_b, w_s, b_s_t)


def _sconv_in(h, w_in, conv_w, j):
    m_rows, k = h.shape
    dm = w_in.shape[-1] // 3
    tm, tn = 1024, 256
    nt = dm // tn
    halo = SUBLANES

    def epilogue(accs, r_refs, o_refs, s_refs):
        bg, cg, z = accs
        cw_ref = r_refs[0]
        ext = s_refs[0]

        @pl.when(pl.program_id(1) == 0)
        def _():
            ext[0:halo, :] = jnp.zeros((halo, tn), F32)

        cz = cg * z
        ext[halo:halo + tm, :] = cz
        conv = cw_ref[SHORT_CONV_WIDTH - 1:SHORT_CONV_WIDTH, :] * cz
        for back in range(1, SHORT_CONV_WIDTH):
            tap = SHORT_CONV_WIDTH - 1 - back
            conv = conv + cw_ref[tap:tap + 1, :] * ext[halo - back:halo - back + tm, :]
        ext[0:halo, :] = ext[tm:tm + halo, :]
        o_refs[0][...] = (bg * conv).astype(BF16)

    raw = [(conv_w, pl.BlockSpec((None, SHORT_CONV_WIDTH, tn), lambda n, m: (j, 0, n)))]
    out = _colmm(h, [(w_in, (j,), 0), (w_in, (j,), nt), (w_in, (j,), 2 * nt)], raw,
                 [(jax.ShapeDtypeStruct((m_rows, dm), BF16), _tile_spec(tm, tn))],
                 epilogue, tm=tm, tn=tn, nt=nt,
                 scratch=[pltpu.VMEM((tm + halo, tn), F32)], vmem_mib=48, name="sconv_in")
    return out[0]


def _router_body(h_ref, w_ref, info_ref, cum_ref, carry):
    logits = jnp.dot(h_ref[...], w_ref[...].astype(BF16), preferred_element_type=F32)
    lane = lax.broadcasted_iota(jnp.int32, logits.shape, 1)
    neg = jnp.float32(-jnp.inf)
    lg = jnp.where(lane < N_EXPERTS, logits, neg)
    v1 = jnp.max(lg, axis=-1, keepdims=True)
    lane_f = lane.astype(F32)
    i1 = jnp.min(jnp.where(lg == v1, lane_f, float(LANES)), axis=-1, keepdims=True)
    lg2 = jnp.where(lane_f == i1, neg, lg)
    v2 = jnp.max(lg2, axis=-1, keepdims=True)
    i2 = jnp.min(jnp.where(lg2 == v2, lane_f, float(LANES)), axis=-1, keepdims=True)
    e2 = jnp.exp(v2 - v1)
    den = 1.0 + e2
    w1, w2 = 1.0 / den, e2 / den

    @pl.when(pl.program_id(0) == 0)
    def _():
        carry[...] = jnp.zeros(carry.shape, F32)

    tm = logits.shape[0]
    cnt = jnp.where(lane_f == i1, 1.0, 0.0) + jnp.where(lane_f == i2, 1.0, 0.0)
    strict = jnp.where(lax.broadcasted_iota(jnp.int32, (tm, tm), 1)
                       < lax.broadcasted_iota(jnp.int32, (tm, tm), 0), 1.0, 0.0).astype(BF16)
    before = jnp.dot(strict, cnt.astype(BF16), preferred_element_type=F32) + carry[...]
    r1 = jnp.sum(jnp.where(lane_f == i1, before, 0.0), axis=-1, keepdims=True)
    r2 = jnp.sum(jnp.where(lane_f == i2, before, 0.0), axis=-1, keepdims=True)
    fields = (i1, i2, w1, w2, r1, r2)
    info = jnp.zeros(logits.shape, F32)
    for q, val in enumerate(fields):
        info = jnp.where(lane == q, val, info)
    info_ref[...] = info[:, :info_ref.shape[1]]
    total = carry[...] + jnp.sum(cnt, axis=0, keepdims=True)
    carry[...] = total
    cum_ref[...] = total


def _router(h, w_router_padded):
    m_rows, k = h.shape
    tm = MOE_SUB
    return pl.pallas_call(
        _router_body,
        grid=(m_rows // tm,),
        in_specs=[pl.BlockSpec((tm, k), lambda i: (i, 0)), pl.BlockSpec((k, LANES), lambda i: (0, 0))],
        out_specs=[pl.BlockSpec((tm, SUBLANES), lambda i: (i, 0)),
                   pl.BlockSpec((None, 1, LANES), lambda i: (i, 0, 0))],
        out_shape=[jax.ShapeDtypeStruct((m_rows, SUBLANES), F32),
                   jax.ShapeDtypeStruct((m_rows // tm, 1, LANES), F32)],
        scratch_shapes=[pltpu.VMEM((1, LANES), F32)],
        compiler_params=_params(40, 1),
        name="router",
    )(h, w_router_padded)


def _dispatch_plan(info, cum):
    rows, sub, tok, n_exp = MOE_ROWS, MOE_SUB, TOKEN_BLOCK, N_EXPERTS
    per = rows // sub
    n_tok = info.shape[0]
    tbn = n_tok // tok
    nb_max = (2 * n_tok) // rows + n_exp
    nq = per * nb_max
    s_max = nq + n_exp * (tbn - 1)
    i32 = jnp.int32
    ids = info.astype(i32)
    i1, i2, r1, r2 = ids[:, 0], ids[:, 1], ids[:, 4], ids[:, 5]
    cb_fine = jnp.concatenate([jnp.zeros((1, n_exp), i32), cum[:, 0, :n_exp].astype(i32)])
    cb = cb_fine[::tok // sub]
    counts = cb[-1]
    nblk = (counts + rows - 1) // rows
    blk_end = jnp.cumsum(nblk)
    blk_off = blk_end - nblk
    nb = blk_end[-1]
    experts = jnp.arange(n_exp, dtype=i32)[None, :]
    row_off = (blk_off * rows)[None, :]
    slot1 = jnp.sum(jnp.where(i1[:, None] == experts, row_off, 0), axis=1) + r1
    slot2 = jnp.sum(jnp.where(i2[:, None] == experts, row_off, 0), axis=1) + r2

    b_idx = jnp.arange(nb_max, dtype=i32)
    b_used = b_idx < nb
    b_clamped = jnp.minimum(b_idx, nb - 1)
    blk_e = jnp.minimum(jnp.sum(b_clamped[:, None] >= blk_end[None, :], axis=1), n_exp - 1).astype(i32)
    prev_e = jnp.concatenate([jnp.full((1,), -1, i32), blk_e[:-1]])
    tail_rows = counts[blk_e] - (blk_end[blk_e] - 1 - blk_off[blk_e]) * rows
    half = (b_clamped == blk_end[blk_e] - 1) & (tail_rows <= rows // 2)
    blk_flag = (b_used.astype(i32) + 2 * (b_used & (blk_e != prev_e)).astype(i32)
                + 4 * (b_used & half).astype(i32))

    q = jnp.arange(nq, dtype=i32)
    e_q = jnp.minimum(jnp.sum(q[:, None] >= per * blk_end[None, :], axis=1), n_exp - 1)
    k_q = (q - per * blk_off[e_q])[:, None]
    lo = jnp.maximum(k_q * sub, cb[:-1].T[e_q])
    hi = jnp.minimum((k_q + 1) * sub, cb[1:].T[e_q])
    inter = (lo < hi) & (q < per * nb)[:, None]
    first_tb = jnp.arange(tbn, dtype=i32)[None, :] == 0
    touch = inter | (~jnp.any(inter, axis=1, keepdims=True) & first_tb)
    s_idx = jnp.arange(s_max, dtype=i32)

    def step_lists(mask, inner):
        flat = mask.ravel()
        running = jnp.cumsum(flat.astype(i32))
        n_steps = running[-1]
        s_clamped = jnp.minimum(s_idx, n_steps - 1)
        f = jnp.sum(running[None, :] <= s_clamped[:, None], axis=1).astype(i32)
        major, minor = f // inner, f % inner
        s_valid = s_idx < n_steps
        prev_m = jnp.concatenate([jnp.full((1,), -1, i32), major[:-1]])
        next_m = jnp.concatenate([major[1:], jnp.full((1,), -1, i32)])
        first = s_valid & (prev_m != major)
        last = s_valid & ((next_m != major) | (s_idx == n_steps - 1))
        return major, minor, s_valid, 2 * first.astype(i32) + 4 * last.astype(i32)

    g_q, g_tb, g_valid, g_edge = step_lists(touch, tbn)
    g_flag = (g_valid & inter[g_q, g_tb]).astype(i32) + g_edge

    c_q0 = jnp.minimum((row_off + cb_fine[:-1]) // sub, nq - 2).astype(i32).ravel()
    c_lo = (blk_off * rows).astype(i32)
    c_hi = (blk_end * rows).astype(i32)
    return dict(slot1=slot1, slot2=slot2, w1=info[:, 2], w2=info[:, 3],
                blk_e=blk_e, blk_row=b_clamped, blk_flag=blk_flag,
                g_q=g_q, g_tb=g_tb, g_flag=g_flag, c_q0=c_q0, c_lo=c_lo, c_hi=c_hi,
                nb_max=nb_max, s_max=s_max)


def _moe_gather(h, plan):
    n_tok, d = h.shape
    sub, tok = MOE_SUB, TOKEN_BLOCK
    p_rows = plan["nb_max"] * MOE_ROWS

    def body(q_ref, tb_ref, fl_ref, h_ref, s1_ref, s2_ref, w1_ref, w2_ref, hg_ref, ws_ref, acc, wacc):
        s = pl.program_id(0)
        flag = fl_ref[s]

        @pl.when((flag & 2) != 0)
        def _():
            acc[...] = jnp.zeros(acc.shape, F32)
            wacc[...] = jnp.zeros(wacc.shape, F32)

        @pl.when((flag & 1) != 0)
        def _():
            slot = q_ref[s] * sub + lax.broadcasted_iota(jnp.int32, (sub, tok), 0)
            d1 = s1_ref[...] == slot
            d2 = s2_ref[...] == slot
            sel = jnp.where(d1, 1.0, jnp.where(d2, 1.0, 0.0)).astype(BF16)
            t0 = pl.multiple_of(tb_ref[s] * tok, tok)
            acc[...] += jnp.dot(sel, h_ref[pl.ds(t0, tok), :], preferred_element_type=F32)
            wacc[...] += jnp.sum(jnp.where(d1, w1_ref[...], 0.0) + jnp.where(d2, w2_ref[...], 0.0),
                                 axis=1, keepdims=True)

        @pl.when((flag & 4) != 0)
        def _():
            hg_ref[...] = acc[...].astype(BF16)
            ws_ref[...] = wacc[...]

    tok_row = pl.BlockSpec((None, 1, tok), lambda s, q, tb, fl: (tb[s], 0, 0))
    grid_spec = pltpu.PrefetchScalarGridSpec(
        num_scalar_prefetch=3,
        grid=(plan["s_max"],),
        in_specs=[pl.BlockSpec((n_tok, d), lambda s, q, tb, fl: (0, 0), pipeline_mode=pl.Buffered(1)),
                  tok_row, tok_row, tok_row, tok_row],
        out_specs=[pl.BlockSpec((sub, d), lambda s, q, tb, fl: (q[s], 0)),
                   pl.BlockSpec((sub, 1), lambda s, q, tb, fl: (q[s], 0))],
        scratch_shapes=[pltpu.VMEM((sub, d), F32), pltpu.VMEM((sub, 1), F32)])
    rows = lambda a: a.reshape(n_tok // tok, 1, tok)
    return pl.pallas_call(
        body, grid_spec=grid_spec,
        out_shape=[jax.ShapeDtypeStruct((p_rows, d), BF16), jax.ShapeDtypeStruct((p_rows, 1), F32)],
        compiler_params=_params(48, 1), name="moe_gather",
    )(plan["g_q"], plan["g_tb"], plan["g_flag"], h,
      rows(plan["slot1"]), rows(plan["slot2"]), rows(plan["w1"]), rows(plan["w2"]))


def _grouped_mm(x, weights, lead, plan, row_inputs, out_cols, epilogue, *, tn, vmem_mib, name):
    p_rows, k = x.shape
    blk = MOE_ROWS
    half = blk // 2
    nw, nr = len(weights), len(row_inputs)

    def body(be_ref, br_ref, fl_ref, x_ref, *refs):
        w_refs, r_refs = refs[:nw], refs[nw:nw + nr]
        o_ref = refs[nw + nr]
        wb_refs = refs[nw + nr + 1:]
        flag = fl_ref[pl.program_id(1)]

        @pl.when((flag & 2) != 0)
        def _():
            for w_ref, wb in zip(w_refs, wb_refs):
                wb[...] = w_ref[...].astype(BF16)

        def compute(n_rows):
            xv = x_ref[0:n_rows, :]
            accs = [jnp.dot(xv, wb[...], preferred_element_type=F32) for wb in wb_refs]
            o_ref[0:n_rows, :] = epilogue(accs, [r[0:n_rows, :] for r in r_refs]).astype(BF16)

        @pl.when((flag & 5) == 1)
        def _():
            compute(blk)

        @pl.when((flag & 5) == 5)
        def _():
            compute(half)
            o_ref[half:blk, :] = jnp.zeros((blk - half, tn), BF16)

        @pl.when((flag & 1) == 0)
        def _():
            o_ref[...] = jnp.zeros((blk, tn), BF16)

    w_spec = pl.BlockSpec((None,) * (len(lead) + 1) + (k, tn),
                          lambda n, b, be, br, fl: (*lead, be[b], 0, n))
    grid_spec = pltpu.PrefetchScalarGridSpec(
        num_scalar_prefetch=3,
        grid=(out_cols // tn, plan["nb_max"]),
        in_specs=[pl.BlockSpec((blk, k), lambda n, b, be, br, fl: (br[b], 0))]
        + [w_spec] * nw
        + [pl.BlockSpec((blk, a.shape[1]), lambda n, b, be, br, fl: (br[b], 0)) for a in row_inputs],
        out_specs=pl.BlockSpec((blk, tn), lambda n, b, be, br, fl: (b, n)),
        scratch_shapes=[pltpu.VMEM((k, tn), BF16) for _ in weights])
    return pl.pallas_call(
        body, grid_spec=grid_spec,
        out_shape=jax.ShapeDtypeStruct((p_rows, out_cols), BF16),
        compiler_params=_params(vmem_mib, 2), name=name,
    )(plan["blk_e"], plan["blk_row"], plan["blk_flag"], x, *weights, *row_inputs)


def _moe_combine(yw, plan, x_res, gate, norm):
    n_tok, d = x_res.shape
    sub = MOE_SUB
    n_win = 2 * N_EXPERTS
    n_norm = len(norm)

    def body(q0_ref, lo_ref, hi_ref, *refs):
        y_refs = refs[:n_win]
        s1_ref, s2_ref, x_ref, g_ref = refs[n_win:n_win + 4]
        norm_refs = refs[n_win + 4:n_win + 4 + n_norm]
        o_refs = refs[n_win + 4 + n_norm:]
        t = pl.program_id(0)
        s1, s2 = s1_ref[...], s2_ref[...]
        col = lax.broadcasted_iota(jnp.int32, (sub, sub), 1)
        acc = None
        for e in range(N_EXPERTS):
            for j in range(2):
                slot = (q0_ref[t * N_EXPERTS + e] + j) * sub + col
                inside = (slot - lo_ref[e]).astype(jnp.uint32) < (hi_ref[e] - lo_ref[e]).astype(jnp.uint32)
                slot = jnp.where(inside, slot, -1)
                sel = jnp.where(s1 == slot, 1.0, jnp.where(s2 == slot, 1.0, 0.0)).astype(BF16)
                part = jnp.dot(sel, y_refs[2 * e + j][...], preferred_element_type=F32)
                acc = part if acc is None else acc + part
        x_new = x_ref[...] + g_ref[...] * acc
        if n_norm == 1:
            o_refs[0][...] = _rms(x_new, norm_refs[0][...])
        else:
            o_refs[0][...] = x_new
            o_refs[1][...] = _rms_mod(x_new, *[r[...] for r in norm_refs]).astype(BF16)

    def window_spec(e, j):
        return pl.BlockSpec((sub, d), lambda t, q0, lo, hi: (q0[t * N_EXPERTS + e] + j, 0))

    tok_col = pl.BlockSpec((sub, 1), lambda t, q0, lo, hi: (t, 0))
    tok_blk = pl.BlockSpec((sub, d), lambda t, q0, lo, hi: (t, 0))
    row = pl.BlockSpec((1, d), lambda t, q0, lo, hi: (0, 0))
    out_shape = [jax.ShapeDtypeStruct((n_tok, d), F32)]
    if n_norm > 1:
        out_shape.append(jax.ShapeDtypeStruct((n_tok, d), BF16))
    grid_spec = pltpu.PrefetchScalarGridSpec(
        num_scalar_prefetch=3,
        grid=(n_tok // sub,),
        in_specs=[window_spec(e, j) for e in range(N_EXPERTS) for j in range(2)]
        + [tok_col, tok_col, tok_blk, row] + [row] * n_norm,
        out_specs=[tok_blk] * len(out_shape))
    col_of = lambda a: a.reshape(n_tok, 1)
    return pl.pallas_call(
        body, grid_spec=grid_spec,
        out_shape=out_shape,
        compiler_params=_params(56, 1), name="moe_combine",
    )(plan["c_q0"], plan["c_lo"], plan["c_hi"], *([yw] * n_win),
      col_of(plan["slot1"]), col_of(plan["slot2"]), x_res, gate, *norm)


def _moe(h, x_res, gate, w_router, w1, w3, w2, kk, norm):
    w_r = jnp.pad(w_router[kk], ((0, 0), (0, LANES - N_EXPERTS)))
    info, cum = _router(h, w_r)
    plan = _dispatch_plan(info, cum)
    hg, w_slot = _moe_gather(h, plan)
    hid = _grouped_mm(hg, [w1, w3], (kk,), plan, [], w1.shape[-1],
                      lambda accs, _: jax.nn.silu(accs[0]) * accs[1],
                      tn=1024, vmem_mib=58, name="moe_up")
    yw = _grouped_mm(hid, [w2], (kk,), plan, [w_slot], w2.shape[-1],
                     lambda accs, rows: accs[0] * rows[0],
                     tn=1024, vmem_mib=58, name="moe_down")
    return _moe_combine(yw, plan, x_res, gate, norm)


def kernel(x, c, ada_w, ada_b, norm1_g, norm2_g, pool_w_in, pool_w_grp, pool_scale, pool_w_out, conv_w_in, conv_b_in, conv_dw_w, conv_dw_b, conv_ln_g, conv_ln_b, conv_w_out, sgu_w_in, sgu_b_in, sgu_ln_g, sgu_ln_b, sgu_w_s, sgu_b_s, sgu_w_out, sconv_w_in, sconv_w, sconv_w_out, ffn_w1, ffn_w3, ffn_w2, moe_router, moe_w1, moe_w3, moe_w2, final_g):
    batch, seq, d = x.shape
    assert batch == 1
    depth = ada_w.shape[0]
    xs = x.reshape(seq, d)
    mod = _ada_mod(c, ada_w, ada_b)

    mods = [[mod[i, :, q * d:(q + 1) * d] for q in range(6)] for i in range(depth)]
    final_g = final_g.reshape(1, d)
    h = None
    out = None
    for i in range(depth):
        sh1, sc1, g1, sh2, sc2, g2 = mods[i]

        if h is None:
            h = _norm_mod(xs, norm1_g[i:i + 1], sc1, sh1)
        mixer, j = i % 4, i // 4
        if mixer == 0:
            a = _pool_in(h, pool_w_in, pool_w_grp, pool_scale[j:j + 1], j)
            w_out = pool_w_out
        elif mixer == 1:
            z = _glu_in(h, conv_w_in, conv_b_in[j:j + 1], j)
            a = _conv_ln(z, conv_dw_w, conv_dw_b[j:j + 1], conv_ln_g[j:j + 1], conv_ln_b[j:j + 1], j)
            w_out = conv_w_out
        elif mixer == 2:
            a = _gelu_in(h, sgu_w_in, sgu_b_in[j:j + 1], j)
            a = _sgu_gate(a, sgu_ln_g[j:j + 1], sgu_ln_b[j:j + 1], sgu_w_s, sgu_b_s[j].T, j)
            w_out = sgu_w_out
        else:
            a = _sconv_in(h, sconv_w_in, sconv_w, j)
            w_out = sconv_w_out
        xs, h = _mixer_out_norm(a, w_out, j, xs, g1, norm2_g[i:i + 1], sc2, sh2)

        kk = i // 2
        if i % 2 == 0:
            hid = _swiglu_up(h, ffn_w1, ffn_w3, (kk,), tm=1024, tn=512, vmem_mib=48, name="ffn_up")
            xs = _mm_residual(hid, ffn_w2, (kk,), xs, g2, tm=512, tn=512, vmem_mib=56, name="ffn_down")
            h = None
        elif i + 1 < depth:
            sh_n, sc_n = mods[i + 1][0], mods[i + 1][1]
            xs, h = _moe(h, xs, g2, moe_router, moe_w1, moe_w3, moe_w2, kk,
                         (norm1_g[i + 1:i + 2], sc_n, sh_n))
        else:
            (out,) = _moe(h, xs, g2, moe_router, moe_w1, moe_w3, moe_w2, kk, (final_g,))
    if out is None:
        out = _final_norm(xs, final_g)
    return out.reshape(batch, seq, d)
```

```python
import functools
import math

import jax
import jax.numpy as jnp
from jax import lax
from jax.experimental import pallas as pl
from jax.experimental.pallas import tpu as pltpu

EPS = 1e-6
POOL_WINDOWS = (2, 4, 8, 16)
CONV_WIDTH = 31
SGU_CHUNK = 128
SGU_HEADS = 8
SHORT_CONV_WIDTH = 3
N_EXPERTS = 8
MOE_ROWS = 512
MOE_SUB = 256
TOKEN_BLOCK = 512
LANES = 128
SUBLANES = 8
MIB = 1024 * 1024

BF16 = jnp.bfloat16
F32 = jnp.float32


def _params(vmem_mib, ndims):
    return pltpu.CompilerParams(
        dimension_semantics=("arbitrary",) * ndims,
        vmem_limit_bytes=vmem_mib * MIB)


def _ada_chunk(c_ref, w_ref, b_ref):
    c_act = jax.nn.silu(c_ref[...])
    return jnp.sum(w_ref[...] * c_act, axis=0, keepdims=True) + b_ref[...]


def _ada_mod(c_col, ada_w, ada_b3, layer):
    depth, d, n = ada_w.shape
    tn = 1024

    def body(c_ref, w_ref, b_ref, o_ref):
        o_ref[...] = _ada_chunk(c_ref, w_ref, b_ref)

    return pl.pallas_call(
        body,
        grid=(n // tn,),
        in_specs=[
            pl.BlockSpec((d, 1), lambda j: (0, 0)),
            pl.BlockSpec((None, d, tn), lambda j: (layer, 0, j)),
            pl.BlockSpec((None, 1, tn), lambda j: (layer, 0, j)),
        ],
        out_specs=pl.BlockSpec((1, tn), lambda j: (0, j)),
        out_shape=jax.ShapeDtypeStruct((1, n), F32),
        compiler_params=_params(40, 1),
        name="ada_mod",
    )(c_col, ada_w, ada_b3)


class _AdaSide:
    def __init__(self, c_col, ada_w, ada_b3, layer, outer, inner):
        self.arrays = (c_col, ada_w, ada_b3)
        self.layer, self.inner = layer, inner
        _, self.d, self.n = ada_w.shape
        self.chunk = next(ch for ch in range(LANES, self.n + 1, LANES)
                          if self.n % ch == 0 and self.n // ch <= outer * inner)
        self.n_chunks = self.n // self.chunk
        self.out_shape = jax.ShapeDtypeStruct((1, self.n), F32)

    def _chunk(self, n, m):
        return jnp.minimum(n * self.inner + m, self.n_chunks - 1)

    def specs(self):
        d, layer, chunk = self.d, self.layer, self.chunk
        ins = [pl.BlockSpec((d, 1), lambda n, m, *_: (0, 0)),
               pl.BlockSpec((None, d, chunk), lambda n, m, *_: (layer, 0, self._chunk(n, m))),
               pl.BlockSpec((None, 1, chunk), lambda n, m, *_: (layer, 0, self._chunk(n, m)))]
        return ins, pl.BlockSpec((1, chunk), lambda n, m, *_: (0, self._chunk(n, m)))

    def run(self, c_ref, w_ref, b_ref, o_ref, every_step):
        if every_step:
            o_ref[...] = _ada_chunk(c_ref, w_ref, b_ref)
            return

        @pl.when(pl.program_id(0) * self.inner + pl.program_id(1) < self.n_chunks)
        def _():
            o_ref[...] = _ada_chunk(c_ref, w_ref, b_ref)


def _rms(x, g):
    return (x * lax.rsqrt(jnp.mean(x * x, axis=-1, keepdims=True) + EPS)) * g


def _rms_mod(x, g, sc, sh):
    return _rms(x, g) * (1.0 + sc) + sh


def _norm_mod_body(x_ref, g_ref, sc_ref, sh_ref, o_ref):
    o_ref[...] = _rms_mod(x_ref[...], g_ref[...], sc_ref[...], sh_ref[...]).astype(o_ref.dtype)


def _norm_mod(x, g, sc, sh):
    m, d = x.shape
    tm = 512
    row = pl.BlockSpec((1, d), lambda i: (0, 0))
    return pl.pallas_call(
        _norm_mod_body,
        grid=(m // tm,),
        in_specs=[pl.BlockSpec((tm, d), lambda i: (i, 0)), row, row, row],
        out_specs=pl.BlockSpec((tm, d), lambda i: (i, 0)),
        out_shape=jax.ShapeDtypeStruct((m, d), BF16),
        compiler_params=_params(40, 1),
        name="norm_mod",
    )(x, g, sc, sh)


def _final_norm_body(x_ref, g_ref, o_ref):
    o_ref[...] = _rms(x_ref[...], g_ref[...])


def _final_norm(x, g):
    m, d = x.shape
    tm = 512
    return pl.pallas_call(
        _final_norm_body,
        grid=(m // tm,),
        in_specs=[pl.BlockSpec((tm, d), lambda i: (i, 0)), pl.BlockSpec((1, d), lambda i: (0, 0))],
        out_specs=pl.BlockSpec((tm, d), lambda i: (i, 0)),
        out_shape=jax.ShapeDtypeStruct((m, d), F32),
        compiler_params=_params(40, 1),
        name="final_norm",
    )(x, g)


def _w_spec(w, lead, k, tn, off):
    assert w.shape[len(lead)] == k
    return pl.BlockSpec((None,) * len(lead) + (k, tn), lambda n, m: (*lead, 0, n + off))


def _row_spec(a, lead, tn, off=0):
    return pl.BlockSpec((None,) * len(lead) + (1, tn), lambda n, m: (*lead, 0, n + off))


def _colmm(x, weights, raw_inputs, outs, epilogue, *, tm, tn, nt, scratch=(), vmem_mib, name,
           make_side=None):
    m_rows, k = x.shape
    nw, nr, no = len(weights), len(raw_inputs), len(outs)
    side = make_side(nt, m_rows // tm) if make_side else None
    side_in, side_out = side.specs() if side else ([], None)
    ns = len(side_in)

    def body(*refs):
        x_ref = refs[0]
        w_refs = refs[1:1 + nw]
        r_refs = refs[1 + nw:1 + nw + nr]
        side_refs = refs[1 + nw + nr:1 + nw + nr + ns]
        o_refs = refs[1 + nw + nr + ns:1 + nw + nr + ns + no]
        rest = refs[1 + nw + nr + ns + no:]
        side_o, rest = (rest[0], rest[1:]) if side else (None, rest)
        wb_refs, extra = rest[:nw], rest[nw:]

        @pl.when(pl.program_id(1) == 0)
        def _():
            for w_ref, wb in zip(w_refs, wb_refs):
                wb[...] = w_ref[...].astype(BF16)

        xv = x_ref[...]
        accs = [jnp.dot(xv, wb[...], preferred_element_type=F32) for wb in wb_refs]
        epilogue(accs, r_refs, o_refs, extra)
        if side:
            side.run(*side_refs, side_o, every_step=True)

    in_specs = [pl.BlockSpec((tm, k), lambda n, m: (m, 0))]
    in_specs += [_w_spec(w, lead, k, tn, off) for (w, lead, off) in weights]
    in_specs += [spec for (_, spec) in raw_inputs]
    return pl.pallas_call(
        body,
        grid=(nt, m_rows // tm),
        in_specs=in_specs + side_in,
        out_specs=[spec for (_, spec) in outs] + ([side_out] if side else []),
        out_shape=[sds for (sds, _) in outs] + ([side.out_shape] if side else []),
        scratch_shapes=[pltpu.VMEM((k, tn), BF16) for _ in weights] + list(scratch),
        compiler_params=_params(vmem_mib, 2),
        name=name,
    )(x, *[w for (w, _, _) in weights], *[a for (a, _) in raw_inputs], *(side.arrays if side else ()))


def _tile_spec(tm, tn):
    return pl.BlockSpec((tm, tn), lambda n, m: (m, n))


def _mm_residual(a, w, lead, x_res, gate, *, tm, tn, vmem_mib, name):
    m_rows = a.shape[0]
    n_cols = x_res.shape[1]
    raw = [(gate, _row_spec(gate, (), tn)), (x_res, _tile_spec(tm, tn))]

    def epilogue(accs, r_refs, o_refs, _):
        o_refs[0][...] = r_refs[1][...] + accs[0] * r_refs[0][...]

    out = _colmm(a, [(w, lead, 0)], raw,
                 [(jax.ShapeDtypeStruct((m_rows, n_cols), F32), _tile_spec(tm, tn))],
                 epilogue, tm=tm, tn=tn, nt=n_cols // tn, vmem_mib=vmem_mib, name=name)
    return out[0]


def _mixer_out_norm(a, w, j, x_res, gate, g, sc, sh):
    m_rows, k = a.shape
    d = x_res.shape[1]
    tm = 256

    def body(a_ref, w_ref, gate_ref, x_ref, g_ref, sc_ref, sh_ref, xo_ref, ho_ref, wb):
        @pl.when(pl.program_id(0) == 0)
        def _():
            wb[...] = w_ref[...].astype(BF16)

        y = jnp.dot(a_ref[...], wb[...], preferred_element_type=F32)
        x_new = x_ref[...] + gate_ref[...] * y
        xo_ref[...] = x_new
        ho_ref[...] = _rms_mod(x_new, g_ref[...], sc_ref[...], sh_ref[...]).astype(BF16)

    row = pl.BlockSpec((1, d), lambda i: (0, 0))
    blk_in = pl.BlockSpec((tm, k), lambda i: (i, 0))
    blk_d = pl.BlockSpec((tm, d), lambda i: (i, 0))
    return pl.pallas_call(
        body,
        grid=(m_rows // tm,),
        in_specs=[blk_in,
                  pl.BlockSpec((None, k, d), lambda i: (j, 0, 0), pipeline_mode=pl.Buffered(1)),
                  row, blk_d, row, row, row],
        out_specs=[blk_d, blk_d],
        out_shape=[jax.ShapeDtypeStruct((m_rows, d), F32), jax.ShapeDtypeStruct((m_rows, d), BF16)],
        scratch_shapes=[pltpu.VMEM((k, d), BF16)],
        compiler_params=_params(48, 1),
        name="mixer_out",
    )(a, w, gate, x_res, g, sc, sh)


def _swiglu_up(h, w1, w3, lead, *, tm, tn, vmem_mib, name, make_side=None):
    m_rows = h.shape[0]
    n_cols = w1.shape[-1]

    def epilogue(accs, r_refs, o_refs, _):
        o_refs[0][...] = (jax.nn.silu(accs[0]) * accs[1]).astype(BF16)

    return _colmm(h, [(w1, lead, 0), (w3, lead, 0)], [],
                  [(jax.ShapeDtypeStruct((m_rows, n_cols), BF16), _tile_spec(tm, tn))],
                  epilogue, tm=tm, tn=tn, nt=n_cols // tn, vmem_mib=vmem_mib, name=name,
                  make_side=make_side)


def _pool_in(h, w_in, w_grp, scale, j):
    m_rows, k = h.shape
    n_groups = len(POOL_WINDOWS)
    tn = w_in.shape[-1] // n_groups
    tm = 1024
    halo = max(POOL_WINDOWS)

    def epilogue(accs, r_refs, o_refs, s_refs):
        z = accs[0]
        wg_ref, scale_ref = r_refs
        zext, wgb, pooled = s_refs
        n, m = pl.program_id(0), pl.program_id(1)

        @pl.when(m == 0)
        def _():
            zext[0:halo, :] = jnp.zeros((halo, tn), F32)
            wgb[...] = wg_ref[...].astype(BF16)

        zext[halo:halo + tm, :] = z
        t = m * tm + lax.broadcasted_iota(jnp.int32, (tm, tn), 0)
        for gi, win in enumerate(POOL_WINDOWS):
            @pl.when(n == gi)
            def _(win=win):
                wsum = z
                for back in range(1, win):
                    wsum = wsum + zext[halo - back:halo - back + tm, :]
                count = jnp.minimum(t + 1, win).astype(F32)
                pooled[...] = (wsum / count - z).astype(BF16)
        zext[0:halo, :] = zext[tm:tm + halo, :]
        mixed = jnp.dot(pooled[...], wgb[...], preferred_element_type=F32)
        o_refs[0][...] = (mixed * scale_ref[...]).astype(BF16)

    raw = [(w_grp, pl.BlockSpec((None, None, tn, tn), lambda n, m: (j, n, 0, 0))),
           (scale, _row_spec(scale, (), tn))]
    out = _colmm(h, [(w_in, (j,), 0)], raw,
                 [(jax.ShapeDtypeStruct((m_rows, n_groups * tn), BF16), _tile_spec(tm, tn))],
                 epilogue, tm=tm, tn=tn, nt=n_groups,
                 scratch=[pltpu.VMEM((tm + halo, tn), F32), pltpu.VMEM((tn, tn), BF16),
                          pltpu.VMEM((tm, tn), BF16)],
                 vmem_mib=48, name="pool_in")
    return out[0]


def _glu_in(h, w_in, b_in, j):
    m_rows, k = h.shape
    dm = w_in.shape[-1] // 2
    tm, tn = 1024, 512
    nt = dm // tn

    def epilogue(accs, r_refs, o_refs, _):
        a = accs[0] + r_refs[0][...]
        g = accs[1] + r_refs[1][...]
        o_refs[0][...] = (a * jax.nn.sigmoid(g)).astype(BF16)

    raw = [(b_in, _row_spec(b_in, (), tn)), (b_in, _row_spec(b_in, (), tn, off=nt))]
    out = _colmm(h, [(w_in, (j,), 0), (w_in, (j,), nt)], raw,
                 [(jax.ShapeDtypeStruct((m_rows, dm), BF16), _tile_spec(tm, tn))],
                 epilogue, tm=tm, tn=tn, nt=nt, vmem_mib=48, name="glu_in")
    return out[0]


def _conv_ln_body(z_ref, w_ref, cb_ref, g_ref, b_ref, o_ref, zsh, wbc, conv, *, tm, halo, rows, cols):
    d = z_ref.shape[1]
    sl = SUBLANES

    @pl.when(pl.program_id(0) == 0)
    def _():
        zsh[0, 0:halo, :] = jnp.zeros((halo, d), F32)
        for kk in range(CONV_WIDTH):
            wbc[kk] = jnp.broadcast_to(w_ref[kk:kk + 1, :], (sl, d))

    zsh[0, halo:halo + tm, :] = z_ref[...].astype(F32)
    for b in range(1, sl):
        zsh[b, sl:halo + tm, :] = zsh[0, sl - b:halo + tm - b, :]

    for c0 in range(0, d, cols):
        cs = slice(c0, c0 + cols)
        for r0 in range(0, tm, rows):
            acc = jnp.zeros((rows, cols), F32)
            for kk in range(CONV_WIDTH):
                a, b = divmod(CONV_WIDTH - 1 - kk, sl)
                start = halo + r0 - sl * a
                wv = jnp.tile(wbc[kk, :, cs], (rows // sl, 1))
                acc = acc + wv * zsh[b, start:start + rows, cs]
            conv[r0:r0 + rows, cs] = acc + cb_ref[:, cs]
    zsh[0, 0:halo, :] = zsh[0, tm:tm + halo, :]

    y = conv[...]
    mu = jnp.mean(y, axis=-1, keepdims=True)
    var = jnp.mean(jnp.square(y - mu), axis=-1, keepdims=True)
    yn = (y - mu) * lax.rsqrt(var + EPS) * g_ref[...] + b_ref[...]
    o_ref[...] = jax.nn.silu(yn).astype(o_ref.dtype)


def _conv_ln(z, dw_w, dw_b, ln_g, ln_b, j):
    m_rows, d = z.shape
    tm, halo = 256, 32
    row = pl.BlockSpec((1, d), lambda i: (0, 0))
    return pl.pallas_call(
        functools.partial(_conv_ln_body, tm=tm, halo=halo, rows=64, cols=512),
        grid=(m_rows // tm,),
        in_specs=[pl.BlockSpec((tm, d), lambda i: (i, 0)),
                  pl.BlockSpec((None, CONV_WIDTH, d), lambda i: (j, 0, 0)), row, row, row],
        out_specs=pl.BlockSpec((tm, d), lambda i: (i, 0)),
        out_shape=jax.ShapeDtypeStruct((m_rows, d), BF16),
        scratch_shapes=[pltpu.VMEM((SUBLANES, tm + halo, d), F32),
                        pltpu.VMEM((CONV_WIDTH, SUBLANES, d), F32), pltpu.VMEM((tm, d), F32)],
        compiler_params=_params(40, 1),
        name="conv_ln",
    )(z, dw_w, dw_b, ln_g, ln_b)


def _gelu_in(h, w_in, b_in, j):
    m_rows, k = h.shape
    n_cols = w_in.shape[-1]
    tm, tn = 512, 1024

    def epilogue(accs, r_refs, o_refs, _):
        a = accs[0] + r_refs[0][...]
        o_refs[0][...] = (0.5 * a * (1.0 + lax.erf(a * math.sqrt(0.5)))).astype(BF16)

    out = _colmm(h, [(w_in, (j,), 0)], [(b_in, _row_spec(b_in, (), tn))],
                 [(jax.ShapeDtypeStruct((m_rows, n_cols), BF16), _tile_spec(tm, tn))],
                 epilogue, tm=tm, tn=tn, nt=n_cols // tn, vmem_mib=48, name="gelu_in")
    return out[0]


def _sgu_gate_body(u_ref, v_ref, g_ref, b_ref, ws_ref, bst_ref, o_ref, *, tm):
    v = v_ref[...].astype(F32)
    mu = jnp.mean(v, axis=-1, keepdims=True)
    var = jnp.mean(jnp.square(v - mu), axis=-1, keepdims=True)
    vn = ((v - mu) * lax.rsqrt(var + EPS) * g_ref[...] + b_ref[...]).astype(BF16)
    hd = v.shape[1] // SGU_HEADS
    tri = (lax.broadcasted_iota(jnp.int32, (SGU_CHUNK, SGU_CHUNK), 0)
           >= lax.broadcasted_iota(jnp.int32, (SGU_CHUNK, SGU_CHUNK), 1))
    for hh in range(SGU_HEADS):
        wc = jnp.where(tri, ws_ref[hh], 0.0).astype(BF16)
        bias = bst_ref[:, hh:hh + 1]
        for ck in range(tm // SGU_CHUNK):
            rs = slice(ck * SGU_CHUNK, (ck + 1) * SGU_CHUNK)
            cs = slice(hh * hd, (hh + 1) * hd)
            sv = jnp.dot(wc, vn[rs, cs], preferred_element_type=F32) + bias
            o_ref[rs, cs] = (u_ref[rs, cs].astype(F32) * sv).astype(o_ref.dtype)


def _sgu_gate(a, ln_g, ln_b, w_s, b_s_t, j):
    m_rows = a.shape[0]
    d = a.shape[1] // 2
    tm = 256
    row = pl.BlockSpec((1, d), lambda i: (0, 0))
    return pl.pallas_call(
        functools.partial(_sgu_gate_body, tm=tm),
        grid=(m_rows // tm,),
        in_specs=[pl.BlockSpec((tm, d), lambda i: (i, 0)), pl.BlockSpec((tm, d), lambda i: (i, 1)),
                  row, row,
                  pl.BlockSpec((None, SGU_HEADS, SGU_CHUNK, SGU_CHUNK), lambda i: (j, 0, 0, 0)),
                  pl.BlockSpec((SGU_CHUNK, SGU_HEADS), lambda i: (0, 0))],
        out_specs=pl.BlockSpec((tm, d), lambda i: (i, 0)),
        out_shape=jax.ShapeDtypeStruct((m_rows, d), BF16),
        compiler_params=_params(40, 1),
        name="sgu_gate",
    )(a, a, ln_g, ln_b, w_s, b_s_t)


def _sconv_in(h, w_in, conv_w, j):
    m_rows, k = h.shape
    dm = w_in.shape[-1] // 3
    tm, tn = 1024, 256
    nt = dm // tn
    halo = SUBLANES

    def epilogue(accs, r_refs, o_refs, s_refs):
        bg, cg, z = accs
        cw_ref = r_refs[0]
        ext = s_refs[0]

        @pl.when(pl.program_id(1) == 0)
        def _():
            ext[0:halo, :] = jnp.zeros((halo, tn), F32)

        cz = cg * z
        ext[halo:halo + tm, :] = cz
        conv = cw_ref[SHORT_CONV_WIDTH - 1:SHORT_CONV_WIDTH, :] * cz
        for back in range(1, SHORT_CONV_WIDTH):
            tap = SHORT_CONV_WIDTH - 1 - back
            conv = conv + cw_ref[tap:tap + 1, :] * ext[halo - back:halo - back + tm, :]
        ext[0:halo, :] = ext[tm:tm + halo, :]
        o_refs[0][...] = (bg * conv).astype(BF16)

    raw = [(conv_w, pl.BlockSpec((None, SHORT_CONV_WIDTH, tn), lambda n, m: (j, 0, n)))]
    out = _colmm(h, [(w_in, (j,), 0), (w_in, (j,), nt), (w_in, (j,), 2 * nt)], raw,
                 [(jax.ShapeDtypeStruct((m_rows, dm), BF16), _tile_spec(tm, tn))],
                 epilogue, tm=tm, tn=tn, nt=nt,
                 scratch=[pltpu.VMEM((tm + halo, tn), F32)], vmem_mib=48, name="sconv_in")
    return out[0]


def _router_body(h_ref, w_ref, info_ref, cum_ref, carry):
    logits = jnp.dot(h_ref[...], w_ref[...].astype(BF16), preferred_element_type=F32)
    lane = lax.broadcasted_iota(jnp.int32, logits.shape, 1)
    neg = jnp.float32(-jnp.inf)
    lg = jnp.where(lane < N_EXPERTS, logits, neg)
    v1 = jnp.max(lg, axis=-1, keepdims=True)
    lane_f = lane.astype(F32)
    i1 = jnp.min(jnp.where(lg == v1, lane_f, float(LANES)), axis=-1, keepdims=True)
    lg2 = jnp.where(lane_f == i1, neg, lg)
    v2 = jnp.max(lg2, axis=-1, keepdims=True)
    i2 = jnp.min(jnp.where(lg2 == v2, lane_f, float(LANES)), axis=-1, keepdims=True)
    e2 = jnp.exp(v2 - v1)
    den = 1.0 + e2
    w1, w2 = 1.0 / den, e2 / den

    @pl.when(pl.program_id(0) == 0)
    def _():
        carry[...] = jnp.zeros(carry.shape, F32)

    tm = logits.shape[0]
    cnt = jnp.where(lane_f == i1, 1.0, 0.0) + jnp.where(lane_f == i2, 1.0, 0.0)
    strict = jnp.where(lax.broadcasted_iota(jnp.int32, (tm, tm), 1)
                       < lax.broadcasted_iota(jnp.int32, (tm, tm), 0), 1.0, 0.0).astype(BF16)
    before = jnp.dot(strict, cnt.astype(BF16), preferred_element_type=F32) + carry[...]
    r1 = jnp.sum(jnp.where(lane_f == i1, before, 0.0), axis=-1, keepdims=True)
    r2 = jnp.sum(jnp.where(lane_f == i2, before, 0.0), axis=-1, keepdims=True)
    fields = (i1, i2, w1, w2, r1, r2)
    info = jnp.zeros(logits.shape, F32)
    for q, val in enumerate(fields):
        info = jnp.where(lane == q, val, info)
    info_ref[...] = info[:, :info_ref.shape[1]]
    total = carry[...] + jnp.sum(cnt, axis=0, keepdims=True)
    carry[...] = total
    cum_ref[...] = total


def _router(h, w_router_padded):
    m_rows, k = h.shape
    tm = MOE_SUB
    return pl.pallas_call(
        _router_body,
        grid=(m_rows // tm,),
        in_specs=[pl.BlockSpec((tm, k), lambda i: (i, 0)), pl.BlockSpec((k, LANES), lambda i: (0, 0))],
        out_specs=[pl.BlockSpec((tm, SUBLANES), lambda i: (i, 0)),
                   pl.BlockSpec((None, 1, LANES), lambda i: (i, 0, 0))],
        out_shape=[jax.ShapeDtypeStruct((m_rows, SUBLANES), F32),
                   jax.ShapeDtypeStruct((m_rows // tm, 1, LANES), F32)],
        scratch_shapes=[pltpu.VMEM((1, LANES), F32)],
        compiler_params=_params(40, 1),
        name="router",
    )(h, w_router_padded)


def _dispatch_plan(info, cum):
    rows, sub, tok, n_exp = MOE_ROWS, MOE_SUB, TOKEN_BLOCK, N_EXPERTS
    per = rows // sub
    n_tok = info.shape[0]
    tbn = n_tok // tok
    nb_max = (2 * n_tok) // rows + n_exp
    nq = per * nb_max
    s_max = nq + n_exp * (tbn - 1)
    i32 = jnp.int32
    ids = info.astype(i32)
    i1, i2, r1, r2 = ids[:, 0], ids[:, 1], ids[:, 4], ids[:, 5]
    cb_fine = jnp.concatenate([jnp.zeros((1, n_exp), i32), cum[:, 0, :n_exp].astype(i32)])
    cb = cb_fine[::tok // sub]
    counts = cb[-1]
    nblk = (counts + rows - 1) // rows
    blk_end = jnp.cumsum(nblk)
    blk_off = blk_end - nblk
    nb = blk_end[-1]
    experts = jnp.arange(n_exp, dtype=i32)[None, :]
    row_off = (blk_off * rows)[None, :]
    slot1 = jnp.sum(jnp.where(i1[:, None] == experts, row_off, 0), axis=1) + r1
    slot2 = jnp.sum(jnp.where(i2[:, None] == experts, row_off, 0), axis=1) + r2

    b_idx = jnp.arange(nb_max, dtype=i32)
    b_used = b_idx < nb
    b_clamped = jnp.minimum(b_idx, nb - 1)
    blk_e = jnp.minimum(jnp.sum(b_clamped[:, None] >= blk_end[None, :], axis=1), n_exp - 1).astype(i32)
    prev_e = jnp.concatenate([jnp.full((1,), -1, i32), blk_e[:-1]])
    tail_rows = counts[blk_e] - (blk_end[blk_e] - 1 - blk_off[blk_e]) * rows
    half = (b_clamped == blk_end[blk_e] - 1) & (tail_rows <= rows // 2)
    blk_flag = (b_used.astype(i32) + 2 * (b_used & (blk_e != prev_e)).astype(i32)
                + 4 * (b_used & half).astype(i32))

    q = jnp.arange(nq, dtype=i32)
    e_q = jnp.minimum(jnp.sum(q[:, None] >= per * blk_end[None, :], axis=1), n_exp - 1)
    k_q = (q - per * blk_off[e_q])[:, None]
    lo = jnp.maximum(k_q * sub, cb[:-1].T[e_q])
    hi = jnp.minimum((k_q + 1) * sub, cb[1:].T[e_q])
    inter = (lo < hi) & (q < per * nb)[:, None]
    first_tb = jnp.arange(tbn, dtype=i32)[None, :] == 0
    touch = inter | (~jnp.any(inter, axis=1, keepdims=True) & first_tb)
    s_idx = jnp.arange(s_max, dtype=i32)

    def step_lists(mask, inner):
        flat = mask.ravel()
        running = jnp.cumsum(flat.astype(i32))
        n_steps = running[-1]
        s_clamped = jnp.minimum(s_idx, n_steps - 1)
        f = jnp.sum(running[None, :] <= s_clamped[:, None], axis=1).astype(i32)
        major, minor = f // inner, f % inner
        s_valid = s_idx < n_steps
        prev_m = jnp.concatenate([jnp.full((1,), -1, i32), major[:-1]])
        next_m = jnp.concatenate([major[1:], jnp.full((1,), -1, i32)])
        first = s_valid & (prev_m != major)
        last = s_valid & ((next_m != major) | (s_idx == n_steps - 1))
        return major, minor, s_valid, 2 * first.astype(i32) + 4 * last.astype(i32)

    g_q, g_tb, g_valid, g_edge = step_lists(touch, tbn)
    g_flag = (g_valid & inter[g_q, g_tb]).astype(i32) + g_edge

    c_q0 = jnp.minimum((row_off + cb_fine[:-1]) // sub, nq - 2).astype(i32).ravel()
    c_lo = (blk_off * rows).astype(i32)
    c_hi = (blk_end * rows).astype(i32)
    return dict(slot1=slot1, slot2=slot2, w1=info[:, 2], w2=info[:, 3],
                blk_e=blk_e, blk_row=b_clamped, blk_flag=blk_flag,
                g_q=g_q, g_tb=g_tb, g_flag=g_flag, c_q0=c_q0, c_lo=c_lo, c_hi=c_hi,
                nb_max=nb_max, s_max=s_max)


def _moe_gather(h, plan):
    n_tok, d = h.shape
    sub, tok = MOE_SUB, TOKEN_BLOCK
    p_rows = plan["nb_max"] * MOE_ROWS

    def body(q_ref, tb_ref, fl_ref, h_ref, s1_ref, s2_ref, w1_ref, w2_ref, hg_ref, ws_ref, acc, wacc):
        s = pl.program_id(0)
        flag = fl_ref[s]

        @pl.when((flag & 2) != 0)
        def _():
            acc[...] = jnp.zeros(acc.shape, F32)
            wacc[...] = jnp.zeros(wacc.shape, F32)

        @pl.when((flag & 1) != 0)
        def _():
            slot = q_ref[s] * sub + lax.broadcasted_iota(jnp.int32, (sub, tok), 0)
            d1 = s1_ref[...] == slot
            d2 = s2_ref[...] == slot
            sel = jnp.where(d1, 1.0, jnp.where(d2, 1.0, 0.0)).astype(BF16)
            t0 = pl.multiple_of(tb_ref[s] * tok, tok)
            acc[...] += jnp.dot(sel, h_ref[pl.ds(t0, tok), :], preferred_element_type=F32)
            wacc[...] += jnp.sum(jnp.where(d1, w1_ref[...], 0.0) + jnp.where(d2, w2_ref[...], 0.0),
                                 axis=1, keepdims=True)

        @pl.when((flag & 4) != 0)
        def _():
            hg_ref[...] = acc[...].astype(BF16)
            ws_ref[...] = wacc[...]

    tok_row = pl.BlockSpec((None, 1, tok), lambda s, q, tb, fl: (tb[s], 0, 0))
    grid_spec = pltpu.PrefetchScalarGridSpec(
        num_scalar_prefetch=3,
        grid=(plan["s_max"],),
        in_specs=[pl.BlockSpec((n_tok, d), lambda s, q, tb, fl: (0, 0), pipeline_mode=pl.Buffered(1)),
                  tok_row, tok_row, tok_row, tok_row],
        out_specs=[pl.BlockSpec((sub, d), lambda s, q, tb, fl: (q[s], 0)),
                   pl.BlockSpec((sub, 1), lambda s, q, tb, fl: (q[s], 0))],
        scratch_shapes=[pltpu.VMEM((sub, d), F32), pltpu.VMEM((sub, 1), F32)])
    rows = lambda a: a.reshape(n_tok // tok, 1, tok)
    return pl.pallas_call(
        body, grid_spec=grid_spec,
        out_shape=[jax.ShapeDtypeStruct((p_rows, d), BF16), jax.ShapeDtypeStruct((p_rows, 1), F32)],
        compiler_params=_params(48, 1), name="moe_gather",
    )(plan["g_q"], plan["g_tb"], plan["g_flag"], h,
      rows(plan["slot1"]), rows(plan["slot2"]), rows(plan["w1"]), rows(plan["w2"]))


def _grouped_mm(x, weights, lead, plan, row_inputs, out_cols, epilogue, *, tn, vmem_mib, name,
                make_side=None):
    p_rows, k = x.shape
    blk = MOE_ROWS
    half = blk // 2
    nw, nr = len(weights), len(row_inputs)
    side = make_side(out_cols // tn, plan["nb_max"]) if make_side else None
    side_in, side_out = side.specs() if side else ([], None)
    ns = len(side_in)

    def body(be_ref, br_ref, fl_ref, x_ref, *refs):
        w_refs, r_refs = refs[:nw], refs[nw:nw + nr]
        side_refs = refs[nw + nr:nw + nr + ns]
        o_ref = refs[nw + nr + ns]
        rest = refs[nw + nr + ns + 1:]
        if side:
            side.run(*side_refs, rest[0], every_step=False)
            rest = rest[1:]
        wb_refs = rest
        flag = fl_ref[pl.program_id(1)]

        @pl.when((flag & 2) != 0)
        def _():
            for w_ref, wb in zip(w_refs, wb_refs):
                wb[...] = w_ref[...].astype(BF16)

        def compute(n_rows):
            xv = x_ref[0:n_rows, :]
            accs = [jnp.dot(xv, wb[...], preferred_element_type=F32) for wb in wb_refs]
            o_ref[0:n_rows, :] = epilogue(accs, [r[0:n_rows, :] for r in r_refs]).astype(BF16)

        @pl.when((flag & 5) == 1)
        def _():
            compute(blk)

        @pl.when((flag & 5) == 5)
        def _():
            compute(half)
            o_ref[half:blk, :] = jnp.zeros((blk - half, tn), BF16)

        @pl.when((flag & 1) == 0)
        def _():
            o_ref[...] = jnp.zeros((blk, tn), BF16)

    w_spec = pl.BlockSpec((None,) * (len(lead) + 1) + (k, tn),
                          lambda n, b, be, br, fl: (*lead, be[b], 0, n))
    grid_spec = pltpu.PrefetchScalarGridSpec(
        num_scalar_prefetch=3,
        grid=(out_cols // tn, plan["nb_max"]),
        in_specs=[pl.BlockSpec((blk, k), lambda n, b, be, br, fl: (br[b], 0))]
        + [w_spec] * nw
        + [pl.BlockSpec((blk, a.shape[1]), lambda n, b, be, br, fl: (br[b], 0)) for a in row_inputs]
        + side_in,
        out_specs=[pl.BlockSpec((blk, tn), lambda n, b, be, br, fl: (b, n))] + ([side_out] if side else []),
        scratch_shapes=[pltpu.VMEM((k, tn), BF16) for _ in weights])
    return pl.pallas_call(
        body, grid_spec=grid_spec,
        out_shape=[jax.ShapeDtypeStruct((p_rows, out_cols), BF16)] + ([side.out_shape] if side else []),
        compiler_params=_params(vmem_mib, 2), name=name,
    )(plan["blk_e"], plan["blk_row"], plan["blk_flag"], x, *weights, *row_inputs,
      *(side.arrays if side else ()))


def _moe_combine(yw, plan, x_res, gate, norm):
    n_tok, d = x_res.shape
    sub = MOE_SUB
    n_win = 2 * N_EXPERTS
    n_norm = len(norm)

    def body(q0_ref, lo_ref, hi_ref, *refs):
        y_refs = refs[:n_win]
        s1_ref, s2_ref, x_ref, g_ref = refs[n_win:n_win + 4]
        norm_refs = refs[n_win + 4:n_win + 4 + n_norm]
        o_refs = refs[n_win + 4 + n_norm:]
        t = pl.program_id(0)
        s1, s2 = s1_ref[...], s2_ref[...]
        col = lax.broadcasted_iota(jnp.int32, (sub, sub), 1)
        acc = None
        for e in range(N_EXPERTS):
            for j in range(2):
                slot = (q0_ref[t * N_EXPERTS + e] + j) * sub + col
                inside = (slot - lo_ref[e]).astype(jnp.uint32) < (hi_ref[e] - lo_ref[e]).astype(jnp.uint32)
                slot = jnp.where(inside, slot, -1)
                sel = jnp.where(s1 == slot, 1.0, jnp.where(s2 == slot, 1.0, 0.0)).astype(BF16)
                part = jnp.dot(sel, y_refs[2 * e + j][...], preferred_element_type=F32)
                acc = part if acc is None else acc + part
        x_new = x_ref[...] + g_ref[...] * acc
        if n_norm == 1:
            o_refs[0][...] = _rms(x_new, norm_refs[0][...])
        else:
            o_refs[0][...] = x_new
            o_refs[1][...] = _rms_mod(x_new, *[r[...] for r in norm_refs]).astype(BF16)

    def window_spec(e, j):
        return pl.BlockSpec((sub, d), lambda t, q0, lo, hi: (q0[t * N_EXPERTS + e] + j, 0))

    tok_col = pl.BlockSpec((sub, 1), lambda t, q0, lo, hi: (t, 0))
    tok_blk = pl.BlockSpec((sub, d), lambda t, q0, lo, hi: (t, 0))
    row = pl.BlockSpec((1, d), lambda t, q0, lo, hi: (0, 0))
    out_shape = [jax.ShapeDtypeStruct((n_tok, d), F32)]
    if n_norm > 1:
        out_shape.append(jax.ShapeDtypeStruct((n_tok, d), BF16))
    grid_spec = pltpu.PrefetchScalarGridSpec(
        num_scalar_prefetch=3,
        grid=(n_tok // sub,),
        in_specs=[window_spec(e, j) for e in range(N_EXPERTS) for j in range(2)]
        + [tok_col, tok_col, tok_blk, row] + [row] * n_norm,
        out_specs=[tok_blk] * len(out_shape))
    col_of = lambda a: a.reshape(n_tok, 1)
    return pl.pallas_call(
        body, grid_spec=grid_spec,
        out_shape=out_shape,
        compiler_params=_params(56, 1), name="moe_combine",
    )(plan["c_q0"], plan["c_lo"], plan["c_hi"], *([yw] * n_win),
      col_of(plan["slot1"]), col_of(plan["slot2"]), x_res, gate, *norm)


def _moe(h, x_res, gate, w_router, w1, w3, w2, kk, make_side, make_norm):
    w_r = jnp.pad(w_router[kk], ((0, 0), (0, LANES - N_EXPERTS)))
    info, cum = _router(h, w_r)
    plan = _dispatch_plan(info, cum)
    hg, w_slot = _moe_gather(h, plan)
    hid, *side_out = _grouped_mm(hg, [w1, w3], (kk,), plan, [], w1.shape[-1],
                                 lambda accs, _: jax.nn.silu(accs[0]) * accs[1],
                                 tn=1024, vmem_mib=58, name="moe_up", make_side=make_side)
    (yw,) = _grouped_mm(hid, [w2], (kk,), plan, [w_slot], w2.shape[-1],
                        lambda accs, rows: accs[0] * rows[0],
                        tn=1024, vmem_mib=58, name="moe_down")
    mod_next = side_out[0] if side_out else None
    return _moe_combine(yw, plan, x_res, gate, make_norm(mod_next)), mod_next


def kernel(x, c, ada_w, ada_b, norm1_g, norm2_g, pool_w_in, pool_w_grp, pool_scale, pool_w_out, conv_w_in, conv_b_in, conv_dw_w, conv_dw_b, conv_ln_g, conv_ln_b, conv_w_out, sgu_w_in, sgu_b_in, sgu_ln_g, sgu_ln_b, sgu_w_s, sgu_b_s, sgu_w_out, sconv_w_in, sconv_w, sconv_w_out, ffn_w1, ffn_w3, ffn_w2, moe_router, moe_w1, moe_w3, moe_w2, final_g):
    batch, seq, d = x.shape
    assert batch == 1
    depth = ada_w.shape[0]
    xs = x.reshape(seq, d)
    c_col = c.reshape(d, 1)
    ada_b3 = ada_b.reshape(depth, 1, ada_b.shape[-1])
    split = lambda mod: [mod[:, q * d:(q + 1) * d] for q in range(6)]

    def side_for(layer):
        if layer >= depth:
            return None
        return lambda outer, inner: _AdaSide(c_col, ada_w, ada_b3, layer, outer, inner)

    mod_next = _ada_mod(c_col, ada_w, ada_b3, 0)
    final_g = final_g.reshape(1, d)
    h = None
    out = None
    for i in range(depth):
        sh1, sc1, g1, sh2, sc2, g2 = split(mod_next)

        if h is None:
            h = _norm_mod(xs, norm1_g[i:i + 1], sc1, sh1)
        mixer, j = i % 4, i // 4
        if mixer == 0:
            a = _pool_in(h, pool_w_in, pool_w_grp, pool_scale[j:j + 1], j)
            w_out = pool_w_out
        elif mixer == 1:
            z = _glu_in(h, conv_w_in, conv_b_in[j:j + 1], j)
            a = _conv_ln(z, conv_dw_w, conv_dw_b[j:j + 1], conv_ln_g[j:j + 1], conv_ln_b[j:j + 1], j)
            w_out = conv_w_out
        elif mixer == 2:
            a = _gelu_in(h, sgu_w_in, sgu_b_in[j:j + 1], j)
            a = _sgu_gate(a, sgu_ln_g[j:j + 1], sgu_ln_b[j:j + 1], sgu_w_s, sgu_b_s[j].T, j)
            w_out = sgu_w_out
        else:
            a = _sconv_in(h, sconv_w_in, sconv_w, j)
            w_out = sconv_w_out
        xs, h = _mixer_out_norm(a, w_out, j, xs, g1, norm2_g[i:i + 1], sc2, sh2)

        kk = i // 2
        if i % 2 == 0:
            hid, *side_out = _swiglu_up(h, ffn_w1, ffn_w3, (kk,), tm=1024, tn=512, vmem_mib=52,
                                        name="ffn_up", make_side=side_for(i + 1))
            xs = _mm_residual(hid, ffn_w2, (kk,), xs, g2, tm=512, tn=512, vmem_mib=56, name="ffn_down")
            mod_next = side_out[0] if side_out else None
            h = None
        elif i + 1 < depth:
            def next_norm(mod, i=i):
                sh_n, sc_n = split(mod)[:2]
                return (norm1_g[i + 1:i + 2], sc_n, sh_n)

            (xs, h), mod_next = _moe(h, xs, g2, moe_router, moe_w1, moe_w3, moe_w2, kk,
                                     side_for(i + 1), next_norm)
        else:
            (out,), _ = _moe(h, xs, g2, moe_router, moe_w1, moe_w3, moe_w2, kk, None,
                             lambda _: (final_g,))
    if out is None:
        out = _final_norm(xs, final_g)
    return out.reshape(batch, seq, d)
```

```python
import functools
import math

import jax
import jax.numpy as jnp
from jax import lax
from jax.experimental import pallas as pl
from jax.experimental.pallas import tpu as pltpu

EPS = 1e-6
POOL_WINDOWS = (2, 4, 8, 16)
CONV_WIDTH = 31
SGU_CHUNK = 128
SGU_HEADS = 8
SHORT_CONV_WIDTH = 3
N_EXPERTS = 8
MOE_ROWS = 512
MOE_SUB = 256
TOKEN_BLOCK = 512
LANES = 128
SUBLANES = 8
MIB = 1024 * 1024

BF16 = jnp.bfloat16
F32 = jnp.float32


def _params(vmem_mib, ndims):
    return pltpu.CompilerParams(
        dimension_semantics=("arbitrary",) * ndims,
        vmem_limit_bytes=vmem_mib * MIB)


def _ada_chunk(c_ref, w_ref, b_ref):
    c_act = jax.nn.silu(c_ref[...])
    return jnp.sum(w_ref[...] * c_act, axis=0, keepdims=True) + b_ref[...]


def _ada_mod(c_col, ada_w, ada_b3, layer):
    depth, d, n = ada_w.shape
    tn = 1024

    def body(c_ref, w_ref, b_ref, o_ref):
        o_ref[...] = _ada_chunk(c_ref, w_ref, b_ref)

    return pl.pallas_call(
        body,
        grid=(n // tn,),
        in_specs=[
            pl.BlockSpec((d, 1), lambda j: (0, 0)),
            pl.BlockSpec((None, d, tn), lambda j: (layer, 0, j)),
            pl.BlockSpec((None, 1, tn), lambda j: (layer, 0, j)),
        ],
        out_specs=pl.BlockSpec((1, tn), lambda j: (0, j)),
        out_shape=jax.ShapeDtypeStruct((1, n), F32),
        compiler_params=_params(40, 1),
        name="ada_mod",
    )(c_col, ada_w, ada_b3)


class _AdaSide:
    def __init__(self, c_col, ada_w, ada_b3, layer, outer, inner):
        self.arrays = (c_col, ada_w, ada_b3)
        self.layer, self.inner = layer, inner
        _, self.d, self.n = ada_w.shape
        self.chunk = next(ch for ch in range(LANES, self.n + 1, LANES)
                          if self.n % ch == 0 and self.n // ch <= outer * inner)
        self.n_chunks = self.n // self.chunk
        self.out_shape = jax.ShapeDtypeStruct((1, self.n), F32)

    def _chunk(self, n, m):
        return jnp.minimum(n * self.inner + m, self.n_chunks - 1)

    def specs(self):
        d, layer, chunk = self.d, self.layer, self.chunk
        ins = [pl.BlockSpec((d, 1), lambda n, m, *_: (0, 0)),
               pl.BlockSpec((None, d, chunk), lambda n, m, *_: (layer, 0, self._chunk(n, m))),
               pl.BlockSpec((None, 1, chunk), lambda n, m, *_: (layer, 0, self._chunk(n, m)))]
        return ins, pl.BlockSpec((1, chunk), lambda n, m, *_: (0, self._chunk(n, m)))

    def run(self, c_ref, w_ref, b_ref, o_ref):
        o_ref[...] = _ada_chunk(c_ref, w_ref, b_ref)


def _rms(x, g):
    return (x * lax.rsqrt(jnp.mean(x * x, axis=-1, keepdims=True) + EPS)) * g


def _rms_mod(x, g, sc, sh):
    return _rms(x, g) * (1.0 + sc) + sh


def _norm_mod_body(x_ref, g_ref, sc_ref, sh_ref, o_ref):
    o_ref[...] = _rms_mod(x_ref[...], g_ref[...], sc_ref[...], sh_ref[...]).astype(o_ref.dtype)


def _norm_mod(x, g, sc, sh):
    m, d = x.shape
    tm = 512
    row = pl.BlockSpec((1, d), lambda i: (0, 0))
    return pl.pallas_call(
        _norm_mod_body,
        grid=(m // tm,),
        in_specs=[pl.BlockSpec((tm, d), lambda i: (i, 0)), row, row, row],
        out_specs=pl.BlockSpec((tm, d), lambda i: (i, 0)),
        out_shape=jax.ShapeDtypeStruct((m, d), BF16),
        compiler_params=_params(40, 1),
        name="norm_mod",
    )(x, g, sc, sh)


def _final_norm_body(x_ref, g_ref, o_ref):
    o_ref[...] = _rms(x_ref[...], g_ref[...])


def _final_norm(x, g):
    m, d = x.shape
    tm = 512
    return pl.pallas_call(
        _final_norm_body,
        grid=(m // tm,),
        in_specs=[pl.BlockSpec((tm, d), lambda i: (i, 0)), pl.BlockSpec((1, d), lambda i: (0, 0))],
        out_specs=pl.BlockSpec((tm, d), lambda i: (i, 0)),
        out_shape=jax.ShapeDtypeStruct((m, d), F32),
        compiler_params=_params(40, 1),
        name="final_norm",
    )(x, g)


def _w_spec(w, lead, k, tn, off):
    assert w.shape[len(lead)] == k
    return pl.BlockSpec((None,) * len(lead) + (k, tn), lambda n, m: (*lead, 0, n + off))


def _row_spec(a, lead, tn, off=0):
    return pl.BlockSpec((None,) * len(lead) + (1, tn), lambda n, m: (*lead, 0, n + off))


def _colmm(x, weights, raw_inputs, outs, epilogue, *, tm, tn, nt, scratch=(), vmem_mib, name,
           make_side=None):
    m_rows, k = x.shape
    nw, nr, no = len(weights), len(raw_inputs), len(outs)
    side = make_side(nt, m_rows // tm) if make_side else None
    side_in, side_out = side.specs() if side else ([], None)
    ns = len(side_in)

    def body(*refs):
        x_ref = refs[0]
        w_refs = refs[1:1 + nw]
        r_refs = refs[1 + nw:1 + nw + nr]
        side_refs = refs[1 + nw + nr:1 + nw + nr + ns]
        o_refs = refs[1 + nw + nr + ns:1 + nw + nr + ns + no]
        rest = refs[1 + nw + nr + ns + no:]
        side_o, rest = (rest[0], rest[1:]) if side else (None, rest)
        wb_refs, extra = rest[:nw], rest[nw:]

        @pl.when(pl.program_id(1) == 0)
        def _():
            for w_ref, wb in zip(w_refs, wb_refs):
                wb[...] = w_ref[...].astype(BF16)

        xv = x_ref[...]
        accs = [jnp.dot(xv, wb[...], preferred_element_type=F32) for wb in wb_refs]
        epilogue(accs, r_refs, o_refs, extra)
        if side:
            side.run(*side_refs, side_o)

    in_specs = [pl.BlockSpec((tm, k), lambda n, m: (m, 0))]
    in_specs += [_w_spec(w, lead, k, tn, off) for (w, lead, off) in weights]
    in_specs += [spec for (_, spec) in raw_inputs]
    return pl.pallas_call(
        body,
        grid=(nt, m_rows // tm),
        in_specs=in_specs + side_in,
        out_specs=[spec for (_, spec) in outs] + ([side_out] if side else []),
        out_shape=[sds for (sds, _) in outs] + ([side.out_shape] if side else []),
        scratch_shapes=[pltpu.VMEM((k, tn), BF16) for _ in weights] + list(scratch),
        compiler_params=_params(vmem_mib, 2),
        name=name,
    )(x, *[w for (w, _, _) in weights], *[a for (a, _) in raw_inputs], *(side.arrays if side else ()))


def _tile_spec(tm, tn):
    return pl.BlockSpec((tm, tn), lambda n, m: (m, n))


def _mm_residual(a, w, lead, x_res, gate, *, tm, tn, vmem_mib, name):
    m_rows = a.shape[0]
    n_cols = x_res.shape[1]
    raw = [(gate, _row_spec(gate, (), tn)), (x_res, _tile_spec(tm, tn))]

    def epilogue(accs, r_refs, o_refs, _):
        o_refs[0][...] = r_refs[1][...] + accs[0] * r_refs[0][...]

    out = _colmm(a, [(w, lead, 0)], raw,
                 [(jax.ShapeDtypeStruct((m_rows, n_cols), F32), _tile_spec(tm, tn))],
                 epilogue, tm=tm, tn=tn, nt=n_cols // tn, vmem_mib=vmem_mib, name=name)
    return out[0]


def _mixer_out_norm(a, w, j, x_res, gate, g, sc, sh):
    m_rows, k = a.shape
    d = x_res.shape[1]
    tm = 256

    def body(a_ref, w_ref, gate_ref, x_ref, g_ref, sc_ref, sh_ref, xo_ref, ho_ref, wb):
        @pl.when(pl.program_id(0) == 0)
        def _():
            wb[...] = w_ref[...].astype(BF16)

        y = jnp.dot(a_ref[...], wb[...], preferred_element_type=F32)
        x_new = x_ref[...] + gate_ref[...] * y
        xo_ref[...] = x_new
        ho_ref[...] = _rms_mod(x_new, g_ref[...], sc_ref[...], sh_ref[...]).astype(BF16)

    row = pl.BlockSpec((1, d), lambda i: (0, 0))
    blk_in = pl.BlockSpec((tm, k), lambda i: (i, 0))
    blk_d = pl.BlockSpec((tm, d), lambda i: (i, 0))
    return pl.pallas_call(
        body,
        grid=(m_rows // tm,),
        in_specs=[blk_in,
                  pl.BlockSpec((None, k, d), lambda i: (j, 0, 0), pipeline_mode=pl.Buffered(1)),
                  row, blk_d, row, row, row],
        out_specs=[blk_d, blk_d],
        out_shape=[jax.ShapeDtypeStruct((m_rows, d), F32), jax.ShapeDtypeStruct((m_rows, d), BF16)],
        scratch_shapes=[pltpu.VMEM((k, d), BF16)],
        compiler_params=_params(48, 1),
        name="mixer_out",
    )(a, w, gate, x_res, g, sc, sh)


def _swiglu_up(h, w1, w3, lead, *, tm, tn, vmem_mib, name, make_side=None):
    m_rows = h.shape[0]
    n_cols = w1.shape[-1]

    def epilogue(accs, r_refs, o_refs, _):
        o_refs[0][...] = (jax.nn.silu(accs[0]) * accs[1]).astype(BF16)

    return _colmm(h, [(w1, lead, 0), (w3, lead, 0)], [],
                  [(jax.ShapeDtypeStruct((m_rows, n_cols), BF16), _tile_spec(tm, tn))],
                  epilogue, tm=tm, tn=tn, nt=n_cols // tn, vmem_mib=vmem_mib, name=name,
                  make_side=make_side)


def _pool_in(h, w_in, w_grp, scale, j):
    m_rows, k = h.shape
    n_groups = len(POOL_WINDOWS)
    tn = w_in.shape[-1] // n_groups
    tm = 1024
    halo = max(POOL_WINDOWS)

    def epilogue(accs, r_refs, o_refs, s_refs):
        z = accs[0]
        wg_ref, scale_ref = r_refs
        zext, wgb, pooled = s_refs
        n, m = pl.program_id(0), pl.program_id(1)

        @pl.when(m == 0)
        def _():
            zext[0:halo, :] = jnp.zeros((halo, tn), F32)
            wgb[...] = wg_ref[...].astype(BF16)

        zext[halo:halo + tm, :] = z
        t = m * tm + lax.broadcasted_iota(jnp.int32, (tm, tn), 0)
        for gi, win in enumerate(POOL_WINDOWS):
            @pl.when(n == gi)
            def _(win=win):
                wsum = z
                for back in range(1, win):
                    wsum = wsum + zext[halo - back:halo - back + tm, :]
                count = jnp.minimum(t + 1, win).astype(F32)
                pooled[...] = (wsum / count - z).astype(BF16)
        zext[0:halo, :] = zext[tm:tm + halo, :]
        mixed = jnp.dot(pooled[...], wgb[...], preferred_element_type=F32)
        o_refs[0][...] = (mixed * scale_ref[...]).astype(BF16)

    raw = [(w_grp, pl.BlockSpec((None, None, tn, tn), lambda n, m: (j, n, 0, 0))),
           (scale, _row_spec(scale, (), tn))]
    out = _colmm(h, [(w_in, (j,), 0)], raw,
                 [(jax.ShapeDtypeStruct((m_rows, n_groups * tn), BF16), _tile_spec(tm, tn))],
                 epilogue, tm=tm, tn=tn, nt=n_groups,
                 scratch=[pltpu.VMEM((tm + halo, tn), F32), pltpu.VMEM((tn, tn), BF16),
                          pltpu.VMEM((tm, tn), BF16)],
                 vmem_mib=48, name="pool_in")
    return out[0]


def _glu_in(h, w_in, b_in, j, make_side=None):
    m_rows, k = h.shape
    dm = w_in.shape[-1] // 2
    tm, tn = 1024, 512
    nt = dm // tn

    def epilogue(accs, r_refs, o_refs, _):
        a = accs[0] + r_refs[0][...]
        g = accs[1] + r_refs[1][...]
        o_refs[0][...] = (a * jax.nn.sigmoid(g)).astype(BF16)

    raw = [(b_in, _row_spec(b_in, (), tn)), (b_in, _row_spec(b_in, (), tn, off=nt))]
    out = _colmm(h, [(w_in, (j,), 0), (w_in, (j,), nt)], raw,
                 [(jax.ShapeDtypeStruct((m_rows, dm), BF16), _tile_spec(tm, tn))],
                 epilogue, tm=tm, tn=tn, nt=nt, vmem_mib=52, name="glu_in", make_side=make_side)
    return out


def _conv_ln_body(z_ref, w_ref, cb_ref, g_ref, b_ref, o_ref, zsh, wbc, conv, *, tm, halo, rows, cols):
    d = z_ref.shape[1]
    sl = SUBLANES

    @pl.when(pl.program_id(0) == 0)
    def _():
        zsh[0, 0:halo, :] = jnp.zeros((halo, d), F32)
        for kk in range(CONV_WIDTH):
            wbc[kk] = jnp.broadcast_to(w_ref[kk:kk + 1, :], (sl, d))

    zsh[0, halo:halo + tm, :] = z_ref[...].astype(F32)
    for b in range(1, sl):
        zsh[b, sl:halo + tm, :] = zsh[0, sl - b:halo + tm - b, :]

    for c0 in range(0, d, cols):
        cs = slice(c0, c0 + cols)
        for r0 in range(0, tm, rows):
            acc = jnp.zeros((rows, cols), F32)
            for kk in range(CONV_WIDTH):
                a, b = divmod(CONV_WIDTH - 1 - kk, sl)
                start = halo + r0 - sl * a
                wv = jnp.tile(wbc[kk, :, cs], (rows // sl, 1))
                acc = acc + wv * zsh[b, start:start + rows, cs]
            conv[r0:r0 + rows, cs] = acc + cb_ref[:, cs]
    zsh[0, 0:halo, :] = zsh[0, tm:tm + halo, :]

    y = conv[...]
    mu = jnp.mean(y, axis=-1, keepdims=True)
    var = jnp.mean(jnp.square(y - mu), axis=-1, keepdims=True)
    yn = (y - mu) * lax.rsqrt(var + EPS) * g_ref[...] + b_ref[...]
    o_ref[...] = jax.nn.silu(yn).astype(o_ref.dtype)


def _conv_ln(z, dw_w, dw_b, ln_g, ln_b, j):
    m_rows, d = z.shape
    tm, halo = 256, 32
    row = pl.BlockSpec((1, d), lambda i: (0, 0))
    return pl.pallas_call(
        functools.partial(_conv_ln_body, tm=tm, halo=halo, rows=64, cols=512),
        grid=(m_rows // tm,),
        in_specs=[pl.BlockSpec((tm, d), lambda i: (i, 0)),
                  pl.BlockSpec((None, CONV_WIDTH, d), lambda i: (j, 0, 0)), row, row, row],
        out_specs=pl.BlockSpec((tm, d), lambda i: (i, 0)),
        out_shape=jax.ShapeDtypeStruct((m_rows, d), BF16),
        scratch_shapes=[pltpu.VMEM((SUBLANES, tm + halo, d), F32),
                        pltpu.VMEM((CONV_WIDTH, SUBLANES, d), F32), pltpu.VMEM((tm, d), F32)],
        compiler_params=_params(40, 1),
        name="conv_ln",
    )(z, dw_w, dw_b, ln_g, ln_b)


def _gelu_in(h, w_in, b_in, j):
    m_rows, k = h.shape
    n_cols = w_in.shape[-1]
    tm, tn = 512, 1024

    def epilogue(accs, r_refs, o_refs, _):
        a = accs[0] + r_refs[0][...]
        o_refs[0][...] = (0.5 * a * (1.0 + lax.erf(a * math.sqrt(0.5)))).astype(BF16)

    out = _colmm(h, [(w_in, (j,), 0)], [(b_in, _row_spec(b_in, (), tn))],
                 [(jax.ShapeDtypeStruct((m_rows, n_cols), BF16), _tile_spec(tm, tn))],
                 epilogue, tm=tm, tn=tn, nt=n_cols // tn, vmem_mib=48, name="gelu_in")
    return out[0]


def _sgu_gate_body(u_ref, v_ref, g_ref, b_ref, ws_ref, bst_ref, o_ref, *, tm):
    v = v_ref[...].astype(F32)
    mu = jnp.mean(v, axis=-1, keepdims=True)
    var = jnp.mean(jnp.square(v - mu), axis=-1, keepdims=True)
    vn = ((v - mu) * lax.rsqrt(var + EPS) * g_ref[...] + b_ref[...]).astype(BF16)
    hd = v.shape[1] // SGU_HEADS
    tri = (lax.broadcasted_iota(jnp.int32, (SGU_CHUNK, SGU_CHUNK), 0)
           >= lax.broadcasted_iota(jnp.int32, (SGU_CHUNK, SGU_CHUNK), 1))
    for hh in range(SGU_HEADS):
        wc = jnp.where(tri, ws_ref[hh], 0.0).astype(BF16)
        bias = bst_ref[:, hh:hh + 1]
        for ck in range(tm // SGU_CHUNK):
            rs = slice(ck * SGU_CHUNK, (ck + 1) * SGU_CHUNK)
            cs = slice(hh * hd, (hh + 1) * hd)
            sv = jnp.dot(wc, vn[rs, cs], preferred_element_type=F32) + bias
            o_ref[rs, cs] = (u_ref[rs, cs].astype(F32) * sv).astype(o_ref.dtype)


def _sgu_gate(a, ln_g, ln_b, w_s, b_s_t, j):
    m_rows = a.shape[0]
    d = a.shape[1] // 2
    tm = 256
    row = pl.BlockSpec((1, d), lambda i: (0, 0))
    return pl.pallas_call(
        functools.partial(_sgu_gate_body, tm=tm),
        grid=(m_rows // tm,),
        in_specs=[pl.BlockSpec((tm, d), lambda i: (i, 0)), pl.BlockSpec((tm, d), lambda i: (i, 1)),
                  row, row,
                  pl.BlockSpec((None, SGU_HEADS, SGU_CHUNK, SGU_CHUNK), lambda i: (j, 0, 0, 0)),
                  pl.BlockSpec((SGU_CHUNK, SGU_HEADS), lambda i: (0, 0))],
        out_specs=pl.BlockSpec((tm, d), lambda i: (i, 0)),
        out_shape=jax.ShapeDtypeStruct((m_rows, d), BF16),
        compiler_params=_params(40, 1),
        name="sgu_gate",
    )(a, a, ln_g, ln_b, w_s, b_s_t)


def _sconv_in(h, w_in, conv_w, j, make_side=None):
    m_rows, k = h.shape
    dm = w_in.shape[-1] // 3
    tm, tn = 1024, 256
    nt = dm // tn
    halo = SUBLANES

    def epilogue(accs, r_refs, o_refs, s_refs):
        bg, cg, z = accs
        cw_ref = r_refs[0]
        ext = s_refs[0]

        @pl.when(pl.program_id(1) == 0)
        def _():
            ext[0:halo, :] = jnp.zeros((halo, tn), F32)

        cz = cg * z
        ext[halo:halo + tm, :] = cz
        conv = cw_ref[SHORT_CONV_WIDTH - 1:SHORT_CONV_WIDTH, :] * cz
        for back in range(1, SHORT_CONV_WIDTH):
            tap = SHORT_CONV_WIDTH - 1 - back
            conv = conv + cw_ref[tap:tap + 1, :] * ext[halo - back:halo - back + tm, :]
        ext[0:halo, :] = ext[tm:tm + halo, :]
        o_refs[0][...] = (bg * conv).astype(BF16)

    raw = [(conv_w, pl.BlockSpec((None, SHORT_CONV_WIDTH, tn), lambda n, m: (j, 0, n)))]
    out = _colmm(h, [(w_in, (j,), 0), (w_in, (j,), nt), (w_in, (j,), 2 * nt)], raw,
                 [(jax.ShapeDtypeStruct((m_rows, dm), BF16), _tile_spec(tm, tn))],
                 epilogue, tm=tm, tn=tn, nt=nt,
                 scratch=[pltpu.VMEM((tm + halo, tn), F32)], vmem_mib=52, name="sconv_in",
                 make_side=make_side)
    return out


def _router_body(h_ref, w_ref, info_ref, cum_ref, carry):
    logits = jnp.dot(h_ref[...], w_ref[...].astype(BF16), preferred_element_type=F32)
    lane = lax.broadcasted_iota(jnp.int32, logits.shape, 1)
    neg = jnp.float32(-jnp.inf)
    lg = jnp.where(lane < N_EXPERTS, logits, neg)
    v1 = jnp.max(lg, axis=-1, keepdims=True)
    lane_f = lane.astype(F32)
    i1 = jnp.min(jnp.where(lg == v1, lane_f, float(LANES)), axis=-1, keepdims=True)
    lg2 = jnp.where(lane_f == i1, neg, lg)
    v2 = jnp.max(lg2, axis=-1, keepdims=True)
    i2 = jnp.min(jnp.where(lg2 == v2, lane_f, float(LANES)), axis=-1, keepdims=True)
    e2 = jnp.exp(v2 - v1)
    den = 1.0 + e2
    w1, w2 = 1.0 / den, e2 / den

    @pl.when(pl.program_id(0) == 0)
    def _():
        carry[...] = jnp.zeros(carry.shape, F32)

    tm = logits.shape[0]
    cnt = jnp.where(lane_f == i1, 1.0, 0.0) + jnp.where(lane_f == i2, 1.0, 0.0)
    strict = jnp.where(lax.broadcasted_iota(jnp.int32, (tm, tm), 1)
                       < lax.broadcasted_iota(jnp.int32, (tm, tm), 0), 1.0, 0.0).astype(BF16)
    before = jnp.dot(strict, cnt.astype(BF16), preferred_element_type=F32) + carry[...]
    r1 = jnp.sum(jnp.where(lane_f == i1, before, 0.0), axis=-1, keepdims=True)
    r2 = jnp.sum(jnp.where(lane_f == i2, before, 0.0), axis=-1, keepdims=True)
    fields = (i1, i2, w1, w2, r1, r2)
    info = jnp.zeros(logits.shape, F32)
    for q, val in enumerate(fields):
        info = jnp.where(lane == q, val, info)
    info_ref[...] = info[:, :info_ref.shape[1]]
    total = carry[...] + jnp.sum(cnt, axis=0, keepdims=True)
    carry[...] = total
    cum_ref[...] = total


def _router(h, w_router_padded):
    m_rows, k = h.shape
    tm = MOE_SUB
    return pl.pallas_call(
        _router_body,
        grid=(m_rows // tm,),
        in_specs=[pl.BlockSpec((tm, k), lambda i: (i, 0)), pl.BlockSpec((k, LANES), lambda i: (0, 0))],
        out_specs=[pl.BlockSpec((tm, SUBLANES), lambda i: (i, 0)),
                   pl.BlockSpec((None, 1, LANES), lambda i: (i, 0, 0))],
        out_shape=[jax.ShapeDtypeStruct((m_rows, SUBLANES), F32),
                   jax.ShapeDtypeStruct((m_rows // tm, 1, LANES), F32)],
        scratch_shapes=[pltpu.VMEM((1, LANES), F32)],
        compiler_params=_params(40, 1),
        name="router",
    )(h, w_router_padded)


def _dispatch_plan(info, cum):
    rows, sub, tok, n_exp = MOE_ROWS, MOE_SUB, TOKEN_BLOCK, N_EXPERTS
    per = rows // sub
    n_tok = info.shape[0]
    tbn = n_tok // tok
    nb_max = (2 * n_tok) // rows + n_exp
    nq = per * nb_max
    s_max = nq + n_exp * (tbn - 1)
    i32 = jnp.int32
    ids = info.astype(i32)
    i1, i2, r1, r2 = ids[:, 0], ids[:, 1], ids[:, 4], ids[:, 5]
    cb_fine = jnp.concatenate([jnp.zeros((1, n_exp), i32), cum[:, 0, :n_exp].astype(i32)])
    cb = cb_fine[::tok // sub]
    counts = cb[-1]
    nblk = (counts + rows - 1) // rows
    blk_end = jnp.cumsum(nblk)
    blk_off = blk_end - nblk
    nb = blk_end[-1]
    experts = jnp.arange(n_exp, dtype=i32)[None, :]
    row_off = (blk_off * rows)[None, :]
    slot1 = jnp.sum(jnp.where(i1[:, None] == experts, row_off, 0), axis=1) + r1
    slot2 = jnp.sum(jnp.where(i2[:, None] == experts, row_off, 0), axis=1) + r2

    b_idx = jnp.arange(nb_max, dtype=i32)
    b_used = b_idx < nb
    b_clamped = jnp.minimum(b_idx, nb - 1)
    blk_e = jnp.minimum(jnp.sum(b_clamped[:, None] >= blk_end[None, :], axis=1), n_exp - 1).astype(i32)
    prev_e = jnp.concatenate([jnp.full((1,), -1, i32), blk_e[:-1]])
    tail_rows = counts[blk_e] - (blk_end[blk_e] - 1 - blk_off[blk_e]) * rows
    half = (b_clamped == blk_end[blk_e] - 1) & (tail_rows <= rows // 2)
    blk_flag = (b_used.astype(i32) + 2 * (b_used & (blk_e != prev_e)).astype(i32)
                + 4 * (b_used & half).astype(i32))

    q = jnp.arange(nq, dtype=i32)
    e_q = jnp.minimum(jnp.sum(q[:, None] >= per * blk_end[None, :], axis=1), n_exp - 1)
    k_q = (q - per * blk_off[e_q])[:, None]
    lo = jnp.maximum(k_q * sub, cb[:-1].T[e_q])
    hi = jnp.minimum((k_q + 1) * sub, cb[1:].T[e_q])
    inter = (lo < hi) & (q < per * nb)[:, None]
    first_tb = jnp.arange(tbn, dtype=i32)[None, :] == 0
    touch = inter | (~jnp.any(inter, axis=1, keepdims=True) & first_tb)
    s_idx = jnp.arange(s_max, dtype=i32)

    def step_lists(mask, inner):
        flat = mask.ravel()
        running = jnp.cumsum(flat.astype(i32))
        n_steps = running[-1]
        s_clamped = jnp.minimum(s_idx, n_steps - 1)
        f = jnp.sum(running[None, :] <= s_clamped[:, None], axis=1).astype(i32)
        major, minor = f // inner, f % inner
        s_valid = s_idx < n_steps
        prev_m = jnp.concatenate([jnp.full((1,), -1, i32), major[:-1]])
        next_m = jnp.concatenate([major[1:], jnp.full((1,), -1, i32)])
        first = s_valid & (prev_m != major)
        last = s_valid & ((next_m != major) | (s_idx == n_steps - 1))
        return major, minor, s_valid, 2 * first.astype(i32) + 4 * last.astype(i32)

    g_q, g_tb, g_valid, g_edge = step_lists(touch, tbn)
    g_flag = (g_valid & inter[g_q, g_tb]).astype(i32) + g_edge

    c_q0 = jnp.minimum((row_off + cb_fine[:-1]) // sub, nq - 2).astype(i32).ravel()
    c_lo = (blk_off * rows).astype(i32)
    c_hi = (blk_end * rows).astype(i32)
    return dict(slot1=slot1, slot2=slot2, w1=info[:, 2], w2=info[:, 3],
                blk_e=blk_e, blk_row=b_clamped, blk_flag=blk_flag,
                g_q=g_q, g_tb=g_tb, g_flag=g_flag, c_q0=c_q0, c_lo=c_lo, c_hi=c_hi,
                nb_max=nb_max, s_max=s_max)


def _moe_gather(h, plan):
    n_tok, d = h.shape
    sub, tok = MOE_SUB, TOKEN_BLOCK
    p_rows = plan["nb_max"] * MOE_ROWS

    def body(q_ref, tb_ref, fl_ref, h_ref, s1_ref, s2_ref, w1_ref, w2_ref, hg_ref, ws_ref, acc, wacc):
        s = pl.program_id(0)
        flag = fl_ref[s]

        @pl.when((flag & 2) != 0)
        def _():
            acc[...] = jnp.zeros(acc.shape, F32)
            wacc[...] = jnp.zeros(wacc.shape, F32)

        @pl.when((flag & 1) != 0)
        def _():
            slot = q_ref[s] * sub + lax.broadcasted_iota(jnp.int32, (sub, tok), 0)
            d1 = s1_ref[...] == slot
            d2 = s2_ref[...] == slot
            sel = jnp.where(d1, 1.0, jnp.where(d2, 1.0, 0.0)).astype(BF16)
            t0 = pl.multiple_of(tb_ref[s] * tok, tok)
            acc[...] += jnp.dot(sel, h_ref[pl.ds(t0, tok), :], preferred_element_type=F32)
            wacc[...] += jnp.sum(jnp.where(d1, w1_ref[...], 0.0) + jnp.where(d2, w2_ref[...], 0.0),
                                 axis=1, keepdims=True)

        @pl.when((flag & 4) != 0)
        def _():
            hg_ref[...] = acc[...].astype(BF16)
            ws_ref[...] = wacc[...]

    tok_row = pl.BlockSpec((None, 1, tok), lambda s, q, tb, fl: (tb[s], 0, 0))
    grid_spec = pltpu.PrefetchScalarGridSpec(
        num_scalar_prefetch=3,
        grid=(plan["s_max"],),
        in_specs=[pl.BlockSpec((n_tok, d), lambda s, q, tb, fl: (0, 0), pipeline_mode=pl.Buffered(1)),
                  tok_row, tok_row, tok_row, tok_row],
        out_specs=[pl.BlockSpec((sub, d), lambda s, q, tb, fl: (q[s], 0)),
                   pl.BlockSpec((sub, 1), lambda s, q, tb, fl: (q[s], 0))],
        scratch_shapes=[pltpu.VMEM((sub, d), F32), pltpu.VMEM((sub, 1), F32)])
    rows = lambda a: a.reshape(n_tok // tok, 1, tok)
    return pl.pallas_call(
        body, grid_spec=grid_spec,
        out_shape=[jax.ShapeDtypeStruct((p_rows, d), BF16), jax.ShapeDtypeStruct((p_rows, 1), F32)],
        compiler_params=_params(48, 1), name="moe_gather",
    )(plan["g_q"], plan["g_tb"], plan["g_flag"], h,
      rows(plan["slot1"]), rows(plan["slot2"]), rows(plan["w1"]), rows(plan["w2"]))


def _grouped_mm(x, weights, lead, plan, row_inputs, out_cols, epilogue, *, tn, vmem_mib, name):
    p_rows, k = x.shape
    blk = MOE_ROWS
    half = blk // 2
    nw, nr = len(weights), len(row_inputs)

    def body(be_ref, br_ref, fl_ref, x_ref, *refs):
        w_refs, r_refs = refs[:nw], refs[nw:nw + nr]
        o_ref = refs[nw + nr]
        wb_refs = refs[nw + nr + 1:]
        flag = fl_ref[pl.program_id(1)]

        @pl.when((flag & 2) != 0)
        def _():
            for w_ref, wb in zip(w_refs, wb_refs):
                wb[...] = w_ref[...].astype(BF16)

        def compute(n_rows):
            xv = x_ref[0:n_rows, :]
            accs = [jnp.dot(xv, wb[...], preferred_element_type=F32) for wb in wb_refs]
            o_ref[0:n_rows, :] = epilogue(accs, [r[0:n_rows, :] for r in r_refs]).astype(BF16)

        @pl.when((flag & 5) == 1)
        def _():
            compute(blk)

        @pl.when((flag & 5) == 5)
        def _():
            compute(half)
            o_ref[half:blk, :] = jnp.zeros((blk - half, tn), BF16)

        @pl.when((flag & 1) == 0)
        def _():
            o_ref[...] = jnp.zeros((blk, tn), BF16)

    w_spec = pl.BlockSpec((None,) * (len(lead) + 1) + (k, tn),
                          lambda n, b, be, br, fl: (*lead, be[b], 0, n))
    grid_spec = pltpu.PrefetchScalarGridSpec(
        num_scalar_prefetch=3,
        grid=(out_cols // tn, plan["nb_max"]),
        in_specs=[pl.BlockSpec((blk, k), lambda n, b, be, br, fl: (br[b], 0))]
        + [w_spec] * nw
        + [pl.BlockSpec((blk, a.shape[1]), lambda n, b, be, br, fl: (br[b], 0)) for a in row_inputs],
        out_specs=pl.BlockSpec((blk, tn), lambda n, b, be, br, fl: (b, n)),
        scratch_shapes=[pltpu.VMEM((k, tn), BF16) for _ in weights])
    return pl.pallas_call(
        body, grid_spec=grid_spec,
        out_shape=jax.ShapeDtypeStruct((p_rows, out_cols), BF16),
        compiler_params=_params(vmem_mib, 2), name=name,
    )(plan["blk_e"], plan["blk_row"], plan["blk_flag"], x, *weights, *row_inputs)


def _moe_combine(yw, plan, x_res, gate, norm):
    n_tok, d = x_res.shape
    sub = MOE_SUB
    n_win = 2 * N_EXPERTS
    n_norm = len(norm)

    def body(q0_ref, lo_ref, hi_ref, *refs):
        y_refs = refs[:n_win]
        s1_ref, s2_ref, x_ref, g_ref = refs[n_win:n_win + 4]
        norm_refs = refs[n_win + 4:n_win + 4 + n_norm]
        o_refs = refs[n_win + 4 + n_norm:]
        t = pl.program_id(0)
        s1, s2 = s1_ref[...], s2_ref[...]
        col = lax.broadcasted_iota(jnp.int32, (sub, sub), 1)
        acc = None
        for e in range(N_EXPERTS):
            for j in range(2):
                slot = (q0_ref[t * N_EXPERTS + e] + j) * sub + col
                inside = (slot - lo_ref[e]).astype(jnp.uint32) < (hi_ref[e] - lo_ref[e]).astype(jnp.uint32)
                slot = jnp.where(inside, slot, -1)
                sel = jnp.where(s1 == slot, 1.0, jnp.where(s2 == slot, 1.0, 0.0)).astype(BF16)
                part = jnp.dot(sel, y_refs[2 * e + j][...], preferred_element_type=F32)
                acc = part if acc is None else acc + part
        x_new = x_ref[...] + g_ref[...] * acc
        if n_norm == 1:
            o_refs[0][...] = _rms(x_new, norm_refs[0][...])
        else:
            o_refs[0][...] = x_new
            o_refs[1][...] = _rms_mod(x_new, *[r[...] for r in norm_refs]).astype(BF16)

    def window_spec(e, j):
        return pl.BlockSpec((sub, d), lambda t, q0, lo, hi: (q0[t * N_EXPERTS + e] + j, 0))

    tok_col = pl.BlockSpec((sub, 1), lambda t, q0, lo, hi: (t, 0))
    tok_blk = pl.BlockSpec((sub, d), lambda t, q0, lo, hi: (t, 0))
    row = pl.BlockSpec((1, d), lambda t, q0, lo, hi: (0, 0))
    out_shape = [jax.ShapeDtypeStruct((n_tok, d), F32)]
    if n_norm > 1:
        out_shape.append(jax.ShapeDtypeStruct((n_tok, d), BF16))
    grid_spec = pltpu.PrefetchScalarGridSpec(
        num_scalar_prefetch=3,
        grid=(n_tok // sub,),
        in_specs=[window_spec(e, j) for e in range(N_EXPERTS) for j in range(2)]
        + [tok_col, tok_col, tok_blk, row] + [row] * n_norm,
        out_specs=[tok_blk] * len(out_shape))
    col_of = lambda a: a.reshape(n_tok, 1)
    return pl.pallas_call(
        body, grid_spec=grid_spec,
        out_shape=out_shape,
        compiler_params=_params(56, 1), name="moe_combine",
    )(plan["c_q0"], plan["c_lo"], plan["c_hi"], *([yw] * n_win),
      col_of(plan["slot1"]), col_of(plan["slot2"]), x_res, gate, *norm)


def _moe(h, x_res, gate, w_router, w1, w3, w2, kk, norm):
    w_r = jnp.pad(w_router[kk], ((0, 0), (0, LANES - N_EXPERTS)))
    info, cum = _router(h, w_r)
    plan = _dispatch_plan(info, cum)
    hg, w_slot = _moe_gather(h, plan)
    hid = _grouped_mm(hg, [w1, w3], (kk,), plan, [], w1.shape[-1],
                      lambda accs, _: jax.nn.silu(accs[0]) * accs[1],
                      tn=1024, vmem_mib=58, name="moe_up")
    yw = _grouped_mm(hid, [w2], (kk,), plan, [w_slot], w2.shape[-1],
                     lambda accs, rows: accs[0] * rows[0],
                     tn=1024, vmem_mib=58, name="moe_down")
    return _moe_combine(yw, plan, x_res, gate, norm)


def kernel(x, c, ada_w, ada_b, norm1_g, norm2_g, pool_w_in, pool_w_grp, pool_scale, pool_w_out, conv_w_in, conv_b_in, conv_dw_w, conv_dw_b, conv_ln_g, conv_ln_b, conv_w_out, sgu_w_in, sgu_b_in, sgu_ln_g, sgu_ln_b, sgu_w_s, sgu_b_s, sgu_w_out, sconv_w_in, sconv_w, sconv_w_out, ffn_w1, ffn_w3, ffn_w2, moe_router, moe_w1, moe_w3, moe_w2, final_g):
    batch, seq, d = x.shape
    assert batch == 1
    depth = ada_w.shape[0]
    xs = x.reshape(seq, d)
    c_col = c.reshape(d, 1)
    ada_b3 = ada_b.reshape(depth, 1, ada_b.shape[-1])
    split = lambda mod: [mod[:, q * d:(q + 1) * d] for q in range(6)]

    def side_for(layer):
        if layer >= depth:
            return None
        return lambda outer, inner: _AdaSide(c_col, ada_w, ada_b3, layer, outer, inner)

    mod_next = _ada_mod(c_col, ada_w, ada_b3, 0)
    final_g = final_g.reshape(1, d)
    h = None
    out = None
    for i in range(depth):
        sh1, sc1, g1, sh2, sc2, g2 = split(mod_next)

        if h is None:
            h = _norm_mod(xs, norm1_g[i:i + 1], sc1, sh1)
        mixer, j = i % 4, i // 4
        mixer_side = side_for(i + 1) if i % 2 == 1 else None
        side_out = []
        if mixer == 0:
            a = _pool_in(h, pool_w_in, pool_w_grp, pool_scale[j:j + 1], j)
            w_out = pool_w_out
        elif mixer == 1:
            z, *side_out = _glu_in(h, conv_w_in, conv_b_in[j:j + 1], j, mixer_side)
            a = _conv_ln(z, conv_dw_w, conv_dw_b[j:j + 1], conv_ln_g[j:j + 1], conv_ln_b[j:j + 1], j)
            w_out = conv_w_out
        elif mixer == 2:
            a = _gelu_in(h, sgu_w_in, sgu_b_in[j:j + 1], j)
            a = _sgu_gate(a, sgu_ln_g[j:j + 1], sgu_ln_b[j:j + 1], sgu_w_s, sgu_b_s[j].T, j)
            w_out = sgu_w_out
        else:
            a, *side_out = _sconv_in(h, sconv_w_in, sconv_w, j, mixer_side)
            w_out = sconv_w_out
        xs, h = _mixer_out_norm(a, w_out, j, xs, g1, norm2_g[i:i + 1], sc2, sh2)

        kk = i // 2
        if i % 2 == 0:
            hid, *side_out = _swiglu_up(h, ffn_w1, ffn_w3, (kk,), tm=1024, tn=512, vmem_mib=52,
                                        name="ffn_up", make_side=side_for(i + 1))
            xs = _mm_residual(hid, ffn_w2, (kk,), xs, g2, tm=512, tn=512, vmem_mib=56, name="ffn_down")
            mod_next = side_out[0] if side_out else None
            h = None
        elif i + 1 < depth:
            mod_next = side_out[0]
            sh_n, sc_n = split(mod_next)[:2]
            xs, h = _moe(h, xs, g2, moe_router, moe_w1, moe_w3, moe_w2, kk,
                         (norm1_g[i + 1:i + 2], sc_n, sh_n))
        else:
            (out,) = _moe(h, xs, g2, moe_router, moe_w1, moe_w3, moe_w2, kk, (final_g,))
    if out is None:
        out = _final_norm(xs, final_g)
    return out.reshape(batch, seq, d)
```

```python
import functools
import math

import jax
import jax.numpy as jnp
from jax import lax
from jax.experimental import pallas as pl
from jax.experimental.pallas import tpu as pltpu

EPS = 1e-6
POOL_WINDOWS = (2, 4, 8, 16)
CONV_WIDTH = 31
SGU_CHUNK = 128
SGU_HEADS = 8
SHORT_CONV_WIDTH = 3
N_EXPERTS = 8
MOE_ROWS = 512
MOE_SUB = 256
TOKEN_BLOCK = 512
LANES = 128
SUBLANES = 8
MIB = 1024 * 1024

BF16 = jnp.bfloat16
F32 = jnp.float32


def _params(vmem_mib, ndims):
    return pltpu.CompilerParams(
        dimension_semantics=("arbitrary",) * ndims,
        vmem_limit_bytes=vmem_mib * MIB)


def _ada_chunk(c_ref, w_ref, b_ref):
    c_act = jax.nn.silu(c_ref[...])
    return jnp.sum(w_ref[...] * c_act, axis=0, keepdims=True) + b_ref[...]


def _ada_mod(c_col, ada_w, ada_b3, layer):
    depth, d, n = ada_w.shape
    tn = 1024

    def body(c_ref, w_ref, b_ref, o_ref):
        o_ref[...] = _ada_chunk(c_ref, w_ref, b_ref)

    return pl.pallas_call(
        body,
        grid=(n // tn,),
        in_specs=[
            pl.BlockSpec((d, 1), lambda j: (0, 0)),
            pl.BlockSpec((None, d, tn), lambda j: (layer, 0, j)),
            pl.BlockSpec((None, 1, tn), lambda j: (layer, 0, j)),
        ],
        out_specs=pl.BlockSpec((1, tn), lambda j: (0, j)),
        out_shape=jax.ShapeDtypeStruct((1, n), F32),
        compiler_params=_params(40, 1),
        name="ada_mod",
    )(c_col, ada_w, ada_b3)


class _AdaSide:
    def __init__(self, c_col, ada_w, ada_b3, layer, outer, inner):
        self.arrays = (c_col, ada_w, ada_b3)
        self.layer, self.inner = layer, inner
        _, self.d, self.n = ada_w.shape
        self.chunk = next(ch for ch in range(LANES, self.n + 1, LANES)
                          if self.n % ch == 0 and self.n // ch <= outer * inner)
        self.n_chunks = self.n // self.chunk
        self.out_shape = jax.ShapeDtypeStruct((1, self.n), F32)

    def _chunk(self, n, m):
        return jnp.minimum(n * self.inner + m, self.n_chunks - 1)

    def specs(self):
        d, layer, chunk = self.d, self.layer, self.chunk
        ins = [pl.BlockSpec((d, 1), lambda n, m, *_: (0, 0)),
               pl.BlockSpec((None, d, chunk), lambda n, m, *_: (layer, 0, self._chunk(n, m))),
               pl.BlockSpec((None, 1, chunk), lambda n, m, *_: (layer, 0, self._chunk(n, m)))]
        return ins, pl.BlockSpec((1, chunk), lambda n, m, *_: (0, self._chunk(n, m)))

    def run(self, c_ref, w_ref, b_ref, o_ref):
        o_ref[...] = _ada_chunk(c_ref, w_ref, b_ref)


def _rms(x, g):
    return (x * lax.rsqrt(jnp.mean(x * x, axis=-1, keepdims=True) + EPS)) * g


def _rms_mod(x, g, sc, sh):
    return _rms(x, g) * (1.0 + sc) + sh


def _norm_mod_body(x_ref, g_ref, sc_ref, sh_ref, o_ref):
    o_ref[...] = _rms_mod(x_ref[...], g_ref[...], sc_ref[...], sh_ref[...]).astype(o_ref.dtype)


def _norm_mod(x, g, sc, sh):
    m, d = x.shape
    tm = 512
    row = pl.BlockSpec((1, d), lambda i: (0, 0))
    return pl.pallas_call(
        _norm_mod_body,
        grid=(m // tm,),
        in_specs=[pl.BlockSpec((tm, d), lambda i: (i, 0)), row, row, row],
        out_specs=pl.BlockSpec((tm, d), lambda i: (i, 0)),
        out_shape=jax.ShapeDtypeStruct((m, d), BF16),
        compiler_params=_params(40, 1),
        name="norm_mod",
    )(x, g, sc, sh)


def _final_norm_body(x_ref, g_ref, o_ref):
    o_ref[...] = _rms(x_ref[...], g_ref[...])


def _final_norm(x, g):
    m, d = x.shape
    tm = 512
    return pl.pallas_call(
        _final_norm_body,
        grid=(m // tm,),
        in_specs=[pl.BlockSpec((tm, d), lambda i: (i, 0)), pl.BlockSpec((1, d), lambda i: (0, 0))],
        out_specs=pl.BlockSpec((tm, d), lambda i: (i, 0)),
        out_shape=jax.ShapeDtypeStruct((m, d), F32),
        compiler_params=_params(40, 1),
        name="final_norm",
    )(x, g)


def _w_spec(w, lead, k, tn, off):
    assert w.shape[len(lead)] == k
    return pl.BlockSpec((None,) * len(lead) + (k, tn), lambda n, m: (*lead, 0, n + off))


def _row_spec(a, lead, tn, off=0):
    return pl.BlockSpec((None,) * len(lead) + (1, tn), lambda n, m: (*lead, 0, n + off))


def _colmm(x, weights, raw_inputs, outs, epilogue, *, tm, tn, nt, scratch=(), vmem_mib, name,
           make_side=None):
    m_rows, k = x.shape
    nw, nr, no = len(weights), len(raw_inputs), len(outs)
    side = make_side(nt, m_rows // tm) if make_side else None
    side_in, side_out = side.specs() if side else ([], None)
    ns = len(side_in)

    def body(*refs):
        x_ref = refs[0]
        w_refs = refs[1:1 + nw]
        r_refs = refs[1 + nw:1 + nw + nr]
        side_refs = refs[1 + nw + nr:1 + nw + nr + ns]
        o_refs = refs[1 + nw + nr + ns:1 + nw + nr + ns + no]
        rest = refs[1 + nw + nr + ns + no:]
        side_o, rest = (rest[0], rest[1:]) if side else (None, rest)
        wb_refs, extra = rest[:nw], rest[nw:]

        @pl.when(pl.program_id(1) == 0)
        def _():
            for w_ref, wb in zip(w_refs, wb_refs):
                wb[...] = w_ref[...].astype(BF16)

        xv = x_ref[...]
        accs = [jnp.dot(xv, wb[...], preferred_element_type=F32) for wb in wb_refs]
        epilogue(accs, r_refs, o_refs, extra)
        if side:
            side.run(*side_refs, side_o)

    in_specs = [pl.BlockSpec((tm, k), lambda n, m: (m, 0))]
    in_specs += [_w_spec(w, lead, k, tn, off) for (w, lead, off) in weights]
    in_specs += [spec for (_, spec) in raw_inputs]
    return pl.pallas_call(
        body,
        grid=(nt, m_rows // tm),
        in_specs=in_specs + side_in,
        out_specs=[spec for (_, spec) in outs] + ([side_out] if side else []),
        out_shape=[sds for (sds, _) in outs] + ([side.out_shape] if side else []),
        scratch_shapes=[pltpu.VMEM((k, tn), BF16) for _ in weights] + list(scratch),
        compiler_params=_params(vmem_mib, 2),
        name=name,
    )(x, *[w for (w, _, _) in weights], *[a for (a, _) in raw_inputs], *(side.arrays if side else ()))


def _tile_spec(tm, tn):
    return pl.BlockSpec((tm, tn), lambda n, m: (m, n))


def _mm_residual(a, w, lead, x_res, gate, *, tm, tn, vmem_mib, name):
    m_rows = a.shape[0]
    n_cols = x_res.shape[1]
    raw = [(gate, _row_spec(gate, (), tn)), (x_res, _tile_spec(tm, tn))]

    def epilogue(accs, r_refs, o_refs, _):
        o_refs[0][...] = r_refs[1][...] + accs[0] * r_refs[0][...]

    out = _colmm(a, [(w, lead, 0)], raw,
                 [(jax.ShapeDtypeStruct((m_rows, n_cols), F32), _tile_spec(tm, tn))],
                 epilogue, tm=tm, tn=tn, nt=n_cols // tn, vmem_mib=vmem_mib, name=name)
    return out[0]


def _mixer_out_norm(a, w, j, x_res, gate, g, sc, sh):
    m_rows, k = a.shape
    d = x_res.shape[1]
    tm = 256

    def body(a_ref, w_ref, gate_ref, x_ref, g_ref, sc_ref, sh_ref, xo_ref, ho_ref, wb):
        @pl.when(pl.program_id(0) == 0)
        def _():
            wb[...] = w_ref[...].astype(BF16)

        y = jnp.dot(a_ref[...], wb[...], preferred_element_type=F32)
        x_new = x_ref[...] + gate_ref[...] * y
        xo_ref[...] = x_new
        ho_ref[...] = _rms_mod(x_new, g_ref[...], sc_ref[...], sh_ref[...]).astype(BF16)

    row = pl.BlockSpec((1, d), lambda i: (0, 0))
    blk_in = pl.BlockSpec((tm, k), lambda i: (i, 0))
    blk_d = pl.BlockSpec((tm, d), lambda i: (i, 0))
    return pl.pallas_call(
        body,
        grid=(m_rows // tm,),
        in_specs=[blk_in,
                  pl.BlockSpec((None, k, d), lambda i: (j, 0, 0), pipeline_mode=pl.Buffered(1)),
                  row, blk_d, row, row, row],
        out_specs=[blk_d, blk_d],
        out_shape=[jax.ShapeDtypeStruct((m_rows, d), F32), jax.ShapeDtypeStruct((m_rows, d), BF16)],
        scratch_shapes=[pltpu.VMEM((k, d), BF16)],
        compiler_params=_params(48, 1),
        name="mixer_out",
    )(a, w, gate, x_res, g, sc, sh)


def _swiglu_up(h, w1, w3, lead, *, tm, tn, vmem_mib, name, make_side=None):
    m_rows = h.shape[0]
    n_cols = w1.shape[-1]

    def epilogue(accs, r_refs, o_refs, _):
        o_refs[0][...] = (jax.nn.silu(accs[0]) * accs[1]).astype(BF16)

    return _colmm(h, [(w1, lead, 0), (w3, lead, 0)], [],
                  [(jax.ShapeDtypeStruct((m_rows, n_cols), BF16), _tile_spec(tm, tn))],
                  epilogue, tm=tm, tn=tn, nt=n_cols // tn, vmem_mib=vmem_mib, name=name,
                  make_side=make_side)


def _pool_in(h, w_in, w_grp, scale, j):
    m_rows, k = h.shape
    n_groups = len(POOL_WINDOWS)
    tn = w_in.shape[-1] // n_groups
    tm = 1024
    assert all(w & (w - 1) == 0 for w in POOL_WINDOWS)
    halo = 2 * max(POOL_WINDOWS)

    def epilogue(accs, r_refs, o_refs, s_refs):
        z = accs[0]
        wg_ref, scale_ref = r_refs
        zext, buf_a, buf_b, wgb, pooled = s_refs
        n, m = pl.program_id(0), pl.program_id(1)
        end = halo + tm

        @pl.when(m == 0)
        def _():
            zext[0:halo, :] = jnp.zeros((halo, tn), F32)
            wgb[...] = wg_ref[...].astype(BF16)

        zext[halo:end, :] = z
        t = m * tm + lax.broadcasted_iota(jnp.int32, (tm, tn), 0)
        for gi, win in enumerate(POOL_WINDOWS):
            @pl.when(n == gi)
            def _(win=win):
                levels = win.bit_length() - 1
                src = zext
                for lv in range(levels - 1):
                    dst = buf_a if lv % 2 == 0 else buf_b
                    lo, span = halo - SUBLANES * (levels - 1 - lv), 1 << lv
                    dst[lo:end, :] = src[lo:end, :] + src[lo - span:end - span, :]
                    src = dst
                span = win // 2
                wsum = src[halo:end, :] + src[halo - span:end - span, :]
                count = jnp.minimum(t + 1, win).astype(F32)
                pooled[...] = (wsum / count - z).astype(BF16)
        zext[0:halo, :] = zext[tm:end, :]
        mixed = jnp.dot(pooled[...], wgb[...], preferred_element_type=F32)
        o_refs[0][...] = (mixed * scale_ref[...]).astype(BF16)

    raw = [(w_grp, pl.BlockSpec((None, None, tn, tn), lambda n, m: (j, n, 0, 0))),
           (scale, _row_spec(scale, (), tn))]
    out = _colmm(h, [(w_in, (j,), 0)], raw,
                 [(jax.ShapeDtypeStruct((m_rows, n_groups * tn), BF16), _tile_spec(tm, tn))],
                 epilogue, tm=tm, tn=tn, nt=n_groups,
                 scratch=[pltpu.VMEM((tm + halo, tn), F32)] * 3
                 + [pltpu.VMEM((tn, tn), BF16), pltpu.VMEM((tm, tn), BF16)],
                 vmem_mib=48, name="pool_in")
    return out[0]


def _glu_in(h, w_in, b_in, j, make_side=None):
    m_rows, k = h.shape
    dm = w_in.shape[-1] // 2
    tm, tn = 1024, 512
    nt = dm // tn

    def epilogue(accs, r_refs, o_refs, _):
        a = accs[0] + r_refs[0][...]
        g = accs[1] + r_refs[1][...]
        o_refs[0][...] = (a * jax.nn.sigmoid(g)).astype(BF16)

    raw = [(b_in, _row_spec(b_in, (), tn)), (b_in, _row_spec(b_in, (), tn, off=nt))]
    out = _colmm(h, [(w_in, (j,), 0), (w_in, (j,), nt)], raw,
                 [(jax.ShapeDtypeStruct((m_rows, dm), BF16), _tile_spec(tm, tn))],
                 epilogue, tm=tm, tn=tn, nt=nt, vmem_mib=52, name="glu_in", make_side=make_side)
    return out


def _conv_ln_body(z_ref, w_ref, cb_ref, g_ref, b_ref, o_ref, zsh, wbc, conv, *, tm, halo, rows, cols):
    d = z_ref.shape[1]
    sl = SUBLANES

    @pl.when(pl.program_id(0) == 0)
    def _():
        zsh[0, 0:halo, :] = jnp.zeros((halo, d), F32)
        for kk in range(CONV_WIDTH):
            wbc[kk] = jnp.broadcast_to(w_ref[kk:kk + 1, :], (sl, d))

    zsh[0, halo:halo + tm, :] = z_ref[...].astype(F32)
    for b in range(1, sl):
        zsh[b, sl:halo + tm, :] = zsh[0, sl - b:halo + tm - b, :]

    for c0 in range(0, d, cols):
        cs = slice(c0, c0 + cols)
        for r0 in range(0, tm, rows):
            acc = jnp.zeros((rows, cols), F32)
            for kk in range(CONV_WIDTH):
                a, b = divmod(CONV_WIDTH - 1 - kk, sl)
                start = halo + r0 - sl * a
                wv = jnp.tile(wbc[kk, :, cs], (rows // sl, 1))
                acc = acc + wv * zsh[b, start:start + rows, cs]
            conv[r0:r0 + rows, cs] = acc + cb_ref[:, cs]
    zsh[0, 0:halo, :] = zsh[0, tm:tm + halo, :]

    y = conv[...]
    mu = jnp.mean(y, axis=-1, keepdims=True)
    var = jnp.mean(jnp.square(y - mu), axis=-1, keepdims=True)
    yn = (y - mu) * lax.rsqrt(var + EPS) * g_ref[...] + b_ref[...]
    o_ref[...] = jax.nn.silu(yn).astype(o_ref.dtype)


def _conv_ln(z, dw_w, dw_b, ln_g, ln_b, j):
    m_rows, d = z.shape
    tm, halo = 256, 32
    row = pl.BlockSpec((1, d), lambda i: (0, 0))
    return pl.pallas_call(
        functools.partial(_conv_ln_body, tm=tm, halo=halo, rows=64, cols=512),
        grid=(m_rows // tm,),
        in_specs=[pl.BlockSpec((tm, d), lambda i: (i, 0)),
                  pl.BlockSpec((None, CONV_WIDTH, d), lambda i: (j, 0, 0)), row, row, row],
        out_specs=pl.BlockSpec((tm, d), lambda i: (i, 0)),
        out_shape=jax.ShapeDtypeStruct((m_rows, d), BF16),
        scratch_shapes=[pltpu.VMEM((SUBLANES, tm + halo, d), F32),
                        pltpu.VMEM((CONV_WIDTH, SUBLANES, d), F32), pltpu.VMEM((tm, d), F32)],
        compiler_params=_params(40, 1),
        name="conv_ln",
    )(z, dw_w, dw_b, ln_g, ln_b)


def _gelu_in(h, w_in, b_in, j):
    m_rows, k = h.shape
    n_cols = w_in.shape[-1]
    tm, tn = 512, 1024

    def epilogue(accs, r_refs, o_refs, _):
        a = accs[0] + r_refs[0][...]
        o_refs[0][...] = (0.5 * a * (1.0 + lax.erf(a * math.sqrt(0.5)))).astype(BF16)

    out = _colmm(h, [(w_in, (j,), 0)], [(b_in, _row_spec(b_in, (), tn))],
                 [(jax.ShapeDtypeStruct((m_rows, n_cols), BF16), _tile_spec(tm, tn))],
                 epilogue, tm=tm, tn=tn, nt=n_cols // tn, vmem_mib=48, name="gelu_in")
    return out[0]


def _sgu_gate_body(u_ref, v_ref, g_ref, b_ref, ws_ref, bst_ref, o_ref, *, tm):
    v = v_ref[...].astype(F32)
    mu = jnp.mean(v, axis=-1, keepdims=True)
    var = jnp.mean(jnp.square(v - mu), axis=-1, keepdims=True)
    vn = ((v - mu) * lax.rsqrt(var + EPS) * g_ref[...] + b_ref[...]).astype(BF16)
    hd = v.shape[1] // SGU_HEADS
    tri = (lax.broadcasted_iota(jnp.int32, (SGU_CHUNK, SGU_CHUNK), 0)
           >= lax.broadcasted_iota(jnp.int32, (SGU_CHUNK, SGU_CHUNK), 1))
    for hh in range(SGU_HEADS):
        wc = jnp.where(tri, ws_ref[hh], 0.0).astype(BF16)
        bias = bst_ref[:, hh:hh + 1]
        for ck in range(tm // SGU_CHUNK):
            rs = slice(ck * SGU_CHUNK, (ck + 1) * SGU_CHUNK)
            cs = slice(hh * hd, (hh + 1) * hd)
            sv = jnp.dot(wc, vn[rs, cs], preferred_element_type=F32) + bias
            o_ref[rs, cs] = (u_ref[rs, cs].astype(F32) * sv).astype(o_ref.dtype)


def _sgu_gate(a, ln_g, ln_b, w_s, b_s_t, j):
    m_rows = a.shape[0]
    d = a.shape[1] // 2
    tm = 256
    row = pl.BlockSpec((1, d), lambda i: (0, 0))
    return pl.pallas_call(
        functools.partial(_sgu_gate_body, tm=tm),
        grid=(m_rows // tm,),
        in_specs=[pl.BlockSpec((tm, d), lambda i: (i, 0)), pl.BlockSpec((tm, d), lambda i: (i, 1)),
                  row, row,
                  pl.BlockSpec((None, SGU_HEADS, SGU_CHUNK, SGU_CHUNK), lambda i: (j, 0, 0, 0)),
                  pl.BlockSpec((SGU_CHUNK, SGU_HEADS), lambda i: (0, 0))],
        out_specs=pl.BlockSpec((tm, d), lambda i: (i, 0)),
        out_shape=jax.ShapeDtypeStruct((m_rows, d), BF16),
        compiler_params=_params(40, 1),
        name="sgu_gate",
    )(a, a, ln_g, ln_b, w_s, b_s_t)


def _sconv_in(h, w_in, conv_w, j, make_side=None):
    m_rows, k = h.shape
    dm = w_in.shape[-1] // 3
    tm, tn = 1024, 256
    nt = dm // tn
    halo = SUBLANES

    def epilogue(accs, r_refs, o_refs, s_refs):
        bg, cg, z = accs
        cw_ref = r_refs[0]
        ext = s_refs[0]

        @pl.when(pl.program_id(1) == 0)
        def _():
            ext[0:halo, :] = jnp.zeros((halo, tn), F32)

        cz = cg * z
        ext[halo:halo + tm, :] = cz
        conv = cw_ref[SHORT_CONV_WIDTH - 1:SHORT_CONV_WIDTH, :] * cz
        for back in range(1, SHORT_CONV_WIDTH):
            tap = SHORT_CONV_WIDTH - 1 - back
            conv = conv + cw_ref[tap:tap + 1, :] * ext[halo - back:halo - back + tm, :]
        ext[0:halo, :] = ext[tm:tm + halo, :]
        o_refs[0][...] = (bg * conv).astype(BF16)

    raw = [(conv_w, pl.BlockSpec((None, SHORT_CONV_WIDTH, tn), lambda n, m: (j, 0, n)))]
    out = _colmm(h, [(w_in, (j,), 0), (w_in, (j,), nt), (w_in, (j,), 2 * nt)], raw,
                 [(jax.ShapeDtypeStruct((m_rows, dm), BF16), _tile_spec(tm, tn))],
                 epilogue, tm=tm, tn=tn, nt=nt,
                 scratch=[pltpu.VMEM((tm + halo, tn), F32)], vmem_mib=52, name="sconv_in",
                 make_side=make_side)
    return out


def _router_body(h_ref, w_ref, info_ref, cum_ref, carry):
    logits = jnp.dot(h_ref[...], w_ref[...].astype(BF16), preferred_element_type=F32)
    lane = lax.broadcasted_iota(jnp.int32, logits.shape, 1)
    neg = jnp.float32(-jnp.inf)
    lg = jnp.where(lane < N_EXPERTS, logits, neg)
    v1 = jnp.max(lg, axis=-1, keepdims=True)
    lane_f = lane.astype(F32)
    i1 = jnp.min(jnp.where(lg == v1, lane_f, float(LANES)), axis=-1, keepdims=True)
    lg2 = jnp.where(lane_f == i1, neg, lg)
    v2 = jnp.max(lg2, axis=-1, keepdims=True)
    i2 = jnp.min(jnp.where(lg2 == v2, lane_f, float(LANES)), axis=-1, keepdims=True)
    e2 = jnp.exp(v2 - v1)
    den = 1.0 + e2
    w1, w2 = 1.0 / den, e2 / den

    @pl.when(pl.program_id(0) == 0)
    def _():
        carry[...] = jnp.zeros(carry.shape, F32)

    tm = logits.shape[0]
    cnt = jnp.where(lane_f == i1, 1.0, 0.0) + jnp.where(lane_f == i2, 1.0, 0.0)
    strict = jnp.where(lax.broadcasted_iota(jnp.int32, (tm, tm), 1)
                       < lax.broadcasted_iota(jnp.int32, (tm, tm), 0), 1.0, 0.0).astype(BF16)
    before = jnp.dot(strict, cnt.astype(BF16), preferred_element_type=F32) + carry[...]
    r1 = jnp.sum(jnp.where(lane_f == i1, before, 0.0), axis=-1, keepdims=True)
    r2 = jnp.sum(jnp.where(lane_f == i2, before, 0.0), axis=-1, keepdims=True)
    fields = (i1, i2, w1, w2, r1, r2)
    info = jnp.zeros(logits.shape, F32)
    for q, val in enumerate(fields):
        info = jnp.where(lane == q, val, info)
    info_ref[...] = info[:, :info_ref.shape[1]]
    total = carry[...] + jnp.sum(cnt, axis=0, keepdims=True)
    carry[...] = total
    cum_ref[...] = total


def _router(h, w_router_padded):
    m_rows, k = h.shape
    tm = MOE_SUB
    return pl.pallas_call(
        _router_body,
        grid=(m_rows // tm,),
        in_specs=[pl.BlockSpec((tm, k), lambda i: (i, 0)), pl.BlockSpec((k, LANES), lambda i: (0, 0))],
        out_specs=[pl.BlockSpec((tm, SUBLANES), lambda i: (i, 0)),
                   pl.BlockSpec((None, 1, LANES), lambda i: (i, 0, 0))],
        out_shape=[jax.ShapeDtypeStruct((m_rows, SUBLANES), F32),
                   jax.ShapeDtypeStruct((m_rows // tm, 1, LANES), F32)],
        scratch_shapes=[pltpu.VMEM((1, LANES), F32)],
        compiler_params=_params(40, 1),
        name="router",
    )(h, w_router_padded)


def _dispatch_plan(info, cum):
    rows, sub, tok, n_exp = MOE_ROWS, MOE_SUB, TOKEN_BLOCK, N_EXPERTS
    per = rows // sub
    n_tok = info.shape[0]
    tbn = n_tok // tok
    nb_max = (2 * n_tok) // rows + n_exp
    nq = per * nb_max
    s_max = nq + n_exp * (tbn - 1)
    i32 = jnp.int32
    ids = info.astype(i32)
    i1, i2, r1, r2 = ids[:, 0], ids[:, 1], ids[:, 4], ids[:, 5]
    cb_fine = jnp.concatenate([jnp.zeros((1, n_exp), i32), cum[:, 0, :n_exp].astype(i32)])
    cb = cb_fine[::tok // sub]
    counts = cb[-1]
    nblk = (counts + rows - 1) // rows
    blk_end = jnp.cumsum(nblk)
    blk_off = blk_end - nblk
    nb = blk_end[-1]
    experts = jnp.arange(n_exp, dtype=i32)[None, :]
    row_off = (blk_off * rows)[None, :]
    slot1 = jnp.sum(jnp.where(i1[:, None] == experts, row_off, 0), axis=1) + r1
    slot2 = jnp.sum(jnp.where(i2[:, None] == experts, row_off, 0), axis=1) + r2

    b_idx = jnp.arange(nb_max, dtype=i32)
    b_used = b_idx < nb
    b_clamped = jnp.minimum(b_idx, nb - 1)
    blk_e = jnp.minimum(jnp.sum(b_clamped[:, None] >= blk_end[None, :], axis=1), n_exp - 1).astype(i32)
    prev_e = jnp.concatenate([jnp.full((1,), -1, i32), blk_e[:-1]])
    tail_rows = counts[blk_e] - (blk_end[blk_e] - 1 - blk_off[blk_e]) * rows
    half = (b_clamped == blk_end[blk_e] - 1) & (tail_rows <= rows // 2)
    blk_flag = (b_used.astype(i32) + 2 * (b_used & (blk_e != prev_e)).astype(i32)
                + 4 * (b_used & half).astype(i32))

    q = jnp.arange(nq, dtype=i32)
    e_q = jnp.minimum(jnp.sum(q[:, None] >= per * blk_end[None, :], axis=1), n_exp - 1)
    k_q = (q - per * blk_off[e_q])[:, None]
    lo = jnp.maximum(k_q * sub, cb[:-1].T[e_q])
    hi = jnp.minimum((k_q + 1) * sub, cb[1:].T[e_q])
    inter = (lo < hi) & (q < per * nb)[:, None]
    first_tb = jnp.arange(tbn, dtype=i32)[None, :] == 0
    touch = inter | (~jnp.any(inter, axis=1, keepdims=True) & first_tb)
    s_idx = jnp.arange(s_max, dtype=i32)

    def step_lists(mask):
        in_row = jnp.cumsum(mask.astype(i32), axis=1)
        row_end = jnp.cumsum(in_row[:, -1])
        n_steps = row_end[-1]
        s_clamped = jnp.minimum(s_idx, n_steps - 1)
        major = jnp.sum(row_end[None, :] <= s_clamped[:, None], axis=1).astype(i32)
        within = s_clamped - (row_end - in_row[:, -1])[major]
        minor = jnp.sum(in_row[major] <= within[:, None], axis=1).astype(i32)
        s_valid = s_idx < n_steps
        prev_m = jnp.concatenate([jnp.full((1,), -1, i32), major[:-1]])
        next_m = jnp.concatenate([major[1:], jnp.full((1,), -1, i32)])
        first = s_valid & (prev_m != major)
        last = s_valid & ((next_m != major) | (s_idx == n_steps - 1))
        return major, minor, s_valid, 2 * first.astype(i32) + 4 * last.astype(i32)

    g_q, g_tb, g_valid, g_edge = step_lists(touch)
    g_flag = (g_valid & inter[g_q, g_tb]).astype(i32) + g_edge

    c_q0 = jnp.minimum((row_off + cb_fine[:-1]) // sub, nq - 2).astype(i32).ravel()
    c_lo = (blk_off * rows).astype(i32)
    c_hi = (blk_end * rows).astype(i32)
    return dict(slot1=slot1, slot2=slot2, w1=info[:, 2], w2=info[:, 3],
                blk_e=blk_e, blk_row=b_clamped, blk_flag=blk_flag,
                g_q=g_q, g_tb=g_tb, g_flag=g_flag, c_q0=c_q0, c_lo=c_lo, c_hi=c_hi,
                nb_max=nb_max, s_max=s_max)


def _moe_gather(h, plan):
    n_tok, d = h.shape
    sub, tok = MOE_SUB, TOKEN_BLOCK
    p_rows = plan["nb_max"] * MOE_ROWS

    def body(q_ref, tb_ref, fl_ref, h_ref, s1_ref, s2_ref, w1_ref, w2_ref, hg_ref, ws_ref, acc, wacc):
        s = pl.program_id(0)
        flag = fl_ref[s]

        @pl.when((flag & 2) != 0)
        def _():
            acc[...] = jnp.zeros(acc.shape, F32)
            wacc[...] = jnp.zeros(wacc.shape, F32)

        @pl.when((flag & 1) != 0)
        def _():
            slot = q_ref[s] * sub + lax.broadcasted_iota(jnp.int32, (sub, tok), 0)
            d1 = s1_ref[...] == slot
            d2 = s2_ref[...] == slot
            sel = jnp.where(d1, 1.0, jnp.where(d2, 1.0, 0.0)).astype(BF16)
            t0 = pl.multiple_of(tb_ref[s] * tok, tok)
            acc[...] += jnp.dot(sel, h_ref[pl.ds(t0, tok), :], preferred_element_type=F32)
            wacc[...] += jnp.sum(jnp.where(d1, w1_ref[...], 0.0) + jnp.where(d2, w2_ref[...], 0.0),
                                 axis=1, keepdims=True)

        @pl.when((flag & 4) != 0)
        def _():
            hg_ref[...] = acc[...].astype(BF16)
            ws_ref[...] = wacc[...]

    tok_row = pl.BlockSpec((None, 1, tok), lambda s, q, tb, fl: (tb[s], 0, 0))
    grid_spec = pltpu.PrefetchScalarGridSpec(
        num_scalar_prefetch=3,
        grid=(plan["s_max"],),
        in_specs=[pl.BlockSpec((n_tok, d), lambda s, q, tb, fl: (0, 0), pipeline_mode=pl.Buffered(1)),
                  tok_row, tok_row, tok_row, tok_row],
        out_specs=[pl.BlockSpec((sub, d), lambda s, q, tb, fl: (q[s], 0)),
                   pl.BlockSpec((sub, 1), lambda s, q, tb, fl: (q[s], 0))],
        scratch_shapes=[pltpu.VMEM((sub, d), F32), pltpu.VMEM((sub, 1), F32)])
    rows = lambda a: a.reshape(n_tok // tok, 1, tok)
    return pl.pallas_call(
        body, grid_spec=grid_spec,
        out_shape=[jax.ShapeDtypeStruct((p_rows, d), BF16), jax.ShapeDtypeStruct((p_rows, 1), F32)],
        compiler_params=_params(48, 1), name="moe_gather",
    )(plan["g_q"], plan["g_tb"], plan["g_flag"], h,
      rows(plan["slot1"]), rows(plan["slot2"]), rows(plan["w1"]), rows(plan["w2"]))


def _grouped_mm(x, weights, lead, plan, row_inputs, out_cols, epilogue, *, tn, vmem_mib, name):
    p_rows, k = x.shape
    blk = MOE_ROWS
    half = blk // 2
    nw, nr = len(weights), len(row_inputs)

    def body(be_ref, br_ref, fl_ref, x_ref, *refs):
        w_refs, r_refs = refs[:nw], refs[nw:nw + nr]
        o_ref = refs[nw + nr]
        wb_refs = refs[nw + nr + 1:]
        flag = fl_ref[pl.program_id(1)]

        @pl.when((flag & 2) != 0)
        def _():
            for w_ref, wb in zip(w_refs, wb_refs):
                wb[...] = w_ref[...].astype(BF16)

        def compute(n_rows):
            xv = x_ref[0:n_rows, :]
            accs = [jnp.dot(xv, wb[...], preferred_element_type=F32) for wb in wb_refs]
            o_ref[0:n_rows, :] = epilogue(accs, [r[0:n_rows, :] for r in r_refs]).astype(BF16)

        @pl.when((flag & 5) == 1)
        def _():
            compute(blk)

        @pl.when((flag & 5) == 5)
        def _():
            compute(half)
            o_ref[half:blk, :] = jnp.zeros((blk - half, tn), BF16)

        @pl.when((flag & 1) == 0)
        def _():
            o_ref[...] = jnp.zeros((blk, tn), BF16)

    w_spec = pl.BlockSpec((None,) * (len(lead) + 1) + (k, tn),
                          lambda n, b, be, br, fl: (*lead, be[b], 0, n))
    grid_spec = pltpu.PrefetchScalarGridSpec(
        num_scalar_prefetch=3,
        grid=(out_cols // tn, plan["nb_max"]),
        in_specs=[pl.BlockSpec((blk, k), lambda n, b, be, br, fl: (br[b], 0))]
        + [w_spec] * nw
        + [pl.BlockSpec((blk, a.shape[1]), lambda n, b, be, br, fl: (br[b], 0)) for a in row_inputs],
        out_specs=pl.BlockSpec((blk, tn), lambda n, b, be, br, fl: (b, n)),
        scratch_shapes=[pltpu.VMEM((k, tn), BF16) for _ in weights])
    return pl.pallas_call(
        body, grid_spec=grid_spec,
        out_shape=jax.ShapeDtypeStruct((p_rows, out_cols), BF16),
        compiler_params=_params(vmem_mib, 2), name=name,
    )(plan["blk_e"], plan["blk_row"], plan["blk_flag"], x, *weights, *row_inputs)


def _moe_combine(yw, plan, x_res, gate, norm):
    n_tok, d = x_res.shape
    sub = MOE_SUB
    n_win = 2 * N_EXPERTS
    n_norm = len(norm)

    def body(q0_ref, lo_ref, hi_ref, *refs):
        y_refs = refs[:n_win]
        s1_ref, s2_ref, x_ref, g_ref = refs[n_win:n_win + 4]
        norm_refs = refs[n_win + 4:n_win + 4 + n_norm]
        o_refs = refs[n_win + 4 + n_norm:]
        t = pl.program_id(0)
        s1, s2 = s1_ref[...], s2_ref[...]
        col = lax.broadcasted_iota(jnp.int32, (sub, sub), 1)
        acc = None
        for e in range(N_EXPERTS):
            for j in range(2):
                slot = (q0_ref[t * N_EXPERTS + e] + j) * sub + col
                inside = (slot - lo_ref[e]).astype(jnp.uint32) < (hi_ref[e] - lo_ref[e]).astype(jnp.uint32)
                slot = jnp.where(inside, slot, -1)
                sel = jnp.where(s1 == slot, 1.0, jnp.where(s2 == slot, 1.0, 0.0)).astype(BF16)
                part = jnp.dot(sel, y_refs[2 * e + j][...], preferred_element_type=F32)
                acc = part if acc is None else acc + part
        x_new = x_ref[...] + g_ref[...] * acc
        if n_norm == 1:
            o_refs[0][...] = _rms(x_new, norm_refs[0][...])
        else:
            o_refs[0][...] = x_new
            o_refs[1][...] = _rms_mod(x_new, *[r[...] for r in norm_refs]).astype(BF16)

    def window_spec(e, j):
        return pl.BlockSpec((sub, d), lambda t, q0, lo, hi: (q0[t * N_EXPERTS + e] + j, 0))

    tok_col = pl.BlockSpec((sub, 1), lambda t, q0, lo, hi: (t, 0))
    tok_blk = pl.BlockSpec((sub, d), lambda t, q0, lo, hi: (t, 0))
    row = pl.BlockSpec((1, d), lambda t, q0, lo, hi: (0, 0))
    out_shape = [jax.ShapeDtypeStruct((n_tok, d), F32)]
    if n_norm > 1:
        out_shape.append(jax.ShapeDtypeStruct((n_tok, d), BF16))
    grid_spec = pltpu.PrefetchScalarGridSpec(
        num_scalar_prefetch=3,
        grid=(n_tok // sub,),
        in_specs=[window_spec(e, j) for e in range(N_EXPERTS) for j in range(2)]
        + [tok_col, tok_col, tok_blk, row] + [row] * n_norm,
        out_specs=[tok_blk] * len(out_shape))
    col_of = lambda a: a.reshape(n_tok, 1)
    return pl.pallas_call(
        body, grid_spec=grid_spec,
        out_shape=out_shape,
        compiler_params=_params(56, 1), name="moe_combine",
    )(plan["c_q0"], plan["c_lo"], plan["c_hi"], *([yw] * n_win),
      col_of(plan["slot1"]), col_of(plan["slot2"]), x_res, gate, *norm)


def _moe(h, x_res, gate, w_router, w1, w3, w2, kk, norm):
    w_r = jnp.pad(w_router[kk], ((0, 0), (0, LANES - N_EXPERTS)))
    info, cum = _router(h, w_r)
    plan = _dispatch_plan(info, cum)
    hg, w_slot = _moe_gather(h, plan)
    hid = _grouped_mm(hg, [w1, w3], (kk,), plan, [], w1.shape[-1],
                      lambda accs, _: jax.nn.silu(accs[0]) * accs[1],
                      tn=1024, vmem_mib=58, name="moe_up")
    yw = _grouped_mm(hid, [w2], (kk,), plan, [w_slot], w2.shape[-1],
                     lambda accs, rows: accs[0] * rows[0],
                     tn=1024, vmem_mib=58, name="moe_down")
    return _moe_combine(yw, plan, x_res, gate, norm)


def kernel(x, c, ada_w, ada_b, norm1_g, norm2_g, pool_w_in, pool_w_grp, pool_scale, pool_w_out, conv_w_in, conv_b_in, conv_dw_w, conv_dw_b, conv_ln_g, conv_ln_b, conv_w_out, sgu_w_in, sgu_b_in, sgu_ln_g, sgu_ln_b, sgu_w_s, sgu_b_s, sgu_w_out, sconv_w_in, sconv_w, sconv_w_out, ffn_w1, ffn_w3, ffn_w2, moe_router, moe_w1, moe_w3, moe_w2, final_g):
    batch, seq, d = x.shape
    assert batch == 1
    depth = ada_w.shape[0]
    xs = x.reshape(seq, d)
    c_col = c.reshape(d, 1)
    ada_b3 = ada_b.reshape(depth, 1, ada_b.shape[-1])
    split = lambda mod: [mod[:, q * d:(q + 1) * d] for q in range(6)]

    def side_for(layer):
        if layer >= depth:
            return None
        return lambda outer, inner: _AdaSide(c_col, ada_w, ada_b3, layer, outer, inner)

    mod_next = _ada_mod(c_col, ada_w, ada_b3, 0)
    final_g = final_g.reshape(1, d)
    h = None
    out = None
    for i in range(depth):
        sh1, sc1, g1, sh2, sc2, g2 = split(mod_next)

        if h is None:
            h = _norm_mod(xs, norm1_g[i:i + 1], sc1, sh1)
        mixer, j = i % 4, i // 4
        mixer_side = side_for(i + 1) if i % 2 == 1 else None
        side_out = []
        if mixer == 0:
            a = _pool_in(h, pool_w_in, pool_w_grp, pool_scale[j:j + 1], j)
            w_out = pool_w_out
        elif mixer == 1:
            z, *side_out = _glu_in(h, conv_w_in, conv_b_in[j:j + 1], j, mixer_side)
            a = _conv_ln(z, conv_dw_w, conv_dw_b[j:j + 1], conv_ln_g[j:j + 1], conv_ln_b[j:j + 1], j)
            w_out = conv_w_out
        elif mixer == 2:
            a = _gelu_in(h, sgu_w_in, sgu_b_in[j:j + 1], j)
            a = _sgu_gate(a, sgu_ln_g[j:j + 1], sgu_ln_b[j:j + 1], sgu_w_s, sgu_b_s[j].T, j)
            w_out = sgu_w_out
        else:
            a, *side_out = _sconv_in(h, sconv_w_in, sconv_w, j, mixer_side)
            w_out = sconv_w_out
        xs, h = _mixer_out_norm(a, w_out, j, xs, g1, norm2_g[i:i + 1], sc2, sh2)

        kk = i // 2
        if i % 2 == 0:
            hid, *side_out = _swiglu_up(h, ffn_w1, ffn_w3, (kk,), tm=1024, tn=512, vmem_mib=52,
                                        name="ffn_up", make_side=side_for(i + 1))
            xs = _mm_residual(hid, ffn_w2, (kk,), xs, g2, tm=512, tn=512, vmem_mib=56, name="ffn_down")
            mod_next = side_out[0] if side_out else None
            h = None
        elif i + 1 < depth:
            mod_next = side_out[0]
            sh_n, sc_n = split(mod_next)[:2]
            xs, h = _moe(h, xs, g2, moe_router, moe_w1, moe_w3, moe_w2, kk,
                         (norm1_g[i + 1:i + 2], sc_n, sh_n))
        else:
            (out,) = _moe(h, xs, g2, moe_router, moe_w1, moe_w3, moe_w2, kk, (final_g,))
    if out is None:
        out = _final_norm(xs, final_g)
    return out.reshape(batch, seq, d)
```

```python
import functools
import math

import jax
import jax.numpy as jnp
from jax import lax
from jax.experimental import pallas as pl
from jax.experimental.pallas import tpu as pltpu

EPS = 1e-6
POOL_WINDOWS = (2, 4, 8, 16)
CONV_WIDTH = 31
SGU_CHUNK = 128
SGU_HEADS = 8
SHORT_CONV_WIDTH = 3
N_EXPERTS = 8
MOE_ROWS = 512
MOE_SUB = 256
TOKEN_BLOCK = 512
LANES = 128
SUBLANES = 8
MIB = 1024 * 1024

BF16 = jnp.bfloat16
F32 = jnp.float32


def _params(vmem_mib, ndims):
    return pltpu.CompilerParams(
        dimension_semantics=("arbitrary",) * ndims,
        vmem_limit_bytes=vmem_mib * MIB)


def _ada_chunk(c_ref, w_ref, b_ref):
    c_act = jax.nn.silu(c_ref[...])
    return jnp.sum(w_ref[...] * c_act, axis=0, keepdims=True) + b_ref[...]


def _ada_mod(c_col, ada_w, ada_b3, layer):
    depth, d, n = ada_w.shape
    tn = 1024

    def body(c_ref, w_ref, b_ref, o_ref):
        o_ref[...] = _ada_chunk(c_ref, w_ref, b_ref)

    return pl.pallas_call(
        body,
        grid=(n // tn,),
        in_specs=[
            pl.BlockSpec((d, 1), lambda j: (0, 0)),
            pl.BlockSpec((None, d, tn), lambda j: (layer, 0, j)),
            pl.BlockSpec((None, 1, tn), lambda j: (layer, 0, j)),
        ],
        out_specs=pl.BlockSpec((1, tn), lambda j: (0, j)),
        out_shape=jax.ShapeDtypeStruct((1, n), F32),
        compiler_params=_params(40, 1),
        name="ada_mod",
    )(c_col, ada_w, ada_b3)


class _AdaSide:
    def __init__(self, c_col, ada_w, ada_b3, layer, outer, inner):
        self.arrays = (c_col, ada_w, ada_b3)
        self.layer, self.inner = layer, inner
        _, self.d, self.n = ada_w.shape
        self.chunk = next(ch for ch in range(LANES, self.n + 1, LANES)
                          if self.n % ch == 0 and self.n // ch <= outer * inner)
        self.n_chunks = self.n // self.chunk
        self.out_shape = jax.ShapeDtypeStruct((1, self.n), F32)

    def _chunk(self, n, m):
        return jnp.minimum(n * self.inner + m, self.n_chunks - 1)

    def specs(self):
        d, layer, chunk = self.d, self.layer, self.chunk
        ins = [pl.BlockSpec((d, 1), lambda n, m, *_: (0, 0)),
               pl.BlockSpec((None, d, chunk), lambda n, m, *_: (layer, 0, self._chunk(n, m))),
               pl.BlockSpec((None, 1, chunk), lambda n, m, *_: (layer, 0, self._chunk(n, m)))]
        return ins, pl.BlockSpec((1, chunk), lambda n, m, *_: (0, self._chunk(n, m)))

    def run(self, c_ref, w_ref, b_ref, o_ref):
        o_ref[...] = _ada_chunk(c_ref, w_ref, b_ref)


def _rms(x, g):
    return (x * lax.rsqrt(jnp.mean(x * x, axis=-1, keepdims=True) + EPS)) * g


def _rms_mod(x, g, sc, sh):
    return _rms(x, g) * (1.0 + sc) + sh


def _norm_mod_body(x_ref, g_ref, sc_ref, sh_ref, o_ref):
    o_ref[...] = _rms_mod(x_ref[...], g_ref[...], sc_ref[...], sh_ref[...]).astype(o_ref.dtype)


def _norm_mod(x, g, sc, sh):
    m, d = x.shape
    tm = 512
    row = pl.BlockSpec((1, d), lambda i: (0, 0))
    return pl.pallas_call(
        _norm_mod_body,
        grid=(m // tm,),
        in_specs=[pl.BlockSpec((tm, d), lambda i: (i, 0)), row, row, row],
        out_specs=pl.BlockSpec((tm, d), lambda i: (i, 0)),
        out_shape=jax.ShapeDtypeStruct((m, d), BF16),
        compiler_params=_params(40, 1),
        name="norm_mod",
    )(x, g, sc, sh)


def _final_norm_body(x_ref, g_ref, o_ref):
    o_ref[...] = _rms(x_ref[...], g_ref[...])


def _final_norm(x, g):
    m, d = x.shape
    tm = 512
    return pl.pallas_call(
        _final_norm_body,
        grid=(m // tm,),
        in_specs=[pl.BlockSpec((tm, d), lambda i: (i, 0)), pl.BlockSpec((1, d), lambda i: (0, 0))],
        out_specs=pl.BlockSpec((tm, d), lambda i: (i, 0)),
        out_shape=jax.ShapeDtypeStruct((m, d), F32),
        compiler_params=_params(40, 1),
        name="final_norm",
    )(x, g)


def _w_spec(w, lead, k, tn, off):
    assert w.shape[len(lead)] == k
    return pl.BlockSpec((None,) * len(lead) + (k, tn), lambda n, m: (*lead, 0, n + off))


def _row_spec(a, lead, tn, off=0):
    return pl.BlockSpec((None,) * len(lead) + (1, tn), lambda n, m: (*lead, 0, n + off))


def _colmm(x, weights, raw_inputs, outs, epilogue, *, tm, tn, nt, scratch=(), vmem_mib, name,
           make_side=None):
    m_rows, k = x.shape
    nw, nr, no = len(weights), len(raw_inputs), len(outs)
    side = make_side(nt, m_rows // tm) if make_side else None
    side_in, side_out = side.specs() if side else ([], None)
    ns = len(side_in)

    def body(*refs):
        x_ref = refs[0]
        w_refs = refs[1:1 + nw]
        r_refs = refs[1 + nw:1 + nw + nr]
        side_refs = refs[1 + nw + nr:1 + nw + nr + ns]
        o_refs = refs[1 + nw + nr + ns:1 + nw + nr + ns + no]
        rest = refs[1 + nw + nr + ns + no:]
        side_o, rest = (rest[0], rest[1:]) if side else (None, rest)
        wb_refs, extra = rest[:nw], rest[nw:]

        @pl.when(pl.program_id(1) == 0)
        def _():
            for w_ref, wb in zip(w_refs, wb_refs):
                wb[...] = w_ref[...].astype(BF16)

        xv = x_ref[...]
        accs = [jnp.dot(xv, wb[...], preferred_element_type=F32) for wb in wb_refs]
        epilogue(accs, r_refs, o_refs, extra)
        if side:
            side.run(*side_refs, side_o)

    in_specs = [pl.BlockSpec((tm, k), lambda n, m: (m, 0))]
    in_specs += [_w_spec(w, lead, k, tn, off) for (w, lead, off) in weights]
    in_specs += [spec for (_, spec) in raw_inputs]
    return pl.pallas_call(
        body,
        grid=(nt, m_rows // tm),
        in_specs=in_specs + side_in,
        out_specs=[spec for (_, spec) in outs] + ([side_out] if side else []),
        out_shape=[sds for (sds, _) in outs] + ([side.out_shape] if side else []),
        scratch_shapes=[pltpu.VMEM((k, tn), BF16) for _ in weights] + list(scratch),
        compiler_params=_params(vmem_mib, 2),
        name=name,
    )(x, *[w for (w, _, _) in weights], *[a for (a, _) in raw_inputs], *(side.arrays if side else ()))


def _tile_spec(tm, tn):
    return pl.BlockSpec((tm, tn), lambda n, m: (m, n))


def _mm_residual(a, w, lead, x_res, gate, *, tm, tn, vmem_mib, name):
    m_rows = a.shape[0]
    n_cols = x_res.shape[1]
    raw = [(gate, _row_spec(gate, (), tn)), (x_res, _tile_spec(tm, tn))]

    def epilogue(accs, r_refs, o_refs, _):
        o_refs[0][...] = r_refs[1][...] + accs[0] * r_refs[0][...]

    out = _colmm(a, [(w, lead, 0)], raw,
                 [(jax.ShapeDtypeStruct((m_rows, n_cols), F32), _tile_spec(tm, tn))],
                 epilogue, tm=tm, tn=tn, nt=n_cols // tn, vmem_mib=vmem_mib, name=name)
    return out[0]


def _mixer_out_norm(a, w, j, x_res, gate, g, sc, sh):
    m_rows, k = a.shape
    d = x_res.shape[1]
    tm = 256

    def body(a_ref, w_ref, gate_ref, x_ref, g_ref, sc_ref, sh_ref, xo_ref, ho_ref, wb):
        @pl.when(pl.program_id(0) == 0)
        def _():
            wb[...] = w_ref[...].astype(BF16)

        y = jnp.dot(a_ref[...], wb[...], preferred_element_type=F32)
        x_new = x_ref[...] + gate_ref[...] * y
        xo_ref[...] = x_new
        ho_ref[...] = _rms_mod(x_new, g_ref[...], sc_ref[...], sh_ref[...]).astype(BF16)

    row = pl.BlockSpec((1, d), lambda i: (0, 0))
    blk_in = pl.BlockSpec((tm, k), lambda i: (i, 0))
    blk_d = pl.BlockSpec((tm, d), lambda i: (i, 0))
    return pl.pallas_call(
        body,
        grid=(m_rows // tm,),
        in_specs=[blk_in,
                  pl.BlockSpec((None, k, d), lambda i: (j, 0, 0), pipeline_mode=pl.Buffered(1)),
                  row, blk_d, row, row, row],
        out_specs=[blk_d, blk_d],
        out_shape=[jax.ShapeDtypeStruct((m_rows, d), F32), jax.ShapeDtypeStruct((m_rows, d), BF16)],
        scratch_shapes=[pltpu.VMEM((k, d), BF16)],
        compiler_params=_params(48, 1),
        name="mixer_out",
    )(a, w, gate, x_res, g, sc, sh)


def _swiglu_up(h, w1, w3, lead, *, tm, tn, vmem_mib, name, make_side=None):
    m_rows = h.shape[0]
    n_cols = w1.shape[-1]

    def epilogue(accs, r_refs, o_refs, _):
        o_refs[0][...] = (jax.nn.silu(accs[0]) * accs[1]).astype(BF16)

    return _colmm(h, [(w1, lead, 0), (w3, lead, 0)], [],
                  [(jax.ShapeDtypeStruct((m_rows, n_cols), BF16), _tile_spec(tm, tn))],
                  epilogue, tm=tm, tn=tn, nt=n_cols // tn, vmem_mib=vmem_mib, name=name,
                  make_side=make_side)


def _pool_in(h, w_in, w_grp, scale, j):
    m_rows, k = h.shape
    n_groups = len(POOL_WINDOWS)
    tn = w_in.shape[-1] // n_groups
    tm = 1024
    assert all(w & (w - 1) == 0 for w in POOL_WINDOWS)
    halo = 2 * max(POOL_WINDOWS)

    def epilogue(accs, r_refs, o_refs, s_refs):
        z = accs[0]
        wg_ref, scale_ref = r_refs
        zext, buf_a, buf_b, wgb, pooled = s_refs
        n, m = pl.program_id(0), pl.program_id(1)
        end = halo + tm

        @pl.when(m == 0)
        def _():
            zext[0:halo, :] = jnp.zeros((halo, tn), F32)
            wgb[...] = wg_ref[...].astype(BF16)

        zext[halo:end, :] = z
        t = m * tm + lax.broadcasted_iota(jnp.int32, (tm, tn), 0)
        for gi, win in enumerate(POOL_WINDOWS):
            @pl.when(n == gi)
            def _(win=win):
                levels = win.bit_length() - 1
                src = zext
                for lv in range(levels - 1):
                    dst = buf_a if lv % 2 == 0 else buf_b
                    lo, span = halo - SUBLANES * (levels - 1 - lv), 1 << lv
                    dst[lo:end, :] = src[lo:end, :] + src[lo - span:end - span, :]
                    src = dst
                span = win // 2
                wsum = src[halo:end, :] + src[halo - span:end - span, :]
                count = jnp.minimum(t + 1, win).astype(F32)
                pooled[...] = (wsum / count - z).astype(BF16)
        zext[0:halo, :] = zext[tm:end, :]
        mixed = jnp.dot(pooled[...], wgb[...], preferred_element_type=F32)
        o_refs[0][...] = (mixed * scale_ref[...]).astype(BF16)

    raw = [(w_grp, pl.BlockSpec((None, None, tn, tn), lambda n, m: (j, n, 0, 0))),
           (scale, _row_spec(scale, (), tn))]
    out = _colmm(h, [(w_in, (j,), 0)], raw,
                 [(jax.ShapeDtypeStruct((m_rows, n_groups * tn), BF16), _tile_spec(tm, tn))],
                 epilogue, tm=tm, tn=tn, nt=n_groups,
                 scratch=[pltpu.VMEM((tm + halo, tn), F32)] * 3
                 + [pltpu.VMEM((tn, tn), BF16), pltpu.VMEM((tm, tn), BF16)],
                 vmem_mib=48, name="pool_in")
    return out[0]


def _glu_in(h, w_in, b_in, j, make_side=None):
    m_rows, k = h.shape
    dm = w_in.shape[-1] // 2
    tm, tn = 1024, 512
    nt = dm // tn

    def epilogue(accs, r_refs, o_refs, _):
        a = accs[0] + r_refs[0][...]
        g = accs[1] + r_refs[1][...]
        o_refs[0][...] = (a * jax.nn.sigmoid(g)).astype(BF16)

    raw = [(b_in, _row_spec(b_in, (), tn)), (b_in, _row_spec(b_in, (), tn, off=nt))]
    out = _colmm(h, [(w_in, (j,), 0), (w_in, (j,), nt)], raw,
                 [(jax.ShapeDtypeStruct((m_rows, dm), BF16), _tile_spec(tm, tn))],
                 epilogue, tm=tm, tn=tn, nt=nt, vmem_mib=52, name="glu_in", make_side=make_side)
    return out


def _conv_ln_body(z_ref, w_ref, cb_ref, g_ref, b_ref, o_ref, zsh, wbc, conv, *, tm, halo, rows, cols):
    d = z_ref.shape[1]
    sl = SUBLANES

    @pl.when(pl.program_id(0) == 0)
    def _():
        zsh[0, 0:halo, :] = jnp.zeros((halo, d), F32)
        for kk in range(CONV_WIDTH):
            wbc[kk] = jnp.broadcast_to(w_ref[kk:kk + 1, :], (sl, d))

    zsh[0, halo:halo + tm, :] = z_ref[...].astype(F32)
    for b in range(1, sl):
        zsh[b, sl:halo + tm, :] = zsh[0, sl - b:halo + tm - b, :]

    for c0 in range(0, d, cols):
        cs = slice(c0, c0 + cols)
        for r0 in range(0, tm, rows):
            acc = jnp.zeros((rows, cols), F32)
            for kk in range(CONV_WIDTH):
                a, b = divmod(CONV_WIDTH - 1 - kk, sl)
                start = halo + r0 - sl * a
                wv = jnp.tile(wbc[kk, :, cs], (rows // sl, 1))
                acc = acc + wv * zsh[b, start:start + rows, cs]
            conv[r0:r0 + rows, cs] = acc + cb_ref[:, cs]
    zsh[0, 0:halo, :] = zsh[0, tm:tm + halo, :]

    y = conv[...]
    mu = jnp.mean(y, axis=-1, keepdims=True)
    var = jnp.mean(jnp.square(y - mu), axis=-1, keepdims=True)
    yn = (y - mu) * lax.rsqrt(var + EPS) * g_ref[...] + b_ref[...]
    o_ref[...] = jax.nn.silu(yn).astype(o_ref.dtype)


def _conv_ln(z, dw_w, dw_b, ln_g, ln_b, j):
    m_rows, d = z.shape
    tm, halo = 256, 32
    row = pl.BlockSpec((1, d), lambda i: (0, 0))
    return pl.pallas_call(
        functools.partial(_conv_ln_body, tm=tm, halo=halo, rows=64, cols=512),
        grid=(m_rows // tm,),
        in_specs=[pl.BlockSpec((tm, d), lambda i: (i, 0)),
                  pl.BlockSpec((None, CONV_WIDTH, d), lambda i: (j, 0, 0)), row, row, row],
        out_specs=pl.BlockSpec((tm, d), lambda i: (i, 0)),
        out_shape=jax.ShapeDtypeStruct((m_rows, d), BF16),
        scratch_shapes=[pltpu.VMEM((SUBLANES, tm + halo, d), F32),
                        pltpu.VMEM((CONV_WIDTH, SUBLANES, d), F32), pltpu.VMEM((tm, d), F32)],
        compiler_params=_params(40, 1),
        name="conv_ln",
    )(z, dw_w, dw_b, ln_g, ln_b)


def _gelu_in(h, w_in, b_in, j):
    m_rows, k = h.shape
    n_cols = w_in.shape[-1]
    tm, tn = 512, 1024

    def epilogue(accs, r_refs, o_refs, _):
        a = accs[0] + r_refs[0][...]
        o_refs[0][...] = (0.5 * a * (1.0 + lax.erf(a * math.sqrt(0.5)))).astype(BF16)

    out = _colmm(h, [(w_in, (j,), 0)], [(b_in, _row_spec(b_in, (), tn))],
                 [(jax.ShapeDtypeStruct((m_rows, n_cols), BF16), _tile_spec(tm, tn))],
                 epilogue, tm=tm, tn=tn, nt=n_cols // tn, vmem_mib=48, name="gelu_in")
    return out[0]


def _sgu_gate_body(u_ref, v_ref, g_ref, b_ref, ws_ref, bst_ref, o_ref, *, tm):
    v = v_ref[...].astype(F32)
    mu = jnp.mean(v, axis=-1, keepdims=True)
    var = jnp.mean(jnp.square(v - mu), axis=-1, keepdims=True)
    vn = ((v - mu) * lax.rsqrt(var + EPS) * g_ref[...] + b_ref[...]).astype(BF16)
    hd = v.shape[1] // SGU_HEADS
    tri = (lax.broadcasted_iota(jnp.int32, (SGU_CHUNK, SGU_CHUNK), 0)
           >= lax.broadcasted_iota(jnp.int32, (SGU_CHUNK, SGU_CHUNK), 1))
    for hh in range(SGU_HEADS):
        wc = jnp.where(tri, ws_ref[hh], 0.0).astype(BF16)
        bias = bst_ref[:, hh:hh + 1]
        for ck in range(tm // SGU_CHUNK):
            rs = slice(ck * SGU_CHUNK, (ck + 1) * SGU_CHUNK)
            cs = slice(hh * hd, (hh + 1) * hd)
            sv = jnp.dot(wc, vn[rs, cs], preferred_element_type=F32) + bias
            o_ref[rs, cs] = (u_ref[rs, cs].astype(F32) * sv).astype(o_ref.dtype)


def _sgu_gate(a, ln_g, ln_b, w_s, b_s_t, j):
    m_rows = a.shape[0]
    d = a.shape[1] // 2
    tm = 256
    row = pl.BlockSpec((1, d), lambda i: (0, 0))
    return pl.pallas_call(
        functools.partial(_sgu_gate_body, tm=tm),
        grid=(m_rows // tm,),
        in_specs=[pl.BlockSpec((tm, d), lambda i: (i, 0)), pl.BlockSpec((tm, d), lambda i: (i, 1)),
                  row, row,
                  pl.BlockSpec((None, SGU_HEADS, SGU_CHUNK, SGU_CHUNK), lambda i: (j, 0, 0, 0)),
                  pl.BlockSpec((SGU_CHUNK, SGU_HEADS), lambda i: (0, 0))],
        out_specs=pl.BlockSpec((tm, d), lambda i: (i, 0)),
        out_shape=jax.ShapeDtypeStruct((m_rows, d), BF16),
        compiler_params=_params(40, 1),
        name="sgu_gate",
    )(a, a, ln_g, ln_b, w_s, b_s_t)


def _sconv_in(h, w_in, conv_w, j, make_side=None):
    m_rows, k = h.shape
    dm = w_in.shape[-1] // 3
    tm, tn = 1024, 256
    nt = dm // tn
    halo = SUBLANES

    def epilogue(accs, r_refs, o_refs, s_refs):
        bg, cg, z = accs
        cw_ref = r_refs[0]
        ext = s_refs[0]

        @pl.when(pl.program_id(1) == 0)
        def _():
            ext[0:halo, :] = jnp.zeros((halo, tn), F32)

        cz = cg * z
        ext[halo:halo + tm, :] = cz
        conv = cw_ref[SHORT_CONV_WIDTH - 1:SHORT_CONV_WIDTH, :] * cz
        for back in range(1, SHORT_CONV_WIDTH):
            tap = SHORT_CONV_WIDTH - 1 - back
            conv = conv + cw_ref[tap:tap + 1, :] * ext[halo - back:halo - back + tm, :]
        ext[0:halo, :] = ext[tm:tm + halo, :]
        o_refs[0][...] = (bg * conv).astype(BF16)

    raw = [(conv_w, pl.BlockSpec((None, SHORT_CONV_WIDTH, tn), lambda n, m: (j, 0, n)))]
    out = _colmm(h, [(w_in, (j,), 0), (w_in, (j,), nt), (w_in, (j,), 2 * nt)], raw,
                 [(jax.ShapeDtypeStruct((m_rows, dm), BF16), _tile_spec(tm, tn))],
                 epilogue, tm=tm, tn=tn, nt=nt,
                 scratch=[pltpu.VMEM((tm + halo, tn), F32)], vmem_mib=52, name="sconv_in",
                 make_side=make_side)
    return out


def _router_body(h_ref, w_ref, info_ref, cum_ref, carry):
    logits = jnp.dot(h_ref[...], w_ref[...].astype(BF16), preferred_element_type=F32)
    lane = lax.broadcasted_iota(jnp.int32, logits.shape, 1)
    neg = jnp.float32(-jnp.inf)
    lg = jnp.where(lane < N_EXPERTS, logits, neg)
    v1 = jnp.max(lg, axis=-1, keepdims=True)
    lane_f = lane.astype(F32)
    i1 = jnp.min(jnp.where(lg == v1, lane_f, float(LANES)), axis=-1, keepdims=True)
    lg2 = jnp.where(lane_f == i1, neg, lg)
    v2 = jnp.max(lg2, axis=-1, keepdims=True)
    i2 = jnp.min(jnp.where(lg2 == v2, lane_f, float(LANES)), axis=-1, keepdims=True)
    e2 = jnp.exp(v2 - v1)
    den = 1.0 + e2
    w1, w2 = 1.0 / den, e2 / den

    @pl.when(pl.program_id(0) == 0)
    def _():
        carry[...] = jnp.zeros(carry.shape, F32)

    tm = logits.shape[0]
    cnt = jnp.where(lane_f == i1, 1.0, 0.0) + jnp.where(lane_f == i2, 1.0, 0.0)
    strict = jnp.where(lax.broadcasted_iota(jnp.int32, (tm, tm), 1)
                       < lax.broadcasted_iota(jnp.int32, (tm, tm), 0), 1.0, 0.0).astype(BF16)
    before = jnp.dot(strict, cnt.astype(BF16), preferred_element_type=F32) + carry[...]
    r1 = jnp.sum(jnp.where(lane_f == i1, before, 0.0), axis=-1, keepdims=True)
    r2 = jnp.sum(jnp.where(lane_f == i2, before, 0.0), axis=-1, keepdims=True)
    fields = (i1, i2, w1, w2, r1, r2)
    info = jnp.zeros(logits.shape, F32)
    for q, val in enumerate(fields):
        info = jnp.where(lane == q, val, info)
    info_ref[...] = info[:, :info_ref.shape[1]]
    total = carry[...] + jnp.sum(cnt, axis=0, keepdims=True)
    carry[...] = total
    cum_ref[...] = total


def _router(h, w_router_padded):
    m_rows, k = h.shape
    tm = MOE_SUB
    return pl.pallas_call(
        _router_body,
        grid=(m_rows // tm,),
        in_specs=[pl.BlockSpec((tm, k), lambda i: (i, 0)), pl.BlockSpec((k, LANES), lambda i: (0, 0))],
        out_specs=[pl.BlockSpec((tm, SUBLANES), lambda i: (i, 0)),
                   pl.BlockSpec((None, 1, LANES), lambda i: (i, 0, 0))],
        out_shape=[jax.ShapeDtypeStruct((m_rows, SUBLANES), F32),
                   jax.ShapeDtypeStruct((m_rows // tm, 1, LANES), F32)],
        scratch_shapes=[pltpu.VMEM((1, LANES), F32)],
        compiler_params=_params(40, 1),
        name="router",
    )(h, w_router_padded)


def _dispatch_plan(info, cum):
    rows, sub, tok, n_exp = MOE_ROWS, MOE_SUB, TOKEN_BLOCK, N_EXPERTS
    per = rows // sub
    n_tok = info.shape[0]
    tbn = n_tok // tok
    nb_max = (2 * n_tok) // rows + n_exp
    nq = per * nb_max
    s_max = nq + n_exp * (tbn - 1)
    i32 = jnp.int32
    ids = info.astype(i32)
    i1, i2, r1, r2 = ids[:, 0], ids[:, 1], ids[:, 4], ids[:, 5]
    cb_fine = jnp.concatenate([jnp.zeros((1, n_exp), i32), cum[:, 0, :n_exp].astype(i32)])
    cb = cb_fine[::tok // sub]
    counts = cb[-1]
    nblk = (counts + rows - 1) // rows
    blk_end = jnp.cumsum(nblk)
    blk_off = blk_end - nblk
    nb = blk_end[-1]
    experts = jnp.arange(n_exp, dtype=i32)[:, None]
    row_off = (blk_off * rows)[None, :]
    slot1 = jnp.sum(jnp.where(i1[None, :] == experts, row_off.T, 0), axis=0) + r1
    slot2 = jnp.sum(jnp.where(i2[None, :] == experts, row_off.T, 0), axis=0) + r2

    b_idx = jnp.arange(nb_max, dtype=i32)
    b_used = b_idx < nb
    b_clamped = jnp.minimum(b_idx, nb - 1)
    blk_e = jnp.minimum(jnp.sum(b_clamped[:, None] >= blk_end[None, :], axis=1), n_exp - 1).astype(i32)
    prev_e = jnp.concatenate([jnp.full((1,), -1, i32), blk_e[:-1]])
    tail_rows = counts[blk_e] - (blk_end[blk_e] - 1 - blk_off[blk_e]) * rows
    half = (b_clamped == blk_end[blk_e] - 1) & (tail_rows <= rows // 2)
    blk_flag = (b_used.astype(i32) + 2 * (b_used & (blk_e != prev_e)).astype(i32)
                + 4 * (b_used & half).astype(i32))

    q = jnp.arange(nq, dtype=i32)
    e_q = jnp.minimum(jnp.sum(q[:, None] >= per * blk_end[None, :], axis=1), n_exp - 1)
    k_q = (q - per * blk_off[e_q])[:, None]
    lo = jnp.maximum(k_q * sub, cb[:-1].T[e_q])
    hi = jnp.minimum((k_q + 1) * sub, cb[1:].T[e_q])
    inter = (lo < hi) & (q < per * nb)[:, None]
    first_tb = jnp.arange(tbn, dtype=i32)[None, :] == 0
    touch = inter | (~jnp.any(inter, axis=1, keepdims=True) & first_tb)
    s_idx = jnp.arange(s_max, dtype=i32)

    def step_lists(mask):
        in_row = jnp.cumsum(mask.astype(i32), axis=1)
        row_end = jnp.cumsum(in_row[:, -1])
        n_steps = row_end[-1]
        s_clamped = jnp.minimum(s_idx, n_steps - 1)
        major = jnp.sum(row_end[None, :] <= s_clamped[:, None], axis=1).astype(i32)
        within = s_clamped - (row_end - in_row[:, -1])[major]
        minor = jnp.sum(in_row[major] <= within[:, None], axis=1).astype(i32)
        s_valid = s_idx < n_steps
        prev_m = jnp.concatenate([jnp.full((1,), -1, i32), major[:-1]])
        next_m = jnp.concatenate([major[1:], jnp.full((1,), -1, i32)])
        first = s_valid & (prev_m != major)
        last = s_valid & ((next_m != major) | (s_idx == n_steps - 1))
        return major, minor, s_valid, 2 * first.astype(i32) + 4 * last.astype(i32)

    g_q, g_tb, g_valid, g_edge = step_lists(touch)
    g_flag = (g_valid & inter[g_q, g_tb]).astype(i32) + g_edge

    c_q0 = jnp.minimum((row_off + cb_fine[:-1]) // sub, nq - 2).astype(i32).ravel()
    c_lo = (blk_off * rows).astype(i32)
    c_hi = (blk_end * rows).astype(i32)
    return dict(slot1=slot1, slot2=slot2, w1=info[:, 2], w2=info[:, 3],
                blk_e=blk_e, blk_row=b_clamped, blk_flag=blk_flag,
                g_q=g_q, g_tb=g_tb, g_flag=g_flag, c_q0=c_q0, c_lo=c_lo, c_hi=c_hi,
                nb_max=nb_max, s_max=s_max)


def _moe_gather(h, plan):
    n_tok, d = h.shape
    sub, tok = MOE_SUB, TOKEN_BLOCK
    p_rows = plan["nb_max"] * MOE_ROWS

    def body(q_ref, tb_ref, fl_ref, h_ref, s1_ref, s2_ref, w1_ref, w2_ref, hg_ref, ws_ref, acc, wacc):
        s = pl.program_id(0)
        flag = fl_ref[s]

        @pl.when((flag & 2) != 0)
        def _():
            acc[...] = jnp.zeros(acc.shape, F32)
            wacc[...] = jnp.zeros(wacc.shape, F32)

        @pl.when((flag & 1) != 0)
        def _():
            slot = q_ref[s] * sub + lax.broadcasted_iota(jnp.int32, (sub, tok), 0)
            d1 = s1_ref[...] == slot
            d2 = s2_ref[...] == slot
            sel = jnp.where(d1, 1.0, jnp.where(d2, 1.0, 0.0)).astype(BF16)
            t0 = pl.multiple_of(tb_ref[s] * tok, tok)
            acc[...] += jnp.dot(sel, h_ref[pl.ds(t0, tok), :], preferred_element_type=F32)
            wacc[...] += jnp.sum(jnp.where(d1, w1_ref[...], 0.0) + jnp.where(d2, w2_ref[...], 0.0),
                                 axis=1, keepdims=True)

        @pl.when((flag & 4) != 0)
        def _():
            hg_ref[...] = acc[...].astype(BF16)
            ws_ref[...] = wacc[...]

    tok_row = pl.BlockSpec((None, 1, tok), lambda s, q, tb, fl: (tb[s], 0, 0))
    grid_spec = pltpu.PrefetchScalarGridSpec(
        num_scalar_prefetch=3,
        grid=(plan["s_max"],),
        in_specs=[pl.BlockSpec((n_tok, d), lambda s, q, tb, fl: (0, 0), pipeline_mode=pl.Buffered(1)),
                  tok_row, tok_row, tok_row, tok_row],
        out_specs=[pl.BlockSpec((sub, d), lambda s, q, tb, fl: (q[s], 0)),
                   pl.BlockSpec((sub, 1), lambda s, q, tb, fl: (q[s], 0))],
        scratch_shapes=[pltpu.VMEM((sub, d), F32), pltpu.VMEM((sub, 1), F32)])
    rows = lambda a: a.reshape(n_tok // tok, 1, tok)
    return pl.pallas_call(
        body, grid_spec=grid_spec,
        out_shape=[jax.ShapeDtypeStruct((p_rows, d), BF16), jax.ShapeDtypeStruct((p_rows, 1), F32)],
        compiler_params=_params(48, 1), name="moe_gather",
    )(plan["g_q"], plan["g_tb"], plan["g_flag"], h,
      rows(plan["slot1"]), rows(plan["slot2"]), rows(plan["w1"]), rows(plan["w2"]))


def _grouped_mm(x, weights, lead, plan, row_inputs, out_cols, epilogue, *, tn, vmem_mib, name):
    p_rows, k = x.shape
    blk = MOE_ROWS
    half = blk // 2
    nw, nr = len(weights), len(row_inputs)

    def body(be_ref, br_ref, fl_ref, x_ref, *refs):
        w_refs, r_refs = refs[:nw], refs[nw:nw + nr]
        o_ref = refs[nw + nr]
        wb_refs = refs[nw + nr + 1:]
        flag = fl_ref[pl.program_id(1)]

        @pl.when((flag & 2) != 0)
        def _():
            for w_ref, wb in zip(w_refs, wb_refs):
                wb[...] = w_ref[...].astype(BF16)

        def compute(n_rows):
            xv = x_ref[0:n_rows, :]
            accs = [jnp.dot(xv, wb[...], preferred_element_type=F32) for wb in wb_refs]
            o_ref[0:n_rows, :] = epilogue(accs, [r[0:n_rows, :] for r in r_refs]).astype(BF16)

        @pl.when((flag & 5) == 1)
        def _():
            compute(blk)

        @pl.when((flag & 5) == 5)
        def _():
            compute(half)
            o_ref[half:blk, :] = jnp.zeros((blk - half, tn), BF16)

        @pl.when((flag & 1) == 0)
        def _():
            o_ref[...] = jnp.zeros((blk, tn), BF16)

    w_spec = pl.BlockSpec((None,) * (len(lead) + 1) + (k, tn),
                          lambda n, b, be, br, fl: (*lead, be[b], 0, n))
    grid_spec = pltpu.PrefetchScalarGridSpec(
        num_scalar_prefetch=3,
        grid=(out_cols // tn, plan["nb_max"]),
        in_specs=[pl.BlockSpec((blk, k), lambda n, b, be, br, fl: (br[b], 0))]
        + [w_spec] * nw
        + [pl.BlockSpec((blk, a.shape[1]), lambda n, b, be, br, fl: (br[b], 0)) for a in row_inputs],
        out_specs=pl.BlockSpec((blk, tn), lambda n, b, be, br, fl: (b, n)),
        scratch_shapes=[pltpu.VMEM((k, tn), BF16) for _ in weights])
    return pl.pallas_call(
        body, grid_spec=grid_spec,
        out_shape=jax.ShapeDtypeStruct((p_rows, out_cols), BF16),
        compiler_params=_params(vmem_mib, 2), name=name,
    )(plan["blk_e"], plan["blk_row"], plan["blk_flag"], x, *weights, *row_inputs)


def _moe_combine(yw, plan, x_res, gate, norm):
    n_tok, d = x_res.shape
    sub = MOE_SUB
    n_win = 2 * N_EXPERTS
    n_norm = len(norm)

    def body(q0_ref, lo_ref, hi_ref, *refs):
        y_refs = refs[:n_win]
        s1_ref, s2_ref, x_ref, g_ref = refs[n_win:n_win + 4]
        norm_refs = refs[n_win + 4:n_win + 4 + n_norm]
        o_refs = refs[n_win + 4 + n_norm:]
        t = pl.program_id(0)
        s1, s2 = s1_ref[...], s2_ref[...]
        col = lax.broadcasted_iota(jnp.int32, (sub, sub), 1)
        acc = None
        for e in range(N_EXPERTS):
            for j in range(2):
                slot = (q0_ref[t * N_EXPERTS + e] + j) * sub + col
                inside = (slot - lo_ref[e]).astype(jnp.uint32) < (hi_ref[e] - lo_ref[e]).astype(jnp.uint32)
                slot = jnp.where(inside, slot, -1)
                sel = jnp.where(s1 == slot, 1.0, jnp.where(s2 == slot, 1.0, 0.0)).astype(BF16)
                part = jnp.dot(sel, y_refs[2 * e + j][...], preferred_element_type=F32)
                acc = part if acc is None else acc + part
        x_new = x_ref[...] + g_ref[...] * acc
        if n_norm == 1:
            o_refs[0][...] = _rms(x_new, norm_refs[0][...])
        else:
            o_refs[0][...] = x_new
            o_refs[1][...] = _rms_mod(x_new, *[r[...] for r in norm_refs]).astype(BF16)

    def window_spec(e, j):
        return pl.BlockSpec((sub, d), lambda t, q0, lo, hi: (q0[t * N_EXPERTS + e] + j, 0))

    tok_col = pl.BlockSpec((sub, 1), lambda t, q0, lo, hi: (t, 0))
    tok_blk = pl.BlockSpec((sub, d), lambda t, q0, lo, hi: (t, 0))
    row = pl.BlockSpec((1, d), lambda t, q0, lo, hi: (0, 0))
    out_shape = [jax.ShapeDtypeStruct((n_tok, d), F32)]
    if n_norm > 1:
        out_shape.append(jax.ShapeDtypeStruct((n_tok, d), BF16))
    grid_spec = pltpu.PrefetchScalarGridSpec(
        num_scalar_prefetch=3,
        grid=(n_tok // sub,),
        in_specs=[window_spec(e, j) for e in range(N_EXPERTS) for j in range(2)]
        + [tok_col, tok_col, tok_blk, row] + [row] * n_norm,
        out_specs=[tok_blk] * len(out_shape))
    col_of = lambda a: a.reshape(n_tok, 1)
    return pl.pallas_call(
        body, grid_spec=grid_spec,
        out_shape=out_shape,
        compiler_params=_params(56, 1), name="moe_combine",
    )(plan["c_q0"], plan["c_lo"], plan["c_hi"], *([yw] * n_win),
      col_of(plan["slot1"]), col_of(plan["slot2"]), x_res, gate, *norm)


def _moe(h, x_res, gate, w_router, w1, w3, w2, kk, norm):
    w_r = jnp.pad(w_router[kk], ((0, 0), (0, LANES - N_EXPERTS)))
    info, cum = _router(h, w_r)
    plan = _dispatch_plan(info, cum)
    hg, w_slot = _moe_gather(h, plan)
    hid = _grouped_mm(hg, [w1, w3], (kk,), plan, [], w1.shape[-1],
                      lambda accs, _: jax.nn.silu(accs[0]) * accs[1],
                      tn=1024, vmem_mib=58, name="moe_up")
    yw = _grouped_mm(hid, [w2], (kk,), plan, [w_slot], w2.shape[-1],
                     lambda accs, rows: accs[0] * rows[0],
                     tn=1024, vmem_mib=58, name="moe_down")
    return _moe_combine(yw, plan, x_res, gate, norm)


def kernel(x, c, ada_w, ada_b, norm1_g, norm2_g, pool_w_in, pool_w_grp, pool_scale, pool_w_out, conv_w_in, conv_b_in, conv_dw_w, conv_dw_b, conv_ln_g, conv_ln_b, conv_w_out, sgu_w_in, sgu_b_in, sgu_ln_g, sgu_ln_b, sgu_w_s, sgu_b_s, sgu_w_out, sconv_w_in, sconv_w, sconv_w_out, ffn_w1, ffn_w3, ffn_w2, moe_router, moe_w1, moe_w3, moe_w2, final_g):
    batch, seq, d = x.shape
    assert batch == 1
    depth = ada_w.shape[0]
    xs = x.reshape(seq, d)
    c_col = c.reshape(d, 1)
    ada_b3 = ada_b.reshape(depth, 1, ada_b.shape[-1])
    split = lambda mod: [mod[:, q * d:(q + 1) * d] for q in range(6)]

    def side_for(layer):
        if layer >= depth:
            return None
        return lambda outer, inner: _AdaSide(c_col, ada_w, ada_b3, layer, outer, inner)

    mod_next = _ada_mod(c_col, ada_w, ada_b3, 0)
    final_g = final_g.reshape(1, d)
    h = None
    out = None
    for i in range(depth):
        sh1, sc1, g1, sh2, sc2, g2 = split(mod_next)

        if h is None:
            h = _norm_mod(xs, norm1_g[i:i + 1], sc1, sh1)
        mixer, j = i % 4, i // 4
        mixer_side = side_for(i + 1) if i % 2 == 1 else None
        side_out = []
        if mixer == 0:
            a = _pool_in(h, pool_w_in, pool_w_grp, pool_scale[j:j + 1], j)
            w_out = pool_w_out
        elif mixer == 1:
            z, *side_out = _glu_in(h, conv_w_in, conv_b_in[j:j + 1], j, mixer_side)
            a = _conv_ln(z, conv_dw_w, conv_dw_b[j:j + 1], conv_ln_g[j:j + 1], conv_ln_b[j:j + 1], j)
            w_out = conv_w_out
        elif mixer == 2:
            a = _gelu_in(h, sgu_w_in, sgu_b_in[j:j + 1], j)
            a = _sgu_gate(a, sgu_ln_g[j:j + 1], sgu_ln_b[j:j + 1], sgu_w_s, sgu_b_s[j].T, j)
            w_out = sgu_w_out
        else:
            a, *side_out = _sconv_in(h, sconv_w_in, sconv_w, j, mixer_side)
            w_out = sconv_w_out
        xs, h = _mixer_out_norm(a, w_out, j, xs, g1, norm2_g[i:i + 1], sc2, sh2)

        kk = i // 2
        if i % 2 == 0:
            hid, *side_out = _swiglu_up(h, ffn_w1, ffn_w3, (kk,), tm=1024, tn=512, vmem_mib=52,
                                        name="ffn_up", make_side=side_for(i + 1))
            xs = _mm_residual(hid, ffn_w2, (kk,), xs, g2, tm=512, tn=512, vmem_mib=56, name="ffn_down")
            mod_next = side_out[0] if side_out else None
            h = None
        elif i + 1 < depth:
            mod_next = side_out[0]
            sh_n, sc_n = split(mod_next)[:2]
            xs, h = _moe(h, xs, g2, moe_router, moe_w1, moe_w3, moe_w2, kk,
                         (norm1_g[i + 1:i + 2], sc_n, sh_n))
        else:
            (out,) = _moe(h, xs, g2, moe_router, moe_w1, moe_w3, moe_w2, kk, (final_g,))
    if out is None:
        out = _final_norm(xs, final_g)
    return out.reshape(batch, seq, d)
```

```python
import functools
import math

import jax
import jax.numpy as jnp
from jax import lax
from jax.experimental import pallas as pl
from jax.experimental.pallas import tpu as pltpu

EPS = 1e-6
POOL_WINDOWS = (2, 4, 8, 16)
CONV_WIDTH = 31
SGU_CHUNK = 128
SGU_HEADS = 8
SHORT_CONV_WIDTH = 3
N_EXPERTS = 8
MOE_ROWS = 512
MOE_SUB = 256
TOKEN_BLOCK = 512
LANES = 128
SUBLANES = 8
MIB = 1024 * 1024

BF16 = jnp.bfloat16
F32 = jnp.float32


def _params(vmem_mib, ndims):
    return pltpu.CompilerParams(
        dimension_semantics=("arbitrary",) * ndims,
        vmem_limit_bytes=vmem_mib * MIB)


def _ada_chunk(c_ref, w_ref, b_ref):
    c_act = jax.nn.silu(c_ref[...])
    return jnp.sum(w_ref[...] * c_act, axis=0, keepdims=True) + b_ref[...]


def _ada_mod(c_col, ada_w, ada_b3, layer):
    depth, d, n = ada_w.shape
    tn = 1024

    def body(c_ref, w_ref, b_ref, o_ref):
        o_ref[...] = _ada_chunk(c_ref, w_ref, b_ref)

    return pl.pallas_call(
        body,
        grid=(n // tn,),
        in_specs=[
            pl.BlockSpec((d, 1), lambda j: (0, 0)),
            pl.BlockSpec((None, d, tn), lambda j: (layer, 0, j)),
            pl.BlockSpec((None, 1, tn), lambda j: (layer, 0, j)),
        ],
        out_specs=pl.BlockSpec((1, tn), lambda j: (0, j)),
        out_shape=jax.ShapeDtypeStruct((1, n), F32),
        compiler_params=_params(40, 1),
        name="ada_mod",
    )(c_col, ada_w, ada_b3)


class _AdaSide:
    def __init__(self, c_col, ada_w, ada_b3, layer, outer, inner):
        self.arrays = (c_col, ada_w, ada_b3)
        self.layer, self.inner = layer, inner
        _, self.d, self.n = ada_w.shape
        self.chunk = next(ch for ch in range(LANES, self.n + 1, LANES)
                          if self.n % ch == 0 and self.n // ch <= outer * inner)
        self.n_chunks = self.n // self.chunk
        self.out_shape = jax.ShapeDtypeStruct((1, self.n), F32)

    def _chunk(self, n, m):
        return jnp.minimum(n * self.inner + m, self.n_chunks - 1)

    def specs(self):
        d, layer, chunk = self.d, self.layer, self.chunk
        ins = [pl.BlockSpec((d, 1), lambda n, m, *_: (0, 0)),
               pl.BlockSpec((None, d, chunk), lambda n, m, *_: (layer, 0, self._chunk(n, m))),
               pl.BlockSpec((None, 1, chunk), lambda n, m, *_: (layer, 0, self._chunk(n, m)))]
        return ins, pl.BlockSpec((1, chunk), lambda n, m, *_: (0, self._chunk(n, m)))

    def run(self, c_ref, w_ref, b_ref, o_ref):
        o_ref[...] = _ada_chunk(c_ref, w_ref, b_ref)


def _rms(x, g):
    return (x * lax.rsqrt(jnp.mean(x * x, axis=-1, keepdims=True) + EPS)) * g


def _rms_mod(x, g, sc, sh):
    return _rms(x, g) * (1.0 + sc) + sh


def _norm_mod_body(x_ref, g_ref, sc_ref, sh_ref, o_ref):
    o_ref[...] = _rms_mod(x_ref[...], g_ref[...], sc_ref[...], sh_ref[...]).astype(o_ref.dtype)


def _norm_mod(x, g, sc, sh):
    m, d = x.shape
    tm = 512
    row = pl.BlockSpec((1, d), lambda i: (0, 0))
    return pl.pallas_call(
        _norm_mod_body,
        grid=(m // tm,),
        in_specs=[pl.BlockSpec((tm, d), lambda i: (i, 0)), row, row, row],
        out_specs=pl.BlockSpec((tm, d), lambda i: (i, 0)),
        out_shape=jax.ShapeDtypeStruct((m, d), BF16),
        compiler_params=_params(40, 1),
        name="norm_mod",
    )(x, g, sc, sh)


def _final_norm_body(x_ref, g_ref, o_ref):
    o_ref[...] = _rms(x_ref[...], g_ref[...])


def _final_norm(x, g):
    m, d = x.shape
    tm = 512
    return pl.pallas_call(
        _final_norm_body,
        grid=(m // tm,),
        in_specs=[pl.BlockSpec((tm, d), lambda i: (i, 0)), pl.BlockSpec((1, d), lambda i: (0, 0))],
        out_specs=pl.BlockSpec((tm, d), lambda i: (i, 0)),
        out_shape=jax.ShapeDtypeStruct((m, d), F32),
        compiler_params=_params(40, 1),
        name="final_norm",
    )(x, g)


def _w_spec(w, lead, k, tn, off):
    assert w.shape[len(lead)] == k
    return pl.BlockSpec((None,) * len(lead) + (k, tn), lambda n, m: (*lead, 0, n + off))


def _row_spec(a, lead, tn, off=0):
    return pl.BlockSpec((None,) * len(lead) + (1, tn), lambda n, m: (*lead, 0, n + off))


def _colmm(x, weights, raw_inputs, outs, epilogue, *, tm, tn, nt, scratch=(), vmem_mib, name,
           make_side=None):
    m_rows, k = x.shape
    nw, nr, no = len(weights), len(raw_inputs), len(outs)
    side = make_side(nt, m_rows // tm) if make_side else None
    side_in, side_out = side.specs() if side else ([], None)
    ns = len(side_in)

    def body(*refs):
        x_ref = refs[0]
        w_refs = refs[1:1 + nw]
        r_refs = refs[1 + nw:1 + nw + nr]
        side_refs = refs[1 + nw + nr:1 + nw + nr + ns]
        o_refs = refs[1 + nw + nr + ns:1 + nw + nr + ns + no]
        rest = refs[1 + nw + nr + ns + no:]
        side_o, rest = (rest[0], rest[1:]) if side else (None, rest)
        wb_refs, extra = rest[:nw], rest[nw:]

        @pl.when(pl.program_id(1) == 0)
        def _():
            for w_ref, wb in zip(w_refs, wb_refs):
                wb[...] = w_ref[...].astype(BF16)

        xv = x_ref[...]
        accs = [jnp.dot(xv, wb[...], preferred_element_type=F32) for wb in wb_refs]
        epilogue(accs, r_refs, o_refs, extra)
        if side:
            side.run(*side_refs, side_o)

    in_specs = [pl.BlockSpec((tm, k), lambda n, m: (m, 0))]
    in_specs += [_w_spec(w, lead, k, tn, off) for (w, lead, off) in weights]
    in_specs += [spec for (_, spec) in raw_inputs]
    return pl.pallas_call(
        body,
        grid=(nt, m_rows // tm),
        in_specs=in_specs + side_in,
        out_specs=[spec for (_, spec) in outs] + ([side_out] if side else []),
        out_shape=[sds for (sds, _) in outs] + ([side.out_shape] if side else []),
        scratch_shapes=[pltpu.VMEM((k, tn), BF16) for _ in weights] + list(scratch),
        compiler_params=_params(vmem_mib, 2),
        name=name,
    )(x, *[w for (w, _, _) in weights], *[a for (a, _) in raw_inputs], *(side.arrays if side else ()))


def _tile_spec(tm, tn):
    return pl.BlockSpec((tm, tn), lambda n, m: (m, n))


def _mm_residual(a, w, lead, x_res, gate, *, tm, tn, vmem_mib, name):
    m_rows = a.shape[0]
    n_cols = x_res.shape[1]
    raw = [(gate, _row_spec(gate, (), tn)), (x_res, _tile_spec(tm, tn))]

    def epilogue(accs, r_refs, o_refs, _):
        o_refs[0][...] = r_refs[1][...] + accs[0] * r_refs[0][...]

    out = _colmm(a, [(w, lead, 0)], raw,
                 [(jax.ShapeDtypeStruct((m_rows, n_cols), F32), _tile_spec(tm, tn))],
                 epilogue, tm=tm, tn=tn, nt=n_cols // tn, vmem_mib=vmem_mib, name=name)
    return out[0]


def _mixer_out_norm(a, w, j, x_res, gate, g, sc, sh):
    m_rows, k = a.shape
    d = x_res.shape[1]
    tm = 256

    def body(a_ref, w_ref, gate_ref, x_ref, g_ref, sc_ref, sh_ref, xo_ref, ho_ref, wb):
        @pl.when(pl.program_id(0) == 0)
        def _():
            wb[...] = w_ref[...].astype(BF16)

        y = jnp.dot(a_ref[...], wb[...], preferred_element_type=F32)
        x_new = x_ref[...] + gate_ref[...] * y
        xo_ref[...] = x_new
        ho_ref[...] = _rms_mod(x_new, g_ref[...], sc_ref[...], sh_ref[...]).astype(BF16)

    row = pl.BlockSpec((1, d), lambda i: (0, 0))
    blk_in = pl.BlockSpec((tm, k), lambda i: (i, 0))
    blk_d = pl.BlockSpec((tm, d), lambda i: (i, 0))
    return pl.pallas_call(
        body,
        grid=(m_rows // tm,),
        in_specs=[blk_in,
                  pl.BlockSpec((None, k, d), lambda i: (j, 0, 0), pipeline_mode=pl.Buffered(1)),
                  row, blk_d, row, row, row],
        out_specs=[blk_d, blk_d],
        out_shape=[jax.ShapeDtypeStruct((m_rows, d), F32), jax.ShapeDtypeStruct((m_rows, d), BF16)],
        scratch_shapes=[pltpu.VMEM((k, d), BF16)],
        compiler_params=_params(48, 1),
        name="mixer_out",
    )(a, w, gate, x_res, g, sc, sh)


def _swiglu_up(h, w1, w3, lead, *, tm, tn, vmem_mib, name, make_side=None):
    m_rows = h.shape[0]
    n_cols = w1.shape[-1]

    def epilogue(accs, r_refs, o_refs, _):
        o_refs[0][...] = (jax.nn.silu(accs[0]) * accs[1]).astype(BF16)

    return _colmm(h, [(w1, lead, 0), (w3, lead, 0)], [],
                  [(jax.ShapeDtypeStruct((m_rows, n_cols), BF16), _tile_spec(tm, tn))],
                  epilogue, tm=tm, tn=tn, nt=n_cols // tn, vmem_mib=vmem_mib, name=name,
                  make_side=make_side)


def _pool_in(h, w_in, w_grp, scale, j):
    m_rows, k = h.shape
    n_groups = len(POOL_WINDOWS)
    tn = w_in.shape[-1] // n_groups
    tm = 1024
    assert all(w & (w - 1) == 0 for w in POOL_WINDOWS)
    halo = 2 * max(POOL_WINDOWS)

    def epilogue(accs, r_refs, o_refs, s_refs):
        z = accs[0]
        wg_ref, scale_ref = r_refs
        zext, buf_a, buf_b, wgb, pooled = s_refs
        n, m = pl.program_id(0), pl.program_id(1)
        end = halo + tm

        @pl.when(m == 0)
        def _():
            zext[0:halo, :] = jnp.zeros((halo, tn), F32)
            wgb[...] = wg_ref[...].astype(BF16)

        zext[halo:end, :] = z
        t = m * tm + lax.broadcasted_iota(jnp.int32, (tm, tn), 0)
        for gi, win in enumerate(POOL_WINDOWS):
            @pl.when(n == gi)
            def _(win=win):
                levels = win.bit_length() - 1
                src = zext
                for lv in range(levels - 1):
                    dst = buf_a if lv % 2 == 0 else buf_b
                    lo, span = halo - SUBLANES * (levels - 1 - lv), 1 << lv
                    dst[lo:end, :] = src[lo:end, :] + src[lo - span:end - span, :]
                    src = dst
                span = win // 2
                wsum = src[halo:end, :] + src[halo - span:end - span, :]
                count = jnp.minimum(t + 1, win).astype(F32)
                pooled[...] = (wsum / count - z).astype(BF16)
        zext[0:halo, :] = zext[tm:end, :]
        mixed = jnp.dot(pooled[...], wgb[...], preferred_element_type=F32)
        o_refs[0][...] = (mixed * scale_ref[...]).astype(BF16)

    raw = [(w_grp, pl.BlockSpec((None, None, tn, tn), lambda n, m: (j, n, 0, 0))),
           (scale, _row_spec(scale, (), tn))]
    out = _colmm(h, [(w_in, (j,), 0)], raw,
                 [(jax.ShapeDtypeStruct((m_rows, n_groups * tn), BF16), _tile_spec(tm, tn))],
                 epilogue, tm=tm, tn=tn, nt=n_groups,
                 scratch=[pltpu.VMEM((tm + halo, tn), F32)] * 3
                 + [pltpu.VMEM((tn, tn), BF16), pltpu.VMEM((tm, tn), BF16)],
                 vmem_mib=48, name="pool_in")
    return out[0]


def _glu_in(h, w_in, b_in, j, make_side=None):
    m_rows, k = h.shape
    dm = w_in.shape[-1] // 2
    tm, tn = 1024, 512
    nt = dm // tn

    def epilogue(accs, r_refs, o_refs, _):
        a = accs[0] + r_refs[0][...]
        g = accs[1] + r_refs[1][...]
        o_refs[0][...] = (a * jax.nn.sigmoid(g)).astype(BF16)

    raw = [(b_in, _row_spec(b_in, (), tn)), (b_in, _row_spec(b_in, (), tn, off=nt))]
    out = _colmm(h, [(w_in, (j,), 0), (w_in, (j,), nt)], raw,
                 [(jax.ShapeDtypeStruct((m_rows, dm), BF16), _tile_spec(tm, tn))],
                 epilogue, tm=tm, tn=tn, nt=nt, vmem_mib=52, name="glu_in", make_side=make_side)
    return out


def _conv_ln_body(z_ref, w_ref, cb_ref, g_ref, b_ref, o_ref, zsh, wbc, conv, *, tm, halo, rows, cols):
    d = z_ref.shape[1]
    sl = SUBLANES

    @pl.when(pl.program_id(0) == 0)
    def _():
        zsh[0, 0:halo, :] = jnp.zeros((halo, d), F32)
        for kk in range(CONV_WIDTH):
            wbc[kk] = jnp.broadcast_to(w_ref[kk:kk + 1, :], (sl, d))

    zsh[0, halo:halo + tm, :] = z_ref[...].astype(F32)
    for b in range(1, sl):
        zsh[b, sl:halo + tm, :] = zsh[0, sl - b:halo + tm - b, :]

    for c0 in range(0, d, cols):
        cs = slice(c0, c0 + cols)
        for r0 in range(0, tm, rows):
            acc = jnp.zeros((rows, cols), F32)
            for kk in range(CONV_WIDTH):
                a, b = divmod(CONV_WIDTH - 1 - kk, sl)
                start = halo + r0 - sl * a
                wv = jnp.tile(wbc[kk, :, cs], (rows // sl, 1))
                acc = acc + wv * zsh[b, start:start + rows, cs]
            conv[r0:r0 + rows, cs] = acc + cb_ref[:, cs]
    zsh[0, 0:halo, :] = zsh[0, tm:tm + halo, :]

    y = conv[...]
    mu = jnp.mean(y, axis=-1, keepdims=True)
    var = jnp.mean(jnp.square(y - mu), axis=-1, keepdims=True)
    yn = (y - mu) * lax.rsqrt(var + EPS) * g_ref[...] + b_ref[...]
    o_ref[...] = jax.nn.silu(yn).astype(o_ref.dtype)


def _conv_ln(z, dw_w, dw_b, ln_g, ln_b, j):
    m_rows, d = z.shape
    tm, halo = 256, 32
    row = pl.BlockSpec((1, d), lambda i: (0, 0))
    return pl.pallas_call(
        functools.partial(_conv_ln_body, tm=tm, halo=halo, rows=64, cols=512),
        grid=(m_rows // tm,),
        in_specs=[pl.BlockSpec((tm, d), lambda i: (i, 0)),
                  pl.BlockSpec((None, CONV_WIDTH, d), lambda i: (j, 0, 0)), row, row, row],
        out_specs=pl.BlockSpec((tm, d), lambda i: (i, 0)),
        out_shape=jax.ShapeDtypeStruct((m_rows, d), BF16),
        scratch_shapes=[pltpu.VMEM((SUBLANES, tm + halo, d), F32),
                        pltpu.VMEM((CONV_WIDTH, SUBLANES, d), F32), pltpu.VMEM((tm, d), F32)],
        compiler_params=_params(40, 1),
        name="conv_ln",
    )(z, dw_w, dw_b, ln_g, ln_b)


def _gelu_in(h, w_in, b_in, j):
    m_rows, k = h.shape
    n_cols = w_in.shape[-1]
    tm, tn = 512, 1024

    def epilogue(accs, r_refs, o_refs, _):
        a = accs[0] + r_refs[0][...]
        o_refs[0][...] = (0.5 * a * (1.0 + lax.erf(a * math.sqrt(0.5)))).astype(BF16)

    out = _colmm(h, [(w_in, (j,), 0)], [(b_in, _row_spec(b_in, (), tn))],
                 [(jax.ShapeDtypeStruct((m_rows, n_cols), BF16), _tile_spec(tm, tn))],
                 epilogue, tm=tm, tn=tn, nt=n_cols // tn, vmem_mib=48, name="gelu_in")
    return out[0]


def _sgu_gate_body(u_ref, v_ref, g_ref, b_ref, ws_ref, bst_ref, o_ref, *, tm):
    v = v_ref[...].astype(F32)
    mu = jnp.mean(v, axis=-1, keepdims=True)
    var = jnp.mean(jnp.square(v - mu), axis=-1, keepdims=True)
    vn = ((v - mu) * lax.rsqrt(var + EPS) * g_ref[...] + b_ref[...]).astype(BF16)
    hd = v.shape[1] // SGU_HEADS
    tri = (lax.broadcasted_iota(jnp.int32, (SGU_CHUNK, SGU_CHUNK), 0)
           >= lax.broadcasted_iota(jnp.int32, (SGU_CHUNK, SGU_CHUNK), 1))
    for hh in range(SGU_HEADS):
        wc = jnp.where(tri, ws_ref[hh], 0.0).astype(BF16)
        bias = bst_ref[:, hh:hh + 1]
        for ck in range(tm // SGU_CHUNK):
            rs = slice(ck * SGU_CHUNK, (ck + 1) * SGU_CHUNK)
            cs = slice(hh * hd, (hh + 1) * hd)
            sv = jnp.dot(wc, vn[rs, cs], preferred_element_type=F32) + bias
            o_ref[rs, cs] = (u_ref[rs, cs].astype(F32) * sv).astype(o_ref.dtype)


def _sgu_gate(a, ln_g, ln_b, w_s, b_s_t, j):
    m_rows = a.shape[0]
    d = a.shape[1] // 2
    tm = 256
    row = pl.BlockSpec((1, d), lambda i: (0, 0))
    return pl.pallas_call(
        functools.partial(_sgu_gate_body, tm=tm),
        grid=(m_rows // tm,),
        in_specs=[pl.BlockSpec((tm, d), lambda i: (i, 0)), pl.BlockSpec((tm, d), lambda i: (i, 1)),
                  row, row,
                  pl.BlockSpec((None, SGU_HEADS, SGU_CHUNK, SGU_CHUNK), lambda i: (j, 0, 0, 0)),
                  pl.BlockSpec((SGU_CHUNK, SGU_HEADS), lambda i: (0, 0))],
        out_specs=pl.BlockSpec((tm, d), lambda i: (i, 0)),
        out_shape=jax.ShapeDtypeStruct((m_rows, d), BF16),
        compiler_params=_params(40, 1),
        name="sgu_gate",
    )(a, a, ln_g, ln_b, w_s, b_s_t)


def _sconv_in(h, w_in, conv_w, j, make_side=None):
    m_rows, k = h.shape
    dm = w_in.shape[-1] // 3
    tm, tn = 1024, 256
    nt = dm // tn
    halo = SUBLANES

    def epilogue(accs, r_refs, o_refs, s_refs):
        bg, cg, z = accs
        cw_ref = r_refs[0]
        ext = s_refs[0]

        @pl.when(pl.program_id(1) == 0)
        def _():
            ext[0:halo, :] = jnp.zeros((halo, tn), F32)

        cz = cg * z
        ext[halo:halo + tm, :] = cz
        conv = cw_ref[SHORT_CONV_WIDTH - 1:SHORT_CONV_WIDTH, :] * cz
        for back in range(1, SHORT_CONV_WIDTH):
            tap = SHORT_CONV_WIDTH - 1 - back
            conv = conv + cw_ref[tap:tap + 1, :] * ext[halo - back:halo - back + tm, :]
        ext[0:halo, :] = ext[tm:tm + halo, :]
        o_refs[0][...] = (bg * conv).astype(BF16)

    raw = [(conv_w, pl.BlockSpec((None, SHORT_CONV_WIDTH, tn), lambda n, m: (j, 0, n)))]
    out = _colmm(h, [(w_in, (j,), 0), (w_in, (j,), nt), (w_in, (j,), 2 * nt)], raw,
                 [(jax.ShapeDtypeStruct((m_rows, dm), BF16), _tile_spec(tm, tn))],
                 epilogue, tm=tm, tn=tn, nt=nt,
                 scratch=[pltpu.VMEM((tm + halo, tn), F32)], vmem_mib=52, name="sconv_in",
                 make_side=make_side)
    return out


def _router_body(h_ref, w_ref, info_ref, cum_ref, carry):
    logits = jnp.dot(h_ref[...], w_ref[...].astype(BF16), preferred_element_type=F32)
    lane = lax.broadcasted_iota(jnp.int32, logits.shape, 1)
    neg = jnp.float32(-jnp.inf)
    lg = jnp.where(lane < N_EXPERTS, logits, neg)
    v1 = jnp.max(lg, axis=-1, keepdims=True)
    lane_f = lane.astype(F32)
    i1 = jnp.min(jnp.where(lg == v1, lane_f, float(LANES)), axis=-1, keepdims=True)
    lg2 = jnp.where(lane_f == i1, neg, lg)
    v2 = jnp.max(lg2, axis=-1, keepdims=True)
    i2 = jnp.min(jnp.where(lg2 == v2, lane_f, float(LANES)), axis=-1, keepdims=True)
    e2 = jnp.exp(v2 - v1)
    den = 1.0 + e2
    w1, w2 = 1.0 / den, e2 / den

    @pl.when(pl.program_id(0) == 0)
    def _():
        carry[...] = jnp.zeros(carry.shape, F32)

    tm = logits.shape[0]
    cnt = jnp.where(lane_f == i1, 1.0, 0.0) + jnp.where(lane_f == i2, 1.0, 0.0)
    strict = jnp.where(lax.broadcasted_iota(jnp.int32, (tm, tm), 1)
                       < lax.broadcasted_iota(jnp.int32, (tm, tm), 0), 1.0, 0.0).astype(BF16)
    before = jnp.dot(strict, cnt.astype(BF16), preferred_element_type=F32) + carry[...]
    r1 = jnp.sum(jnp.where(lane_f == i1, before, 0.0), axis=-1, keepdims=True)
    r2 = jnp.sum(jnp.where(lane_f == i2, before, 0.0), axis=-1, keepdims=True)
    fields = (i1, i2, w1, w2, r1, r2)
    info = jnp.zeros(logits.shape, F32)
    for q, val in enumerate(fields):
        info = jnp.where(lane == q, val, info)
    info_ref[...] = info[:, :info_ref.shape[1]]
    total = carry[...] + jnp.sum(cnt, axis=0, keepdims=True)
    carry[...] = total
    cum_ref[...] = total


def _router(h, w_router_padded):
    m_rows, k = h.shape
    tm = MOE_SUB
    return pl.pallas_call(
        _router_body,
        grid=(m_rows // tm,),
        in_specs=[pl.BlockSpec((tm, k), lambda i: (i, 0)), pl.BlockSpec((k, LANES), lambda i: (0, 0))],
        out_specs=[pl.BlockSpec((tm, SUBLANES), lambda i: (i, 0)),
                   pl.BlockSpec((None, 1, LANES), lambda i: (i, 0, 0))],
        out_shape=[jax.ShapeDtypeStruct((m_rows, SUBLANES), F32),
                   jax.ShapeDtypeStruct((m_rows // tm, 1, LANES), F32)],
        scratch_shapes=[pltpu.VMEM((1, LANES), F32)],
        compiler_params=_params(40, 1),
        name="router",
    )(h, w_router_padded)


def _dispatch_plan(info, cum):
    rows, sub, tok, n_exp = MOE_ROWS, MOE_SUB, TOKEN_BLOCK, N_EXPERTS
    per = rows // sub
    n_tok = info.shape[0]
    tbn = n_tok // tok
    nb_max = (2 * n_tok) // rows + n_exp
    nq = per * nb_max
    s_max = nq + n_exp * (tbn - 1)
    i32 = jnp.int32
    ids = info.astype(i32)
    i1, i2, r1, r2 = ids[:, 0], ids[:, 1], ids[:, 4], ids[:, 5]
    cb_fine = jnp.concatenate([jnp.zeros((1, n_exp), i32), cum[:, 0, :n_exp].astype(i32)])
    cb = cb_fine[::tok // sub]
    counts = cb[-1]
    nblk = (counts + rows - 1) // rows
    blk_end = jnp.cumsum(nblk)
    blk_off = blk_end - nblk
    nb = blk_end[-1]
    experts = jnp.arange(n_exp, dtype=i32)[:, None]
    row_off = (blk_off * rows)[None, :]
    slot1 = jnp.sum(jnp.where(i1[None, :] == experts, row_off.T, 0), axis=0) + r1
    slot2 = jnp.sum(jnp.where(i2[None, :] == experts, row_off.T, 0), axis=0) + r2

    b_idx = jnp.arange(nb_max, dtype=i32)
    b_used = b_idx < nb
    b_clamped = jnp.minimum(b_idx, nb - 1)
    blk_e = jnp.minimum(jnp.sum(b_clamped[:, None] >= blk_end[None, :], axis=1), n_exp - 1).astype(i32)
    prev_e = jnp.concatenate([jnp.full((1,), -1, i32), blk_e[:-1]])
    tail_rows = counts[blk_e] - (blk_end[blk_e] - 1 - blk_off[blk_e]) * rows
    half = (b_clamped == blk_end[blk_e] - 1) & (tail_rows <= rows // 2)
    blk_flag = (b_used.astype(i32) + 2 * (b_used & (blk_e != prev_e)).astype(i32)
                + 4 * (b_used & half).astype(i32))

    q = jnp.arange(nq, dtype=i32)
    e_q = jnp.minimum(jnp.sum(q[:, None] >= per * blk_end[None, :], axis=1), n_exp - 1)
    k_q = (q - per * blk_off[e_q])[:, None]
    lo = jnp.maximum(k_q * sub, cb[:-1].T[e_q])
    hi = jnp.minimum((k_q + 1) * sub, cb[1:].T[e_q])
    inter = (lo < hi) & (q < per * nb)[:, None]
    first_tb = jnp.arange(tbn, dtype=i32)[None, :] == 0
    touch = inter | (~jnp.any(inter, axis=1, keepdims=True) & first_tb)
    s_idx = jnp.arange(s_max, dtype=i32)

    def step_lists(mask):
        in_row = jnp.cumsum(mask.astype(i32), axis=1)
        row_end = jnp.cumsum(in_row[:, -1])
        n_steps = row_end[-1]
        s_clamped = jnp.minimum(s_idx, n_steps - 1)
        major = jnp.sum(row_end[None, :] <= s_clamped[:, None], axis=1).astype(i32)
        within = s_clamped - (row_end - in_row[:, -1])[major]
        minor = jnp.sum(in_row[major] <= within[:, None], axis=1).astype(i32)
        s_valid = s_idx < n_steps
        prev_m = jnp.concatenate([jnp.full((1,), -1, i32), major[:-1]])
        next_m = jnp.concatenate([major[1:], jnp.full((1,), -1, i32)])
        first = s_valid & (prev_m != major)
        last = s_valid & ((next_m != major) | (s_idx == n_steps - 1))
        return major, minor, s_valid, 2 * first.astype(i32) + 4 * last.astype(i32)

    g_q, g_tb, g_valid, g_edge = step_lists(touch)
    g_flag = (g_valid & inter[g_q, g_tb]).astype(i32) + g_edge

    c_q0 = jnp.minimum((row_off + cb_fine[:-1]) // sub, nq - 2).astype(i32).ravel()
    c_lo = (blk_off * rows).astype(i32)
    c_hi = (blk_end * rows).astype(i32)
    return dict(slot1=slot1, slot2=slot2, w1=info[:, 2], w2=info[:, 3],
                blk_e=blk_e, blk_row=b_clamped, blk_flag=blk_flag,
                g_q=g_q, g_tb=g_tb, g_flag=g_flag, c_q0=c_q0, c_lo=c_lo, c_hi=c_hi,
                nb_max=nb_max, s_max=s_max)


def _moe_gather(h, plan):
    n_tok, d = h.shape
    sub, tok = MOE_SUB, TOKEN_BLOCK
    p_rows = plan["nb_max"] * MOE_ROWS

    def body(q_ref, tb_ref, fl_ref, h_ref, s1_ref, s2_ref, w1_ref, w2_ref, hg_ref, ws_ref, acc, wacc):
        s = pl.program_id(0)
        flag = fl_ref[s]

        @pl.when((flag & 2) != 0)
        def _():
            acc[...] = jnp.zeros(acc.shape, F32)
            wacc[...] = jnp.zeros(wacc.shape, F32)

        @pl.when((flag & 1) != 0)
        def _():
            slot = q_ref[s] * sub + lax.broadcasted_iota(jnp.int32, (sub, tok), 0)
            d1 = s1_ref[...] == slot
            d2 = s2_ref[...] == slot
            sel = jnp.where(d1, 1.0, jnp.where(d2, 1.0, 0.0)).astype(BF16)
            t0 = pl.multiple_of(tb_ref[s] * tok, tok)
            acc[...] += jnp.dot(sel, h_ref[pl.ds(t0, tok), :], preferred_element_type=F32)
            wacc[...] += jnp.sum(jnp.where(d1, w1_ref[...], 0.0) + jnp.where(d2, w2_ref[...], 0.0),
                                 axis=1, keepdims=True)

        @pl.when((flag & 4) != 0)
        def _():
            hg_ref[...] = acc[...].astype(BF16)
            ws_ref[...] = wacc[...]

    tok_row = pl.BlockSpec((None, 1, tok), lambda s, q, tb, fl: (tb[s], 0, 0))
    grid_spec = pltpu.PrefetchScalarGridSpec(
        num_scalar_prefetch=3,
        grid=(plan["s_max"],),
        in_specs=[pl.BlockSpec((n_tok, d), lambda s, q, tb, fl: (0, 0), pipeline_mode=pl.Buffered(1)),
                  tok_row, tok_row, tok_row, tok_row],
        out_specs=[pl.BlockSpec((sub, d), lambda s, q, tb, fl: (q[s], 0)),
                   pl.BlockSpec((sub, 1), lambda s, q, tb, fl: (q[s], 0))],
        scratch_shapes=[pltpu.VMEM((sub, d), F32), pltpu.VMEM((sub, 1), F32)])
    rows = lambda a: a.reshape(n_tok // tok, 1, tok)
    return pl.pallas_call(
        body, grid_spec=grid_spec,
        out_shape=[jax.ShapeDtypeStruct((p_rows, d), BF16), jax.ShapeDtypeStruct((p_rows, 1), F32)],
        compiler_params=_params(48, 1), name="moe_gather",
    )(plan["g_q"], plan["g_tb"], plan["g_flag"], h,
      rows(plan["slot1"]), rows(plan["slot2"]), rows(plan["w1"]), rows(plan["w2"]))


def _grouped_mm(x, weights, lead, plan, row_inputs, out_cols, epilogue, *, tn, vmem_mib, name):
    p_rows, k = x.shape
    blk = MOE_ROWS
    half = blk // 2
    nw, nr = len(weights), len(row_inputs)

    def body(be_ref, br_ref, fl_ref, x_ref, *refs):
        w_refs, r_refs = refs[:nw], refs[nw:nw + nr]
        o_ref = refs[nw + nr]
        wb_refs = refs[nw + nr + 1:]
        flag = fl_ref[pl.program_id(1)]

        @pl.when((flag & 2) != 0)
        def _():
            for w_ref, wb in zip(w_refs, wb_refs):
                wb[...] = w_ref[...].astype(BF16)

        def compute(n_rows):
            xv = x_ref[0:n_rows, :]
            accs = [jnp.dot(xv, wb[...], preferred_element_type=F32) for wb in wb_refs]
            o_ref[0:n_rows, :] = epilogue(accs, [r[0:n_rows, :] for r in r_refs]).astype(BF16)

        @pl.when((flag & 5) == 1)
        def _():
            compute(blk)

        @pl.when((flag & 5) == 5)
        def _():
            compute(half)
            o_ref[half:blk, :] = jnp.zeros((blk - half, tn), BF16)

        @pl.when((flag & 1) == 0)
        def _():
            o_ref[...] = jnp.zeros((blk, tn), BF16)

    w_spec = pl.BlockSpec((None,) * (len(lead) + 1) + (k, tn),
                          lambda n, b, be, br, fl: (*lead, be[b], 0, n))
    grid_spec = pltpu.PrefetchScalarGridSpec(
        num_scalar_prefetch=3,
        grid=(out_cols // tn, plan["nb_max"]),
        in_specs=[pl.BlockSpec((blk, k), lambda n, b, be, br, fl: (br[b], 0))]
        + [w_spec] * nw
        + [pl.BlockSpec((blk, a.shape[1]), lambda n, b, be, br, fl: (br[b], 0)) for a in row_inputs],
        out_specs=pl.BlockSpec((blk, tn), lambda n, b, be, br, fl: (b, n)),
        scratch_shapes=[pltpu.VMEM((k, tn), BF16) for _ in weights])
    return pl.pallas_call(
        body, grid_spec=grid_spec,
        out_shape=jax.ShapeDtypeStruct((p_rows, out_cols), BF16),
        compiler_params=_params(vmem_mib, 2), name=name,
    )(plan["blk_e"], plan["blk_row"], plan["blk_flag"], x, *weights, *row_inputs)


def _moe_combine(yw, plan, x_res, gate, norm):
    n_tok, d = x_res.shape
    sub = MOE_SUB
    n_win = 2 * N_EXPERTS
    n_norm = len(norm)

    def body(q0_ref, lo_ref, hi_ref, *refs):
        y_refs = refs[:n_win]
        s1_ref, s2_ref, x_ref, g_ref = refs[n_win:n_win + 4]
        norm_refs = refs[n_win + 4:n_win + 4 + n_norm]
        o_refs = refs[n_win + 4 + n_norm:]
        t = pl.program_id(0)
        s1, s2 = s1_ref[...], s2_ref[...]
        col = lax.broadcasted_iota(jnp.int32, (sub, sub), 1)
        acc = None
        for e in range(N_EXPERTS):
            for j in range(2):
                slot = (q0_ref[t * N_EXPERTS + e] + j) * sub + col
                inside = (slot - lo_ref[e]).astype(jnp.uint32) < (hi_ref[e] - lo_ref[e]).astype(jnp.uint32)
                slot = jnp.where(inside, slot, -1)
                sel = jnp.where(s1 == slot, 1.0, jnp.where(s2 == slot, 1.0, 0.0)).astype(BF16)
                part = jnp.dot(sel, y_refs[2 * e + j][...], preferred_element_type=F32)
                acc = part if acc is None else acc + part
        x_new = x_ref[...] + g_ref[...] * acc
        if n_norm == 1:
            o_refs[0][...] = _rms(x_new, norm_refs[0][...])
        else:
            o_refs[0][...] = x_new
            o_refs[1][...] = _rms_mod(x_new, *[r[...] for r in norm_refs]).astype(BF16)

    def window_spec(e, j):
        return pl.BlockSpec((sub, d), lambda t, q0, lo, hi: (q0[t * N_EXPERTS + e] + j, 0))

    tok_col = pl.BlockSpec((sub, 1), lambda t, q0, lo, hi: (t, 0))
    tok_blk = pl.BlockSpec((sub, d), lambda t, q0, lo, hi: (t, 0))
    row = pl.BlockSpec((1, d), lambda t, q0, lo, hi: (0, 0))
    out_shape = [jax.ShapeDtypeStruct((n_tok, d), F32)]
    if n_norm > 1:
        out_shape.append(jax.ShapeDtypeStruct((n_tok, d), BF16))
    grid_spec = pltpu.PrefetchScalarGridSpec(
        num_scalar_prefetch=3,
        grid=(n_tok // sub,),
        in_specs=[window_spec(e, j) for e in range(N_EXPERTS) for j in range(2)]
        + [tok_col, tok_col, tok_blk, row] + [row] * n_norm,
        out_specs=[tok_blk] * len(out_shape))
    col_of = lambda a: a.reshape(n_tok, 1)
    return pl.pallas_call(
        body, grid_spec=grid_spec,
        out_shape=out_shape,
        compiler_params=_params(56, 1), name="moe_combine",
    )(plan["c_q0"], plan["c_lo"], plan["c_hi"], *([yw] * n_win),
      col_of(plan["slot1"]), col_of(plan["slot2"]), x_res, gate, *norm)


def _moe(h, x_res, gate, w_router, w1, w3, w2, kk, norm):
    w_r = jnp.pad(w_router[kk], ((0, 0), (0, LANES - N_EXPERTS)))
    info, cum = _router(h, w_r)
    plan = _dispatch_plan(info, cum)
    hg, w_slot = _moe_gather(h, plan)
    hid = _grouped_mm(hg, [w1, w3], (kk,), plan, [], w1.shape[-1],
                      lambda accs, _: jax.nn.silu(accs[0]) * accs[1],
                      tn=512, vmem_mib=58, name="moe_up")
    yw = _grouped_mm(hid, [w2], (kk,), plan, [w_slot], w2.shape[-1],
                     lambda accs, rows: accs[0] * rows[0],
                     tn=1024, vmem_mib=58, name="moe_down")
    return _moe_combine(yw, plan, x_res, gate, norm)


def kernel(x, c, ada_w, ada_b, norm1_g, norm2_g, pool_w_in, pool_w_grp, pool_scale, pool_w_out, conv_w_in, conv_b_in, conv_dw_w, conv_dw_b, conv_ln_g, conv_ln_b, conv_w_out, sgu_w_in, sgu_b_in, sgu_ln_g, sgu_ln_b, sgu_w_s, sgu_b_s, sgu_w_out, sconv_w_in, sconv_w, sconv_w_out, ffn_w1, ffn_w3, ffn_w2, moe_router, moe_w1, moe_w3, moe_w2, final_g):
    batch, seq, d = x.shape
    assert batch == 1
    depth = ada_w.shape[0]
    xs = x.reshape(seq, d)
    c_col = c.reshape(d, 1)
    ada_b3 = ada_b.reshape(depth, 1, ada_b.shape[-1])
    split = lambda mod: [mod[:, q * d:(q + 1) * d] for q in range(6)]

    def side_for(layer):
        if layer >= depth:
            return None
        return lambda outer, inner: _AdaSide(c_col, ada_w, ada_b3, layer, outer, inner)

    mod_next = _ada_mod(c_col, ada_w, ada_b3, 0)
    final_g = final_g.reshape(1, d)
    h = None
    out = None
    for i in range(depth):
        sh1, sc1, g1, sh2, sc2, g2 = split(mod_next)

        if h is None:
            h = _norm_mod(xs, norm1_g[i:i + 1], sc1, sh1)
        mixer, j = i % 4, i // 4
        mixer_side = side_for(i + 1) if i % 2 == 1 else None
        side_out = []
        if mixer == 0:
            a = _pool_in(h, pool_w_in, pool_w_grp, pool_scale[j:j + 1], j)
            w_out = pool_w_out
        elif mixer == 1:
            z, *side_out = _glu_in(h, conv_w_in, conv_b_in[j:j + 1], j, mixer_side)
            a = _conv_ln(z, conv_dw_w, conv_dw_b[j:j + 1], conv_ln_g[j:j + 1], conv_ln_b[j:j + 1], j)
            w_out = conv_w_out
        elif mixer == 2:
            a = _gelu_in(h, sgu_w_in, sgu_b_in[j:j + 1], j)
            a = _sgu_gate(a, sgu_ln_g[j:j + 1], sgu_ln_b[j:j + 1], sgu_w_s, sgu_b_s[j].T, j)
            w_out = sgu_w_out
        else:
            a, *side_out = _sconv_in(h, sconv_w_in, sconv_w, j, mixer_side)
            w_out = sconv_w_out
        xs, h = _mixer_out_norm(a, w_out, j, xs, g1, norm2_g[i:i + 1], sc2, sh2)

        kk = i // 2
        if i % 2 == 0:
            hid, *side_out = _swiglu_up(h, ffn_w1, ffn_w3, (kk,), tm=1024, tn=512, vmem_mib=52,
                                        name="ffn_up", make_side=side_for(i + 1))
            xs = _mm_residual(hid, ffn_w2, (kk,), xs, g2, tm=512, tn=512, vmem_mib=56, name="ffn_down")
            mod_next = side_out[0] if side_out else None
            h = None
        elif i + 1 < depth:
            mod_next = side_out[0]
            sh_n, sc_n = split(mod_next)[:2]
            xs, h = _moe(h, xs, g2, moe_router, moe_w1, moe_w3, moe_w2, kk,
                         (norm1_g[i + 1:i + 2], sc_n, sh_n))
        else:
            (out,) = _moe(h, xs, g2, moe_router, moe_w1, moe_w3, moe_w2, kk, (final_g,))
    if out is None:
        out = _final_norm(xs, final_g)
    return out.reshape(batch, seq, d)
```
